```python
import math
import jax, jax.numpy as jnp
from jax import lax
import numpy as np

D_MODEL = 1024
BATCH = 4
SEQ = 8192
DEPTH = 4
DEC_BATCH = 16
DEC_SEQ = 32
PAST_LEN = 2048

CHUNK = 64
N_A_LAYERS = DEPTH // 2
N_B_LAYERS = DEPTH - N_A_LAYERS
M_HEADS = 8
M_DV = D_MODEL // M_HEADS
M_DK = M_DV // 2
A_SPLITS = (M_HEADS * M_DK, 2 * M_HEADS * M_DK, 2 * M_HEADS * M_DK + M_HEADS * M_DV,
            2 * M_HEADS * M_DK + 2 * M_HEADS * M_DV, 2 * M_HEADS * M_DK + 2 * M_HEADS * M_DV + M_HEADS)
A_PROJ = 2 * M_HEADS * M_DK + 2 * M_HEADS * M_DV + 2 * M_HEADS
WINDOW = 128
WIN_CHUNKS = WINDOW // CHUNK
N_Q_HEADS = 16
N_KV_HEADS = 2
HEAD_DIM = 64
GROUP = N_Q_HEADS // N_KV_HEADS
D_ATT = N_Q_HEADS * HEAD_DIM
D_FF = ((8 * D_MODEL // 3 + 255) // 256) * 256
DEEPNORM_ALPHA = (2 * DEPTH) ** 0.25
DEEPNORM_BETA = (8 * DEPTH) ** -0.25
LN_EPS = 1e-5
HEAD_NORM_EPS = 1e-6

kernel_name = "yoco_mlstm_swa_sink_alibi_deepnorm_step"

F32 = jnp.float32


def layer_norm(x, g, b):
    xf = x.astype(F32)
    mu = jnp.mean(xf, -1, keepdims=True)
    var = jnp.mean(jnp.square(xf - mu), -1, keepdims=True)
    return ((xf - mu) * lax.rsqrt(var + LN_EPS) * g.astype(F32) + b.astype(F32)).astype(x.dtype)


def swiglu(x, w_gu, w_down):
    g, u = jnp.split(x @ w_gu, 2, axis=-1)
    return (jax.nn.silu(g) * u) @ w_down


def mlstm_chunk(q, k, v, ig, lf, C0, n0, m0):
    L = q.shape[2]
    b = jnp.cumsum(lf, axis=-1)
    causal = jnp.tril(jnp.ones((L, L), bool))
    log_d = jnp.where(causal, b[..., :, None] - b[..., None, :] + ig[..., None, :], -jnp.inf)
    inter = b + m0[..., None]
    m = jnp.maximum(inter, jnp.max(log_d, -1))
    d = jnp.exp(log_d - m[..., None])
    w_inter = jnp.exp(inter - m)
    qk = jnp.einsum('bhtd,bhsd->bhts', q, k) * d
    num = w_inter[..., None] * jnp.einsum('bhtd,bhde->bhte', q, C0) + jnp.einsum('bhts,bhse->bhte', qk, v)
    den = w_inter * jnp.einsum('bhtd,bhd->bht', q, n0) + jnp.sum(qk, -1)
    h = num / jnp.maximum(jnp.abs(den), jnp.exp(-m))[..., None]
    m_new = m[..., -1]
    w_end = jnp.exp(b[..., -1:] - b + ig - m_new[..., None])
    decay = jnp.exp(inter[..., -1] - m_new)
    C_new = decay[..., None, None] * C0 + jnp.einsum('bhs,bhsd,bhse->bhde', w_end, k, v)
    n_new = decay[..., None] * n0 + jnp.einsum('bhs,bhsd->bhd', w_end, k)
    return h, C_new, n_new, m_new


def mlstm_mixer(x, w_in, b_gate, g_norm, w_out, C0, n0, m0):
    B, T, _ = x.shape
    H = M_HEADS
    q, k, v, o, ig, fg = jnp.split(x @ w_in, A_SPLITS, axis=-1)
    heads = lambda a, dd: a.reshape(B, T, H, dd).transpose(0, 2, 1, 3).astype(F32)
    q = heads(q, M_DK)
    k = heads(k, M_DK) * (M_DK ** -0.5)
    v = heads(v, M_DV)
    gates = jnp.concatenate([ig, fg], -1).astype(F32) + b_gate.astype(F32)
    ig = gates[..., :H].transpose(0, 2, 1)
    lf = jax.nn.log_sigmoid(gates[..., H:]).transpose(0, 2, 1)
    L = min(CHUNK, T)
    nc = T // L

    def to_chunks(a):
        return jnp.moveaxis(a.reshape(a.shape[:2] + (nc, L) + a.shape[3:]), 2, 0)

    def step(carry, xs):
        C, n, m = carry
        h, C, n, m = mlstm_chunk(*xs, C, n, m)
        return (C, n, m), h

    (C, n, m), h = lax.scan(step, (C0.astype(F32), n0.astype(F32), m0.astype(F32)),
                            (to_chunks(q), to_chunks(k), to_chunks(v), to_chunks(ig), to_chunks(lf)))
    h = jnp.moveaxis(h, 0, 2).reshape(B, H, T, M_DV).transpose(0, 2, 1, 3)
    h = h * lax.rsqrt(jnp.mean(h * h, -1, keepdims=True) + HEAD_NORM_EPS)
    h = h.reshape(B, T, H * M_DV) * g_norm.astype(F32) * jax.nn.sigmoid(o.astype(F32))
    return h.astype(x.dtype) @ w_out, C, n, m


def alibi_slopes():
    return jnp.exp2(-8.0 * jnp.arange(1, N_Q_HEADS + 1, dtype=F32) / N_Q_HEADS)


def sink_attention(q, k, v, sinks, bias):
    s = jnp.einsum('bcqhgd,bckhd->bchgqk', q, k).astype(F32) * (HEAD_DIM ** -0.5) + bias
    sk = sinks.astype(F32).reshape(N_KV_HEADS, GROUP)[:, :, None, None]
    mx = jnp.maximum(jnp.max(s, -1, keepdims=True), sk)
    p = jnp.exp(s - mx)
    p = p / (jnp.sum(p, -1, keepdims=True) + jnp.exp(sk - mx))
    return jnp.einsum('bchgqk,bckhd->bcqhgd', p.astype(v.dtype), v)


def swa_prompt(x, k, v, w_q, sinks, w_out, slopes):
    B, T, _ = x.shape
    nc = T // CHUNK
    wk = WINDOW + CHUNK
    q = (x @ w_q).reshape(B, nc, CHUNK, N_KV_HEADS, GROUP, HEAD_DIM)
    pad = ((0, 0), (WINDOW, 0), (0, 0), (0, 0))
    kc = jnp.pad(k, pad).reshape(B, nc + WIN_CHUNKS, CHUNK, N_KV_HEADS, HEAD_DIM)
    vc = jnp.pad(v, pad).reshape(B, nc + WIN_CHUNKS, CHUNK, N_KV_HEADS, HEAD_DIM)
    kw = jnp.concatenate([kc[:, j:j + nc] for j in range(WIN_CHUNKS + 1)], axis=2)
    vw = jnp.concatenate([vc[:, j:j + nc] for j in range(WIN_CHUNKS + 1)], axis=2)
    dist = jnp.arange(CHUNK)[:, None] - jnp.arange(wk)[None, :] + WINDOW
    alibi = -slopes.reshape(N_KV_HEADS, GROUP, 1, 1) * jnp.abs(dist).astype(F32)
    key_pos = jnp.arange(nc)[:, None] * CHUNK - WINDOW + jnp.arange(wk)[None, :]
    bias = jnp.where((key_pos >= 0)[:, None, None, None, :], alibi[None], -jnp.inf)[None]
    o = sink_attention(q, kw, vw, sinks, bias)
    return o.reshape(B, T, D_ATT) @ w_out


def swa_sample(x, k_cache, v_cache, k, v, w_q, sinks, w_out, slopes):
    B, T, _ = x.shape
    W = k_cache.shape[1]
    q = (x @ w_q).reshape(B, 1, T, N_KV_HEADS, GROUP, HEAD_DIM)
    kw = jnp.concatenate([k_cache.astype(k.dtype), k], axis=1)[:, None]
    vw = jnp.concatenate([v_cache.astype(v.dtype), v], axis=1)[:, None]
    dist = jnp.arange(T)[:, None] - jnp.arange(W + T)[None, :] + W
    bias = (-slopes.reshape(N_KV_HEADS, GROUP, 1, 1) * jnp.abs(dist).astype(F32))[None, None]
    o = sink_attention(q, kw, vw, sinks, bias)
    return o.reshape(B, T, D_ATT) @ w_out


def trunk(x, C0s, n0s, m0s, k_cache, v_cache, w_in_a, b_gate_a, g_norm_a, w_out_a, w_kv,
          w_q_b, sinks_b, w_out_b, w_gu, w_down, ln_g, ln_b):
    slopes = alibi_slopes()
    Cs, ns, ms = [], [], []
    k_sh = v_sh = None
    for layer in range(DEPTH):
        if layer < N_A_LAYERS:
            mix, C, n, m = mlstm_mixer(x, w_in_a[layer], b_gate_a[layer], g_norm_a[layer], w_out_a[layer],
                                       C0s[layer], n0s[layer], m0s[layer])
            Cs.append(C)
            ns.append(n)
            ms.append(m)
        else:
            if layer == N_A_LAYERS:
                B, T, _ = x.shape
                kv = (x @ w_kv).reshape(B, T, 2, N_KV_HEADS, HEAD_DIM)
                k_sh, v_sh = kv[:, :, 0], kv[:, :, 1]
            j = layer - N_A_LAYERS
            if k_cache is None:
                mix = swa_prompt(x, k_sh, v_sh, w_q_b[j], sinks_b[j], w_out_b[j], slopes)
            else:
                mix = swa_sample(x, k_cache, v_cache, k_sh, v_sh, w_q_b[j], sinks_b[j], w_out_b[j], slopes)
        x = layer_norm(DEEPNORM_ALPHA * x + mix, ln_g[layer, 0], ln_b[layer, 0])
        x = layer_norm(DEEPNORM_ALPHA * x + swiglu(x, w_gu[layer], w_down[layer]), ln_g[layer, 1], ln_b[layer, 1])
    return x, jnp.stack(Cs), jnp.stack(ns), jnp.stack(ms), k_sh, v_sh


def setup_inputs(seed: int = 0) -> dict:
    key = jax.random.key(seed)
    ks = jax.random.split(key, 24)
    nrm = lambda k, shape, scale: jax.random.normal(k, shape, F32) * scale
    win_rows = min(WINDOW, PAST_LEN)
    b_i = nrm(ks[9], (N_A_LAYERS, M_HEADS), 0.1)
    b_f = jnp.linspace(3.0, 6.0, M_HEADS, dtype=F32)[None, :] + nrm(ks[10], (N_A_LAYERS, M_HEADS), 0.1)
    return {
        "x_prompt": nrm(ks[0], (BATCH, SEQ, D_MODEL), 1.0),
        "x_sample": nrm(ks[1], (DEC_BATCH, DEC_SEQ, D_MODEL), 1.0),
        "state_C": nrm(ks[2], (N_A_LAYERS, DEC_BATCH, M_HEADS, M_DK, M_DV), 0.1),
        "state_n": nrm(ks[3], (N_A_LAYERS, DEC_BATCH, M_HEADS, M_DK), 0.3),
        "state_m": nrm(ks[4], (N_A_LAYERS, DEC_BATCH, M_HEADS), 1.0),
        "cache_k": nrm(ks[5], (DEC_BATCH, win_rows, N_KV_HEADS, HEAD_DIM), 1.0),
        "cache_v": nrm(ks[6], (DEC_BATCH, win_rows, N_KV_HEADS, HEAD_DIM), 1.0),
        "w_in_a": nrm(ks[7], (N_A_LAYERS, D_MODEL, A_PROJ), D_MODEL ** -0.5),
        "b_gate_a": jnp.concatenate([b_i, b_f], axis=-1),
        "g_norm_a": 1.0 + nrm(ks[11], (N_A_LAYERS, M_HEADS * M_DV), 0.02),
        "w_out_a": nrm(ks[12], (N_A_LAYERS, M_HEADS * M_DV, D_MODEL), DEEPNORM_BETA * (M_HEADS * M_DV) ** -0.5),
        "w_kv": nrm(ks[13], (D_MODEL, 2 * N_KV_HEADS * HEAD_DIM), D_MODEL ** -0.5),
        "w_q_b": nrm(ks[14], (N_B_LAYERS, D_MODEL, D_ATT), D_MODEL ** -0.5),
        "sinks_b": nrm(ks[15], (N_B_LAYERS, N_Q_HEADS), 0.5),
        "w_out_b": nrm(ks[16], (N_B_LAYERS, D_ATT, D_MODEL), DEEPNORM_BETA * D_ATT ** -0.5),
        "w_gu": nrm(ks[17], (DEPTH, D_MODEL, 2 * D_FF), D_MODEL ** -0.5),
        "w_down": nrm(ks[18], (DEPTH, D_FF, D_MODEL), DEEPNORM_BETA * D_FF ** -0.5),
        "ln_g": 1.0 + nrm(ks[19], (DEPTH, 2, D_MODEL), 0.02),
        "ln_b": nrm(ks[20], (DEPTH, 2, D_MODEL), 0.02),
    }


def reference(x_prompt, x_sample, state_C, state_n, state_m, cache_k, cache_v, w_in_a, b_gate_a, g_norm_a,
              w_out_a, w_kv, w_q_b, sinks_b, w_out_b, w_gu, w_down, ln_g, ln_b):
    B = x_prompt.shape[0]
    C0 = jnp.zeros((N_A_LAYERS, B, M_HEADS, M_DK, M_DV), F32)
    n0 = jnp.zeros((N_A_LAYERS, B, M_HEADS, M_DK), F32)
    m0 = jnp.zeros((N_A_LAYERS, B, M_HEADS), F32)
    y_prompt, p_C, p_n, p_m, p_k, p_v = trunk(x_prompt, C0, n0, m0, None, None, w_in_a, b_gate_a, g_norm_a,
                                              w_out_a, w_kv, w_q_b, sinks_b, w_out_b, w_gu, w_down, ln_g, ln_b)
    y_sample, s_C, s_n, s_m, s_k, s_v = trunk(x_sample, state_C, state_n, state_m, cache_k, cache_v, w_in_a,
                                              b_gate_a, g_norm_a, w_out_a, w_kv, w_q_b, sinks_b, w_out_b,
                                              w_gu, w_down, ln_g, ln_b)
    rows = min(WINDOW, x_prompt.shape[1])
    return (y_prompt, y_sample, p_C, p_n, p_m, p_k[:, -rows:], p_v[:, -rows:], s_C, s_n, s_m, s_k, s_v)
```

```python
import functools

import jax
import jax.numpy as jnp
from jax import lax
from jax.experimental import pallas as pl
from jax.experimental.pallas import tpu as pltpu

F32 = jnp.float32
BF16 = jnp.bfloat16

CHUNK = 64
WINDOW = 128
M_HEADS = 8
M_DK = 64
M_DV = 128
N_Q_HEADS = 16
N_KV_HEADS = 2
HEAD_DIM = 64
GROUP = N_Q_HEADS // N_KV_HEADS
LN_EPS = 1e-5
HEAD_NORM_EPS = 1e-6

LANES = 128
VMEM_LIMIT_BYTES = 56 * 1024 * 1024

NEG_INF = float("-inf")
HIGHEST = lax.Precision.HIGHEST


def _const_spec(shape):
    nd = len(shape)
    return pl.BlockSpec(shape, lambda *_: (0,) * nd, pipeline_mode=pl.Buffered(1))


def _layer_norm(y, g, b):
    mu = jnp.mean(y, axis=-1, keepdims=True)
    yc = y - mu
    var = jnp.mean(yc * yc, axis=-1, keepdims=True)
    return yc * lax.rsqrt(var + LN_EPS) * g + b


def _dot(a, b):
    return jnp.dot(a, b, preferred_element_type=F32)


def _dot_nt(a, b):
    return lax.dot_general(a, b, (((1,), (1,)), ((), ())), preferred_element_type=F32)


def _dot_tn(a, b):
    return lax.dot_general(a, b, (((0,), (0,)), ((), ())), preferred_element_type=F32)


FF_BLOCK = 256


def _ffn_kernel(x_ref, wg_ref, wu_ref, wd_ref, g_ref, b_ref, o_ref, acc_ref, *, alpha, d_ff):
    x = x_ref[...]
    xb = x.astype(BF16)
    for j in range(d_ff // FF_BLOCK):
        sl = slice(j * FF_BLOCK, (j + 1) * FF_BLOCK)
        g = _dot(xb, wg_ref[:, sl])
        u = _dot(xb, wu_ref[:, sl])
        h = (g * jax.nn.sigmoid(g) * u).astype(BF16)
        part = _dot(h, wd_ref[sl, :])
        if j == 0:
            acc_ref[...] = part
        else:
            acc_ref[...] += part
    o_ref[...] = _layer_norm(alpha * x + acc_ref[...], g_ref[...], b_ref[...])


def _ffn(x2d, wg, wu, wd, ln_g, ln_b, *, alpha, tm):
    n, d = x2d.shape
    d_ff = wg.shape[1]
    assert n % tm == 0 and d_ff % FF_BLOCK == 0
    return pl.pallas_call(
        functools.partial(_ffn_kernel, alpha=alpha, d_ff=d_ff),
        grid=(n // tm,),
        in_specs=[
            pl.BlockSpec((tm, d), lambda i: (i, 0)),
            _const_spec(wg.shape),
            _const_spec(wu.shape),
            _const_spec(wd.shape),
            _const_spec(ln_g.shape),
            _const_spec(ln_b.shape),
        ],
        out_specs=pl.BlockSpec((tm, d), lambda i: (i, 0)),
        out_shape=jax.ShapeDtypeStruct((n, d), F32),
        scratch_shapes=[pltpu.VMEM((tm, d), F32)],
        compiler_params=pltpu.CompilerParams(
            dimension_semantics=("arbitrary",), vmem_limit_bytes=VMEM_LIMIT_BYTES),
        name="ffn",
    )(x2d, wg, wu, wd, ln_g, ln_b)


def _kv_kernel(x_ref, w_ref, o_ref):
    o_ref[...] = _dot(x_ref[...].astype(BF16), w_ref[...])


def _kv_proj(x2d, w_kv, *, tm):
    n, d = x2d.shape
    nk = w_kv.shape[1]
    assert n % tm == 0
    return pl.pallas_call(
        _kv_kernel,
        grid=(n // tm,),
        in_specs=[pl.BlockSpec((tm, d), lambda i: (i, 0)), _const_spec(w_kv.shape)],
        out_specs=pl.BlockSpec((tm, nk), lambda i: (i, 0)),
        out_shape=jax.ShapeDtypeStruct((n, nk), F32),
        compiler_params=pltpu.CompilerParams(
            dimension_semantics=("arbitrary",), vmem_limit_bytes=VMEM_LIMIT_BYTES),
        name="kv_proj",
    )(x2d, w_kv)


A_QOFF = 0
A_KOFF = M_HEADS * LANES
A_VOFF = 2 * M_HEADS * LANES
A_OOFF = 3 * M_HEADS * LANES
A_GOFF = 4 * M_HEADS * LANES
A_COLS = A_GOFF + 2 * LANES
A_COL_BLOCK = 512


def _mlstm_kernel(x_ref, c0_ref, n0_ref, m0_ref, w_ref, bg_ref, gn_ref, wo_ref, lg_ref, lb_ref,
                  y_ref, cout_ref, mout_ref,
                  q_s, k_s, v_s, o_s, g_s, h_s, hb_s, c_s, m_s,
                  *, alpha, L, bb_n, tt, n_j):
    j = pl.program_id(1)
    rows = bb_n * tt
    units_per_b = tt // L
    hd = M_HEADS * LANES

    @pl.when(j == 0)
    def _():
        lane = lax.broadcasted_iota(jnp.int32, (1, 1, 1, LANES), 3)
        c_s[:, :, :, 0:LANES] = c0_ref[...]
        c_s[:, :, :, LANES:2 * LANES] = jnp.where(lane == 0, n0_ref[...], 0.0)
        m_s[...] = m0_ref[...]

    x = x_ref[...]
    xb = x.astype(BF16)
    for cb in range(0, hd, A_COL_BLOCK):
        q_s[:, cb:cb + A_COL_BLOCK] = _dot(xb, w_ref[:, A_QOFF + cb:A_QOFF + cb + A_COL_BLOCK]).astype(BF16)
        k_s[:, cb:cb + A_COL_BLOCK] = _dot(xb, w_ref[:, A_KOFF + cb:A_KOFF + cb + A_COL_BLOCK])
        v_s[:, cb:cb + A_COL_BLOCK] = _dot(xb, w_ref[:, A_VOFF + cb:A_VOFF + cb + A_COL_BLOCK]).astype(BF16)
        o_s[:, cb:cb + A_COL_BLOCK] = _dot(xb, w_ref[:, A_OOFF + cb:A_OOFF + cb + A_COL_BLOCK])
    g_s[...] = _dot(xb, w_ref[:, A_GOFF:A_GOFF + 2 * LANES]) + bg_ref[...]

    row_i = lax.broadcasted_iota(jnp.int32, (L, L), 0)
    col_i = lax.broadcasted_iota(jnp.int32, (L, L), 1)
    causal = row_i >= col_i
    tri = causal.astype(F32)
    ones_col = jnp.where(lax.broadcasted_iota(jnp.int32, (L, LANES), 1) == 0, 1.0, 0.0).astype(BF16)

    def unit(u, carry):
        r0 = pl.multiple_of(u * L, L)
        bb = u // units_per_b if bb_n > 1 else 0
        gi = g_s[pl.ds(r0, L), 0:LANES]
        gf = g_s[pl.ds(r0, L), LANES:2 * LANES]
        lf = jnp.minimum(gf, 0.0) - jnp.log1p(jnp.exp(-jnp.abs(gf)))
        bc = lax.dot_general(tri, lf, (((1,), (0,)), ((), ())), precision=HIGHEST,
                             preferred_element_type=F32)
        rs = gi - bc
        rs_t = rs.T
        b_last = bc[L - 1:L, :]
        for h in range(M_HEADS):
            hs = slice(h * LANES, (h + 1) * LANES)
            qh = q_s[pl.ds(r0, L), hs]
            kh = k_s[pl.ds(r0, L), hs]
            vh = v_s[pl.ds(r0, L), hs]
            m0 = m_s[bb, h, 0:1, 0:1]
            c_aug = c_s[bb, h]
            b_col = bc[:, h:h + 1]
            logd = jnp.where(causal, b_col + rs_t[h:h + 1, :], NEG_INF)
            inter = b_col + m0
            mx = jnp.maximum(inter, jnp.max(logd, axis=1, keepdims=True))
            d = jnp.exp(logd - mx)
            w_inter = jnp.exp(inter - mx)
            s = _dot_nt(qh, kh.astype(BF16))
            p = (s * d).astype(BF16)
            v_aug = jnp.concatenate([vh, ones_col], axis=1)
            numden = w_inter * _dot(qh[:, 0:M_DK], c_aug.astype(BF16)) + _dot(p, v_aug)
            num = numden[:, 0:M_DV]
            den = numden[:, M_DV:M_DV + 1]
            h_s[pl.ds(r0, L), hs] = num / jnp.maximum(jnp.abs(den), jnp.exp(-mx))
            m_new = mx[L - 1:L, :]
            bl = b_last[:, h:h + 1]
            w_end = jnp.exp(rs[:, h:h + 1] + (bl - m_new))
            decay = jnp.exp(bl + m0 - m_new)
            wk = (kh[:, 0:M_DK] * w_end).astype(BF16)
            c_s[bb, h] = decay * c_aug + _dot_tn(wk, v_aug)
            m_s[bb, h] = jnp.broadcast_to(m_new, (8, LANES))
        return carry

    lax.fori_loop(0, rows // L, unit, 0)

    for h in range(M_HEADS):
        hs = slice(h * LANES, (h + 1) * LANES)
        hh = h_s[:, hs]
        hn = hh * lax.rsqrt(jnp.mean(hh * hh, axis=-1, keepdims=True) + HEAD_NORM_EPS)
        hb_s[:, hs] = (hn * gn_ref[:, hs] * jax.nn.sigmoid(o_s[:, hs])).astype(BF16)
    mix = _dot(hb_s[...], wo_ref[...])
    y_ref[...] = _layer_norm(alpha * x + mix, lg_ref[...], lb_ref[...])

    @pl.when(j == n_j - 1)
    def _():
        cout_ref[...] = c_s[...]
        mout_ref[...] = m_s[...]


def _mlstm_layer(x2d, c0, n0, m0, w_all, b_gate, g_norm, w_out, ln_g, ln_b, *, alpha, batch, seq, L, bb_n, tt):
    n, d = x2d.shape
    assert n == batch * seq and seq % tt == 0 and tt % L == 0 and batch % bb_n == 0
    assert bb_n == 1 or tt == seq
    n_j = seq // tt
    rows = bb_n * tt
    hd = M_HEADS * LANES
    kern = functools.partial(_mlstm_kernel, alpha=alpha, L=L, bb_n=bb_n, tt=tt, n_j=n_j)
    st4 = lambda bi, j: (bi, 0, 0, 0)
    y, c_out, m_out = pl.pallas_call(
        kern,
        grid=(batch // bb_n, n_j),
        in_specs=[
            pl.BlockSpec((rows, d), lambda bi, j: (bi * n_j + j, 0)),
            pl.BlockSpec((bb_n, M_HEADS, M_DK, M_DV), st4),
            pl.BlockSpec((bb_n, M_HEADS, M_DK, 1), st4),
            pl.BlockSpec((bb_n, M_HEADS, 8, LANES), st4),
            _const_spec(w_all.shape),
            _const_spec(b_gate.shape),
            _const_spec(g_norm.shape),
            _const_spec(w_out.shape),
            _const_spec(ln_g.shape),
            _const_spec(ln_b.shape),
        ],
        out_specs=[
            pl.BlockSpec((rows, d), lambda bi, j: (bi * n_j + j, 0)),
            pl.BlockSpec((bb_n, M_HEADS, M_DK, 2 * LANES), st4),
            pl.BlockSpec((bb_n, M_HEADS, 8, LANES), st4),
        ],
        out_shape=[
            jax.ShapeDtypeStruct((n, d), F32),
            jax.ShapeDtypeStruct((batch, M_HEADS, M_DK, 2 * LANES), F32),
            jax.ShapeDtypeStruct((batch, M_HEADS, 8, LANES), F32),
        ],
        scratch_shapes=[
            pltpu.VMEM((rows, hd), BF16),
            pltpu.VMEM((rows, hd), F32),
            pltpu.VMEM((rows, hd), BF16),
            pltpu.VMEM((rows, hd), F32),
            pltpu.VMEM((rows, 2 * LANES), F32),
            pltpu.VMEM((rows, hd), F32),
            pltpu.VMEM((rows, hd), BF16),
            pltpu.VMEM((bb_n, M_HEADS, M_DK, 2 * LANES), F32),
            pltpu.VMEM((bb_n, M_HEADS, 8, LANES), F32),
        ],
        compiler_params=pltpu.CompilerParams(
            dimension_semantics=("arbitrary", "arbitrary"), vmem_limit_bytes=VMEM_LIMIT_BYTES),
        name="mlstm_layer",
    )(x2d, c0, n0, m0, w_all, b_gate, g_norm, w_out, ln_g, ln_b)
    c_new = c_out[..., 0:M_DV]
    n_new = c_out[..., M_DV]
    m_new = m_out[:, :, 0, 0]
    return y, c_new, n_new, m_new


def _prep_mlstm_weights(w_in, b_gate, g_norm, w_out):
    d = w_in.shape[0]
    hk = M_HEADS * M_DK
    hv = M_HEADS * M_DV
    wq = w_in[:, 0:hk].reshape(d, M_HEADS, M_DK)
    wk = w_in[:, hk:2 * hk].reshape(d, M_HEADS, M_DK) * (M_DK ** -0.5)
    pad = ((0, 0), (0, 0), (0, LANES - M_DK))
    wq = jnp.pad(wq, pad).reshape(d, M_HEADS * LANES)
    wk = jnp.pad(wk, pad).reshape(d, M_HEADS * LANES)
    wv = w_in[:, 2 * hk:2 * hk + hv]
    wo = w_in[:, 2 * hk + hv:2 * hk + 2 * hv]
    wi = jnp.pad(w_in[:, 2 * hk + 2 * hv:2 * hk + 2 * hv + M_HEADS], ((0, 0), (0, LANES - M_HEADS)))
    wf = jnp.pad(w_in[:, 2 * hk + 2 * hv + M_HEADS:], ((0, 0), (0, LANES - M_HEADS)))
    w_all = jnp.concatenate([wq, wk, wv, wo, wi, wf], axis=1).astype(BF16)
    bg = jnp.concatenate([jnp.pad(b_gate[0:M_HEADS], (0, LANES - M_HEADS)),
                          jnp.pad(b_gate[M_HEADS:], (0, LANES - M_HEADS))]).astype(F32)[None, :]
    return w_all, bg, g_norm.astype(F32)[None, :], w_out.astype(BF16)


PAIRS = GROUP // 2


def _attn_unit(q_rows, k_lo, k_hi, v_lo, v_hi, bias_lo, bias_hi, sink_lo, sink_hi, key_ok):
    outs = []
    for k_op, v_op, bias, sink in ((k_lo, v_lo, bias_lo, sink_lo), (k_hi, v_hi, bias_hi, sink_hi)):
        s = _dot_nt(q_rows, k_op) + bias
        if key_ok is not None:
            s = jnp.where(key_ok, s, NEG_INF)
        mx = jnp.maximum(jnp.max(s, axis=-1, keepdims=True), sink)
        p = jnp.exp(s - mx)
        den = jnp.sum(p, axis=-1, keepdims=True) + jnp.exp(sink - mx)
        outs.append(_dot(p.astype(BF16), v_op) / den)
    return outs[0] + outs[1]


def _split_kv(kv):
    lane = lax.broadcasted_iota(jnp.int32, (1, LANES), 1)
    low = lane < HEAD_DIM
    res = []
    kk = kv[:, 0:LANES]
    vv = kv[:, LANES:2 * LANES]
    kk_r = pltpu.roll(kk, HEAD_DIM, 1)
    vv_r = pltpu.roll(vv, HEAD_DIM, 1)
    z = jnp.zeros_like(kk)
    res.append((jnp.where(low, kk, z), jnp.where(low, z, kk_r), jnp.where(low, vv, z), jnp.where(low, z, vv_r)))
    res.append((jnp.where(low, kk_r, z), jnp.where(low, z, kk), jnp.where(low, vv_r, z), jnp.where(low, z, vv)))
    return [tuple(a.astype(BF16) for a in grp) for grp in res]


def _swa_prompt_kernel(x_ref, kvc_ref, kvp_ref, wq_ref, wo_ref, bias_ref, sink_ref, lg_ref, lb_ref,
                       y_ref, q_s, o_s, kw_s, *, alpha, tq):
    j = pl.program_id(1)
    nk = WINDOW + CHUNK
    x = x_ref[...]
    xb = x.astype(BF16)
    q_s[...] = _dot(xb, wq_ref[...]).astype(BF16)
    kv_all = jnp.concatenate([kvp_ref[...], kvc_ref[...]], axis=0)
    for g, grp in enumerate(_split_kv(kv_all)):
        for i, a in enumerate(grp):
            kw_s[g, i] = a
    col = lax.broadcasted_iota(jnp.int32, (1, nk), 1)
    for c in range(tq // CHUNK):
        r0 = c * CHUNK
        key_ok = None
        if r0 < WINDOW:
            key_ok = (j * tq + r0 - WINDOW + col) >= 0
        for g in range(N_KV_HEADS):
            q_rows = jnp.concatenate(
                [q_s[r0:r0 + CHUNK, (g * PAIRS + pp) * LANES:(g * PAIRS + pp + 1) * LANES] for pp in range(PAIRS)],
                axis=0)
            ks = [kw_s[g, i, r0:r0 + nk, :] for i in range(4)]
            o = _attn_unit(q_rows, ks[0], ks[1], ks[2], ks[3], bias_ref[g, 0], bias_ref[g, 1],
                           sink_ref[g, 0], sink_ref[g, 1], key_ok)
            for pp in range(PAIRS):
                o_s[r0:r0 + CHUNK, (g * PAIRS + pp) * LANES:(g * PAIRS + pp + 1) * LANES] = (
                    o[pp * CHUNK:(pp + 1) * CHUNK, :].astype(BF16))
    mix = _dot(o_s[...], wo_ref[...])
    y_ref[...] = _layer_norm(alpha * x + mix, lg_ref[...], lb_ref[...])


def _swa_prompt_layer(x2d, kv2d, wq, wo, bias, sink, ln_g, ln_b, *, alpha, batch, seq, tq):
    n, d = x2d.shape
    assert n == batch * seq and seq % tq == 0 and tq % WINDOW == 0
    n_j = seq // tq
    nkv = kv2d.shape[1]
    per = tq // WINDOW
    return pl.pallas_call(
        functools.partial(_swa_prompt_kernel, alpha=alpha, tq=tq),
        grid=(batch, n_j),
        in_specs=[
            pl.BlockSpec((tq, d), lambda b, j: (b * n_j + j, 0)),
            pl.BlockSpec((tq, nkv), lambda b, j: (b * n_j + j, 0)),
            pl.BlockSpec((WINDOW, nkv), lambda b, j: (jnp.maximum((b * n_j + j) * per - 1, 0), 0)),
            _const_spec(wq.shape),
            _const_spec(wo.shape),
            _const_spec(bias.shape),
            _const_spec(sink.shape),
            _const_spec(ln_g.shape),
            _const_spec(ln_b.shape),
        ],
        out_specs=pl.BlockSpec((tq, d), lambda b, j: (b * n_j + j, 0)),
        out_shape=jax.ShapeDtypeStruct((n, d), F32),
        scratch_shapes=[
            pltpu.VMEM((tq, d), BF16),
            pltpu.VMEM((tq, d), BF16),
            pltpu.VMEM((N_KV_HEADS, 4, WINDOW + tq, LANES), BF16),
        ],
        compiler_params=pltpu.CompilerParams(
            dimension_semantics=("arbitrary", "arbitrary"), vmem_limit_bytes=VMEM_LIMIT_BYTES),
        name="swa_prompt",
    )(x2d, kv2d, kv2d, wq, wo, bias, sink, ln_g, ln_b)


def _swa_sample_kernel(x_ref, kvn_ref, kc_ref, vc_ref, wq_ref, wo_ref, bias_ref, sink_ref, lg_ref, lb_ref,
                       y_ref, q_s, o_s, *, alpha, batch, seq):
    x = x_ref[...]
    xb = x.astype(BF16)
    q_s[...] = _dot(xb, wq_ref[...]).astype(BF16)
    for b in range(batch):
        r0 = b * seq
        kv_new = kvn_ref[r0:r0 + seq, :]
        kv_old = jnp.concatenate([kc_ref[b], vc_ref[b]], axis=1)
        groups = _split_kv(jnp.concatenate([kv_old, kv_new], axis=0))
        for g in range(N_KV_HEADS):
            q_rows = jnp.concatenate(
                [q_s[r0:r0 + seq, (g * PAIRS + pp) * LANES:(g * PAIRS + pp + 1) * LANES] for pp in range(PAIRS)],
                axis=0)
            ks = groups[g]
            o = _attn_unit(q_rows, ks[0], ks[1], ks[2], ks[3], bias_ref[g, 0], bias_ref[g, 1],
                           sink_ref[g, 0], sink_ref[g, 1], None)
            for pp in range(PAIRS):
                o_s[r0:r0 + seq, (g * PAIRS + pp) * LANES:(g * PAIRS + pp + 1) * LANES] = (
                    o[pp * seq:(pp + 1) * seq, :].astype(BF16))
    mix = _dot(o_s[...], wo_ref[...])
    y_ref[...] = _layer_norm(alpha * x + mix, lg_ref[...], lb_ref[...])


def _swa_sample_layer(x2d, kv_new, k_cache, v_cache, wq, wo, bias, sink, ln_g, ln_b, *, alpha, batch, seq):
    n, d = x2d.shape
    assert n == batch * seq
    args = (x2d, kv_new, k_cache, v_cache, wq, wo, bias, sink, ln_g, ln_b)
    return pl.pallas_call(
        functools.partial(_swa_sample_kernel, alpha=alpha, batch=batch, seq=seq),
        grid=(1,),
        in_specs=[_const_spec(a.shape) for a in args],
        out_specs=pl.BlockSpec((n, d), lambda i: (0, 0)),
        out_shape=jax.ShapeDtypeStruct((n, d), F32),
        scratch_shapes=[pltpu.VMEM((n, d), BF16), pltpu.VMEM((n, d), BF16)],
        compiler_params=pltpu.CompilerParams(
            dimension_semantics=("arbitrary",), vmem_limit_bytes=VMEM_LIMIT_BYTES),
        name="swa_sample",
    )(*args)


def _attn_tables(sinks, ql, nk):
    slopes = jnp.exp2(-8.0 * jnp.arange(1, N_Q_HEADS + 1, dtype=F32) / N_Q_HEADS)
    dist = jnp.abs(jnp.arange(ql)[:, None] - jnp.arange(nk)[None, :] + (nk - ql)).astype(F32)
    head = (jnp.arange(N_KV_HEADS)[:, None, None] * GROUP + 2 * jnp.arange(PAIRS)[None, None, :]
            + jnp.arange(2)[None, :, None])
    bias = -slopes[head][..., None, None] * dist
    sink = jnp.broadcast_to(sinks.astype(F32)[head][..., None, None], head.shape + (ql, 1))
    return (bias.reshape(N_KV_HEADS, 2, PAIRS * ql, nk), sink.reshape(N_KV_HEADS, 2, PAIRS * ql, 1))


def _trunk(x, c0s, n0s, m0s, k_cache, v_cache, params, *, is_prompt):
    (w_in_a, b_gate_a, g_norm_a, w_out_a, w_kv, w_q_b, sinks_b, w_out_b, w_gu, w_down, ln_g, ln_b) = params
    batch, seq, d = x.shape
    depth = w_gu.shape[0]
    n_a = w_in_a.shape[0]
    alpha = (2 * depth) ** 0.25
    d_ff = w_down.shape[1]
    n = batch * seq
    x2d = x.reshape(n, d)
    tm = min(512, n)
    if is_prompt:
        L, bb_n, tt = 128, 1, 512
    else:
        L, bb_n, tt = seq, 8, seq
    cs, ns, ms = [], [], []
    kv2d = None
    for layer in range(depth):
        row = lambda a: a.astype(F32)[None, :]
        if layer < n_a:
            w_all, bg, gn, wo = _prep_mlstm_weights(w_in_a[layer], b_gate_a[layer], g_norm_a[layer], w_out_a[layer])
            m0 = jnp.broadcast_to(m0s[layer].astype(F32)[:, :, None, None], (batch, M_HEADS, 8, LANES))
            x2d, c, nn, m = _mlstm_layer(
                x2d, c0s[layer].astype(F32), n0s[layer].astype(F32)[..., None], m0, w_all, bg, gn, wo,
                row(ln_g[layer, 0]), row(ln_b[layer, 0]), alpha=alpha, batch=batch, seq=seq, L=L, bb_n=bb_n, tt=tt)
            cs.append(c)
            ns.append(nn)
            ms.append(m)
        else:
            jb = layer - n_a
            if kv2d is None:
                kv2d = _kv_proj(x2d, w_kv.astype(BF16), tm=tm)
            wq = (w_q_b[jb] * (HEAD_DIM ** -0.5)).astype(BF16)
            wo = w_out_b[jb].astype(BF16)
            if is_prompt:
                bias, sink = _attn_tables(sinks_b[jb], CHUNK, WINDOW + CHUNK)
                x2d = _swa_prompt_layer(x2d, kv2d, wq, wo, bias, sink, row(ln_g[layer, 0]), row(ln_b[layer, 0]),
                                        alpha=alpha, batch=batch, seq=seq, tq=512)
            else:
                w_rows = k_cache.shape[1]
                bias, sink = _attn_tables(sinks_b[jb], seq, w_rows + seq)
                x2d = _swa_sample_layer(
                    x2d, kv2d, k_cache.astype(F32).reshape(batch, w_rows, N_KV_HEADS * HEAD_DIM),
                    v_cache.astype(F32).reshape(batch, w_rows, N_KV_HEADS * HEAD_DIM), wq, wo, bias, sink,
                    row(ln_g[layer, 0]), row(ln_b[layer, 0]), alpha=alpha, batch=batch, seq=seq)
        x2d = _ffn(x2d, w_gu[layer][:, :d_ff].astype(BF16), w_gu[layer][:, d_ff:].astype(BF16),
                   w_down[layer].astype(BF16), row(ln_g[layer, 1]), row(ln_b[layer, 1]), alpha=alpha, tm=tm)
    kv = kv2d.reshape(batch, seq, 2, N_KV_HEADS, HEAD_DIM)
    return (x2d.reshape(batch, seq, d), jnp.stack(cs), jnp.stack(ns), jnp.stack(ms), kv[:, :, 0], kv[:, :, 1])


def kernel(x_prompt, x_sample, state_C, state_n, state_m, cache_k, cache_v, w_in_a, b_gate_a, g_norm_a,
           w_out_a, w_kv, w_q_b, sinks_b, w_out_b, w_gu, w_down, ln_g, ln_b):
    params = (w_in_a, b_gate_a, g_norm_a, w_out_a, w_kv, w_q_b, sinks_b, w_out_b, w_gu, w_down, ln_g, ln_b)
    bp = x_prompt.shape[0]
    n_a = w_in_a.shape[0]
    c0 = jnp.zeros((n_a, bp, M_HEADS, M_DK, M_DV), F32)
    n0 = jnp.zeros((n_a, bp, M_HEADS, M_DK), F32)
    m0 = jnp.zeros((n_a, bp, M_HEADS), F32)
    y_p, p_c, p_n, p_m, p_k, p_v = _trunk(x_prompt, c0, n0, m0, None, None, params, is_prompt=True)
    y_s, s_c, s_n, s_m, s_k, s_v = _trunk(x_sample, state_C, state_n, state_m, cache_k, cache_v, params,
                                          is_prompt=False)
    rows = min(WINDOW, x_prompt.shape[1])
    return (y_p, y_s, p_c, p_n, p_m, p_k[:, -rows:], p_v[:, -rows:], s_c, s_n, s_m, s_k, s_v)
```

```python
import functools

import jax
import jax.numpy as jnp
from jax import lax
from jax.experimental import pallas as pl
from jax.experimental.pallas import tpu as pltpu

F32 = jnp.float32
BF16 = jnp.bfloat16

CHUNK = 64
WINDOW = 128
M_HEADS = 8
M_DK = 64
M_DV = 128
N_Q_HEADS = 16
N_KV_HEADS = 2
HEAD_DIM = 64
GROUP = N_Q_HEADS // N_KV_HEADS
LN_EPS = 1e-5
HEAD_NORM_EPS = 1e-6

LANES = 128
VMEM_LIMIT_BYTES = 56 * 1024 * 1024

NEG_INF = float("-inf")
HIGHEST = lax.Precision.HIGHEST


def _const_spec(shape):
    nd = len(shape)
    return pl.BlockSpec(shape, lambda *_: (0,) * nd, pipeline_mode=pl.Buffered(1))


def _layer_norm(y, g, b):
    mu = jnp.mean(y, axis=-1, keepdims=True)
    yc = y - mu
    var = jnp.mean(yc * yc, axis=-1, keepdims=True)
    return yc * lax.rsqrt(var + LN_EPS) * g + b


def _dot(a, b):
    return jnp.dot(a, b, preferred_element_type=F32)


def _dot_nt(a, b):
    return lax.dot_general(a, b, (((1,), (1,)), ((), ())), preferred_element_type=F32)


def _dot_tn(a, b):
    return lax.dot_general(a, b, (((0,), (0,)), ((), ())), preferred_element_type=F32)


FF_BLOCK = 256


def _ffn_kernel(x_ref, wg_ref, wu_ref, wd_ref, g_ref, b_ref, o_ref, acc_ref, *, alpha, d_ff):
    x = x_ref[...]
    xb = x.astype(BF16)
    for j in range(d_ff // FF_BLOCK):
        sl = slice(j * FF_BLOCK, (j + 1) * FF_BLOCK)
        g = _dot(xb, wg_ref[:, sl])
        u = _dot(xb, wu_ref[:, sl])
        h = (g * jax.nn.sigmoid(g) * u).astype(BF16)
        part = _dot(h, wd_ref[sl, :])
        if j == 0:
            acc_ref[...] = part
        else:
            acc_ref[...] += part
    o_ref[...] = _layer_norm(alpha * x + acc_ref[...], g_ref[...], b_ref[...])


def _ffn(x2d, wg, wu, wd, ln_g, ln_b, *, alpha, tm):
    n, d = x2d.shape
    d_ff = wg.shape[1]
    assert n % tm == 0 and d_ff % FF_BLOCK == 0
    return pl.pallas_call(
        functools.partial(_ffn_kernel, alpha=alpha, d_ff=d_ff),
        grid=(n // tm,),
        in_specs=[
            pl.BlockSpec((tm, d), lambda i: (i, 0)),
            _const_spec(wg.shape),
            _const_spec(wu.shape),
            _const_spec(wd.shape),
            _const_spec(ln_g.shape),
            _const_spec(ln_b.shape),
        ],
        out_specs=pl.BlockSpec((tm, d), lambda i: (i, 0)),
        out_shape=jax.ShapeDtypeStruct((n, d), F32),
        scratch_shapes=[pltpu.VMEM((tm, d), F32)],
        compiler_params=pltpu.CompilerParams(
            dimension_semantics=("arbitrary",), vmem_limit_bytes=VMEM_LIMIT_BYTES),
        name="ffn",
    )(x2d, wg, wu, wd, ln_g, ln_b)


def _kv_kernel(x_ref, w_ref, wvt_ref, o_ref, vt_ref):
    xb = x_ref[...].astype(BF16)
    o_ref[...] = _dot(xb, w_ref[...])
    vt_ref[...] = _dot_nt(wvt_ref[...], xb).astype(BF16)


def _kv_proj(x2d, w_kv, w_vt, *, tm):
    n, d = x2d.shape
    nk = w_kv.shape[1]
    nv = w_vt.shape[0]
    assert n % tm == 0
    return pl.pallas_call(
        _kv_kernel,
        grid=(n // tm,),
        in_specs=[pl.BlockSpec((tm, d), lambda i: (i, 0)), _const_spec(w_kv.shape), _const_spec(w_vt.shape)],
        out_specs=[pl.BlockSpec((tm, nk), lambda i: (i, 0)), pl.BlockSpec((nv, tm), lambda i: (0, i))],
        out_shape=[jax.ShapeDtypeStruct((n, nk), F32), jax.ShapeDtypeStruct((nv, n), BF16)],
        compiler_params=pltpu.CompilerParams(
            dimension_semantics=("arbitrary",), vmem_limit_bytes=VMEM_LIMIT_BYTES),
        name="kv_proj",
    )(x2d, w_kv, w_vt)


A_QOFF = 0
A_KOFF = M_HEADS * LANES
A_VOFF = 2 * M_HEADS * LANES
A_OOFF = 3 * M_HEADS * LANES
A_GOFF = 4 * M_HEADS * LANES
A_COLS = A_GOFF + 2 * LANES
A_COL_BLOCK = 512


def _mlstm_kernel(x_ref, c0_ref, n0_ref, m0_ref, w_ref, bg_ref, gn_ref, wo_ref, lg_ref, lb_ref,
                  y_ref, cout_ref, mout_ref,
                  q_s, k_s, v_s, o_s, g_s, h_s, hb_s, c_s, m_s,
                  *, alpha, L, bb_n, tt, n_j):
    j = pl.program_id(1)
    rows = bb_n * tt
    units_per_b = tt // L
    hd = M_HEADS * LANES

    @pl.when(j == 0)
    def _():
        lane = lax.broadcasted_iota(jnp.int32, (1, 1, 1, LANES), 3)
        c_s[:, :, :, 0:LANES] = c0_ref[...]
        c_s[:, :, :, LANES:2 * LANES] = jnp.where(lane == 0, n0_ref[...], 0.0)
        m_s[...] = m0_ref[...]

    x = x_ref[...]
    xb = x.astype(BF16)
    for cb in range(0, hd, A_COL_BLOCK):
        q_s[:, cb:cb + A_COL_BLOCK] = _dot(xb, w_ref[:, A_QOFF + cb:A_QOFF + cb + A_COL_BLOCK]).astype(BF16)
        k_s[:, cb:cb + A_COL_BLOCK] = _dot(xb, w_ref[:, A_KOFF + cb:A_KOFF + cb + A_COL_BLOCK])
        v_s[:, cb:cb + A_COL_BLOCK] = _dot(xb, w_ref[:, A_VOFF + cb:A_VOFF + cb + A_COL_BLOCK]).astype(BF16)
        o_s[:, cb:cb + A_COL_BLOCK] = _dot(xb, w_ref[:, A_OOFF + cb:A_OOFF + cb + A_COL_BLOCK])
    g_s[...] = _dot(xb, w_ref[:, A_GOFF:A_GOFF + 2 * LANES]) + bg_ref[...]

    row_i = lax.broadcasted_iota(jnp.int32, (L, L), 0)
    col_i = lax.broadcasted_iota(jnp.int32, (L, L), 1)
    causal = row_i >= col_i
    tri = causal.astype(F32)
    ones_col = jnp.where(lax.broadcasted_iota(jnp.int32, (L, LANES), 1) == 0, 1.0, 0.0).astype(BF16)

    def unit(u, carry):
        r0 = pl.multiple_of(u * L, L)
        bb = u // units_per_b if bb_n > 1 else 0
        gi = g_s[pl.ds(r0, L), 0:LANES]
        gf = g_s[pl.ds(r0, L), LANES:2 * LANES]
        lf = jnp.minimum(gf, 0.0) - jnp.log1p(jnp.exp(-jnp.abs(gf)))
        bc = lax.dot_general(tri, lf, (((1,), (0,)), ((), ())), precision=HIGHEST,
                             preferred_element_type=F32)
        rs = gi - bc
        rs_t = rs.T
        b_last = bc[L - 1:L, :]
        for h in range(M_HEADS):
            hs = slice(h * LANES, (h + 1) * LANES)
            qh = q_s[pl.ds(r0, L), hs]
            kh = k_s[pl.ds(r0, L), hs]
            vh = v_s[pl.ds(r0, L), hs]
            m0 = m_s[bb, h, 0:1, 0:1]
            c_aug = c_s[bb, h]
            b_col = bc[:, h:h + 1]
            logd = jnp.where(causal, b_col + rs_t[h:h + 1, :], NEG_INF)
            inter = b_col + m0
            mx = jnp.maximum(inter, jnp.max(logd, axis=1, keepdims=True))
            d = jnp.exp(logd - mx)
            w_inter = jnp.exp(inter - mx)
            s = _dot_nt(qh, kh.astype(BF16))
            p = (s * d).astype(BF16)
            v_aug = jnp.concatenate([vh, ones_col], axis=1)
            numden = w_inter * _dot(qh[:, 0:M_DK], c_aug.astype(BF16)) + _dot(p, v_aug)
            num = numden[:, 0:M_DV]
            den = numden[:, M_DV:M_DV + 1]
            h_s[pl.ds(r0, L), hs] = num / jnp.maximum(jnp.abs(den), jnp.exp(-mx))
            m_new = mx[L - 1:L, :]
            bl = b_last[:, h:h + 1]
            w_end = jnp.exp(rs[:, h:h + 1] + (bl - m_new))
            decay = jnp.exp(bl + m0 - m_new)
            wk = (kh[:, 0:M_DK] * w_end).astype(BF16)
            c_s[bb, h] = decay * c_aug + _dot_tn(wk, v_aug)
            m_s[bb, h] = jnp.broadcast_to(m_new, (8, LANES))
        return carry

    lax.fori_loop(0, rows // L, unit, 0)

    for h in range(M_HEADS):
        hs = slice(h * LANES, (h + 1) * LANES)
        hh = h_s[:, hs]
        hn = hh * lax.rsqrt(jnp.mean(hh * hh, axis=-1, keepdims=True) + HEAD_NORM_EPS)
        hb_s[:, hs] = (hn * gn_ref[:, hs] * jax.nn.sigmoid(o_s[:, hs])).astype(BF16)
    mix = _dot(hb_s[...], wo_ref[...])
    y_ref[...] = _layer_norm(alpha * x + mix, lg_ref[...], lb_ref[...])

    @pl.when(j == n_j - 1)
    def _():
        cout_ref[...] = c_s[...]
        mout_ref[...] = m_s[...]


def _mlstm_layer(x2d, c0, n0, m0, w_all, b_gate, g_norm, w_out, ln_g, ln_b, *, alpha, batch, seq, L, bb_n, tt):
    n, d = x2d.shape
    assert n == batch * seq and seq % tt == 0 and tt % L == 0 and batch % bb_n == 0
    assert bb_n == 1 or tt == seq
    n_j = seq // tt
    rows = bb_n * tt
    hd = M_HEADS * LANES
    kern = functools.partial(_mlstm_kernel, alpha=alpha, L=L, bb_n=bb_n, tt=tt, n_j=n_j)
    st4 = lambda bi, j: (bi, 0, 0, 0)
    y, c_out, m_out = pl.pallas_call(
        kern,
        grid=(batch // bb_n, n_j),
        in_specs=[
            pl.BlockSpec((rows, d), lambda bi, j: (bi * n_j + j, 0)),
            pl.BlockSpec((bb_n, M_HEADS, M_DK, M_DV), st4),
            pl.BlockSpec((bb_n, M_HEADS, M_DK, 1), st4),
            pl.BlockSpec((bb_n, M_HEADS, 8, LANES), st4),
            _const_spec(w_all.shape),
            _const_spec(b_gate.shape),
            _const_spec(g_norm.shape),
            _const_spec(w_out.shape),
            _const_spec(ln_g.shape),
            _const_spec(ln_b.shape),
        ],
        out_specs=[
            pl.BlockSpec((rows, d), lambda bi, j: (bi * n_j + j, 0)),
            pl.BlockSpec((bb_n, M_HEADS, M_DK, 2 * LANES), st4),
            pl.BlockSpec((bb_n, M_HEADS, 8, LANES), st4),
        ],
        out_shape=[
            jax.ShapeDtypeStruct((n, d), F32),
            jax.ShapeDtypeStruct((batch, M_HEADS, M_DK, 2 * LANES), F32),
            jax.ShapeDtypeStruct((batch, M_HEADS, 8, LANES), F32),
        ],
        scratch_shapes=[
            pltpu.VMEM((rows, hd), BF16),
            pltpu.VMEM((rows, hd), F32),
            pltpu.VMEM((rows, hd), BF16),
            pltpu.VMEM((rows, hd), F32),
            pltpu.VMEM((rows, 2 * LANES), F32),
            pltpu.VMEM((rows, hd), F32),
            pltpu.VMEM((rows, hd), BF16),
            pltpu.VMEM((bb_n, M_HEADS, M_DK, 2 * LANES), F32),
            pltpu.VMEM((bb_n, M_HEADS, 8, LANES), F32),
        ],
        compiler_params=pltpu.CompilerParams(
            dimension_semantics=("arbitrary", "arbitrary"), vmem_limit_bytes=VMEM_LIMIT_BYTES),
        name="mlstm_layer",
    )(x2d, c0, n0, m0, w_all, b_gate, g_norm, w_out, ln_g, ln_b)
    c_new = c_out[..., 0:M_DV]
    n_new = c_out[..., M_DV]
    m_new = m_out[:, :, 0, 0]
    return y, c_new, n_new, m_new


def _prep_mlstm_weights(w_in, b_gate, g_norm, w_out):
    d = w_in.shape[0]
    hk = M_HEADS * M_DK
    hv = M_HEADS * M_DV
    wq = w_in[:, 0:hk].reshape(d, M_HEADS, M_DK)
    wk = w_in[:, hk:2 * hk].reshape(d, M_HEADS, M_DK) * (M_DK ** -0.5)
    pad = ((0, 0), (0, 0), (0, LANES - M_DK))
    wq = jnp.pad(wq, pad).reshape(d, M_HEADS * LANES)
    wk = jnp.pad(wk, pad).reshape(d, M_HEADS * LANES)
    wv = w_in[:, 2 * hk:2 * hk + hv]
    wo = w_in[:, 2 * hk + hv:2 * hk + 2 * hv]
    wi = jnp.pad(w_in[:, 2 * hk + 2 * hv:2 * hk + 2 * hv + M_HEADS], ((0, 0), (0, LANES - M_HEADS)))
    wf = jnp.pad(w_in[:, 2 * hk + 2 * hv + M_HEADS:], ((0, 0), (0, LANES - M_HEADS)))
    w_all = jnp.concatenate([wq, wk, wv, wo, wi, wf], axis=1).astype(BF16)
    bg = jnp.concatenate([jnp.pad(b_gate[0:M_HEADS], (0, LANES - M_HEADS)),
                          jnp.pad(b_gate[M_HEADS:], (0, LANES - M_HEADS))]).astype(F32)[None, :]
    return w_all, bg, g_norm.astype(F32)[None, :], w_out.astype(BF16)


PAIRS = GROUP // 2


def _attn_unit(q_rows, k_lo, k_hi, v_lo, v_hi, bias_lo, bias_hi, sink_lo, sink_hi, key_ok):
    outs = []
    for k_op, v_op, bias, sink in ((k_lo, v_lo, bias_lo, sink_lo), (k_hi, v_hi, bias_hi, sink_hi)):
        s = _dot_nt(q_rows, k_op) + bias
        if key_ok is not None:
            s = jnp.where(key_ok, s, NEG_INF)
        mx = jnp.maximum(jnp.max(s, axis=-1, keepdims=True), sink)
        p = jnp.exp(s - mx)
        den = jnp.sum(p, axis=-1, keepdims=True) + jnp.exp(sink - mx)
        outs.append(_dot(p.astype(BF16), v_op) / den)
    return outs[0] + outs[1]


def _split_kv(kv):
    lane = lax.broadcasted_iota(jnp.int32, (1, LANES), 1)
    low = lane < HEAD_DIM
    res = []
    kk = kv[:, 0:LANES]
    vv = kv[:, LANES:2 * LANES]
    kk_r = pltpu.roll(kk, HEAD_DIM, 1)
    vv_r = pltpu.roll(vv, HEAD_DIM, 1)
    z = jnp.zeros_like(kk)
    res.append((jnp.where(low, kk, z), jnp.where(low, z, kk_r), jnp.where(low, vv, z), jnp.where(low, z, vv_r)))
    res.append((jnp.where(low, kk_r, z), jnp.where(low, z, kk), jnp.where(low, vv_r, z), jnp.where(low, z, vv)))
    return [tuple(a.astype(BF16) for a in grp) for grp in res]


SWA_UNIT = 2 * CHUNK
SWA_KEYS = WINDOW + SWA_UNIT


def _swa_prompt_kernel(x_ref, kc_ref, kp_ref, vtc_ref, vtp_ref, wqt_ref, wo_ref, bias_ref, sink_ref, lg_ref, lb_ref,
                       y_ref, qt_s, ot_s, k_s, vt_s, *, alpha, tq):
    j = pl.program_id(1)
    x = x_ref[...]
    xb = x.astype(BF16)
    qt_s[...] = _dot_nt(wqt_ref[...], xb).astype(BF16)
    k_s[0:WINDOW, :] = kp_ref[...].astype(BF16)
    k_s[WINDOW:WINDOW + tq, :] = kc_ref[...].astype(BF16)
    vt_s[:, 0:WINDOW] = vtp_ref[...]
    vt_s[:, WINDOW:WINDOW + tq] = vtc_ref[...]
    zeros = jnp.zeros((HEAD_DIM, GROUP * SWA_UNIT), BF16)
    key_i = lax.broadcasted_iota(jnp.int32, (SWA_KEYS, GROUP * SWA_UNIT), 0)
    for u in range(tq // SWA_UNIT):
        r0 = u * SWA_UNIT
        k_win = k_s[r0:r0 + SWA_KEYS, :]
        vt_win = vt_s[:, r0:r0 + SWA_KEYS]
        for g in range(N_KV_HEADS):
            heads = range(g * GROUP, (g + 1) * GROUP)
            qt_g = jnp.concatenate(
                [qt_s[h * HEAD_DIM:(h + 1) * HEAD_DIM, r0:r0 + SWA_UNIT] for h in heads], axis=1)
            qz = jnp.concatenate([qt_g, zeros] if g == 0 else [zeros, qt_g], axis=0)
            s_t = _dot(k_win, qz) + bias_ref[g]
            if r0 < WINDOW:
                s_t = jnp.where(j * tq + r0 - WINDOW + key_i >= 0, s_t, NEG_INF)
            sink = sink_ref[g]
            mx = jnp.maximum(jnp.max(s_t, axis=0, keepdims=True), sink)
            p = jnp.exp(s_t - mx)
            den = jnp.sum(p, axis=0, keepdims=True) + jnp.exp(sink - mx)
            o_t = (_dot(vt_win[g * HEAD_DIM:(g + 1) * HEAD_DIM, :], p.astype(BF16)) / den).astype(BF16)
            for i, h in enumerate(heads):
                ot_s[h * HEAD_DIM:(h + 1) * HEAD_DIM, r0:r0 + SWA_UNIT] = o_t[:, i * SWA_UNIT:(i + 1) * SWA_UNIT]
    mix = _dot_tn(ot_s[...], wo_ref[...])
    y_ref[...] = _layer_norm(alpha * x + mix, lg_ref[...], lb_ref[...])


def _swa_prompt_layer(x2d, kv2d, vt, wqt, wo, bias, sinks, ln_g, ln_b, *, alpha, batch, seq, tq):
    n, d = x2d.shape
    assert n == batch * seq and seq % tq == 0 and tq % SWA_UNIT == 0 and WINDOW == LANES
    n_j = seq // tq
    per = tq // WINDOW
    prev = lambda b, j: jnp.maximum((b * n_j + j) * per - 1, 0)
    return pl.pallas_call(
        functools.partial(_swa_prompt_kernel, alpha=alpha, tq=tq),
        grid=(batch, n_j),
        in_specs=[
            pl.BlockSpec((tq, d), lambda b, j: (b * n_j + j, 0)),
            pl.BlockSpec((tq, LANES), lambda b, j: (b * n_j + j, 0)),
            pl.BlockSpec((WINDOW, LANES), lambda b, j: (prev(b, j), 0)),
            pl.BlockSpec((LANES, tq), lambda b, j: (0, b * n_j + j)),
            pl.BlockSpec((LANES, WINDOW), lambda b, j: (0, prev(b, j))),
            _const_spec(wqt.shape),
            _const_spec(wo.shape),
            _const_spec(bias.shape),
            _const_spec(sinks.shape),
            _const_spec(ln_g.shape),
            _const_spec(ln_b.shape),
        ],
        out_specs=pl.BlockSpec((tq, d), lambda b, j: (b * n_j + j, 0)),
        out_shape=jax.ShapeDtypeStruct((n, d), F32),
        scratch_shapes=[
            pltpu.VMEM((d, tq), BF16),
            pltpu.VMEM((d, tq), BF16),
            pltpu.VMEM((WINDOW + tq, LANES), BF16),
            pltpu.VMEM((LANES, WINDOW + tq), BF16),
        ],
        compiler_params=pltpu.CompilerParams(
            dimension_semantics=("arbitrary", "arbitrary"), vmem_limit_bytes=VMEM_LIMIT_BYTES),
        name="swa_prompt",
    )(x2d, kv2d, kv2d, vt, vt, wqt, wo, bias, sinks, ln_g, ln_b)


def _swa_prompt_tables(sinks):
    slopes = jnp.exp2(-8.0 * jnp.arange(1, N_Q_HEADS + 1, dtype=F32) / N_Q_HEADS)
    q = jnp.arange(SWA_UNIT)[None, :]
    kx = jnp.arange(SWA_KEYS)[:, None]
    dist = jnp.abs(q + WINDOW - kx).astype(F32)
    first = q < CHUNK
    visible = (first & (kx < WINDOW + CHUNK)) | (~first & (kx >= CHUNK))
    bias = jnp.where(visible[None], -slopes[:, None, None] * dist[None], NEG_INF)
    bias = bias.reshape(N_KV_HEADS, GROUP, SWA_KEYS, SWA_UNIT).transpose(0, 2, 1, 3)
    sink = jnp.broadcast_to(sinks.astype(F32).reshape(N_KV_HEADS, 1, GROUP, 1), (N_KV_HEADS, 1, GROUP, SWA_UNIT))
    return (bias.reshape(N_KV_HEADS, SWA_KEYS, GROUP * SWA_UNIT), sink.reshape(N_KV_HEADS, 1, GROUP * SWA_UNIT))


def _swa_sample_kernel(x_ref, kvn_ref, kc_ref, vc_ref, wq_ref, wo_ref, bias_ref, sink_ref, lg_ref, lb_ref,
                       y_ref, q_s, o_s, *, alpha, batch, seq):
    x = x_ref[...]
    xb = x.astype(BF16)
    q_s[...] = _dot(xb, wq_ref[...]).astype(BF16)
    for b in range(batch):
        r0 = b * seq
        kv_new = kvn_ref[r0:r0 + seq, :]
        kv_old = jnp.concatenate([kc_ref[b], vc_ref[b]], axis=1)
        groups = _split_kv(jnp.concatenate([kv_old, kv_new], axis=0))
        for g in range(N_KV_HEADS):
            q_rows = jnp.concatenate(
                [q_s[r0:r0 + seq, (g * PAIRS + pp) * LANES:(g * PAIRS + pp + 1) * LANES] for pp in range(PAIRS)],
                axis=0)
            ks = groups[g]
            o = _attn_unit(q_rows, ks[0], ks[1], ks[2], ks[3], bias_ref[g, 0], bias_ref[g, 1],
                           sink_ref[g, 0], sink_ref[g, 1], None)
            for pp in range(PAIRS):
                o_s[r0:r0 + seq, (g * PAIRS + pp) * LANES:(g * PAIRS + pp + 1) * LANES] = (
                    o[pp * seq:(pp + 1) * seq, :].astype(BF16))
    mix = _dot(o_s[...], wo_ref[...])
    y_ref[...] = _layer_norm(alpha * x + mix, lg_ref[...], lb_ref[...])


def _swa_sample_layer(x2d, kv_new, k_cache, v_cache, wq, wo, bias, sink, ln_g, ln_b, *, alpha, batch, seq):
    n, d = x2d.shape
    assert n == batch * seq
    args = (x2d, kv_new, k_cache, v_cache, wq, wo, bias, sink, ln_g, ln_b)
    return pl.pallas_call(
        functools.partial(_swa_sample_kernel, alpha=alpha, batch=batch, seq=seq),
        grid=(1,),
        in_specs=[_const_spec(a.shape) for a in args],
        out_specs=pl.BlockSpec((n, d), lambda i: (0, 0)),
        out_shape=jax.ShapeDtypeStruct((n, d), F32),
        scratch_shapes=[pltpu.VMEM((n, d), BF16), pltpu.VMEM((n, d), BF16)],
        compiler_params=pltpu.CompilerParams(
            dimension_semantics=("arbitrary",), vmem_limit_bytes=VMEM_LIMIT_BYTES),
        name="swa_sample",
    )(*args)


def _attn_tables(sinks, ql, nk):
    slopes = jnp.exp2(-8.0 * jnp.arange(1, N_Q_HEADS + 1, dtype=F32) / N_Q_HEADS)
    dist = jnp.abs(jnp.arange(ql)[:, None] - jnp.arange(nk)[None, :] + (nk - ql)).astype(F32)
    head = (jnp.arange(N_KV_HEADS)[:, None, None] * GROUP + 2 * jnp.arange(PAIRS)[None, None, :]
            + jnp.arange(2)[None, :, None])
    bias = -slopes[head][..., None, None] * dist
    sink = jnp.broadcast_to(sinks.astype(F32)[head][..., None, None], head.shape + (ql, 1))
    return (bias.reshape(N_KV_HEADS, 2, PAIRS * ql, nk), sink.reshape(N_KV_HEADS, 2, PAIRS * ql, 1))


def _trunk(x, c0s, n0s, m0s, k_cache, v_cache, params, *, is_prompt):
    (w_in_a, b_gate_a, g_norm_a, w_out_a, w_kv, w_q_b, sinks_b, w_out_b, w_gu, w_down, ln_g, ln_b) = params
    batch, seq, d = x.shape
    depth = w_gu.shape[0]
    n_a = w_in_a.shape[0]
    alpha = (2 * depth) ** 0.25
    d_ff = w_down.shape[1]
    n = batch * seq
    x2d = x.reshape(n, d)
    tm = min(512, n)
    if is_prompt:
        L, bb_n, tt = 128, 1, 512
    else:
        L, bb_n, tt = seq, 8, seq
    cs, ns, ms = [], [], []
    kv2d = None
    for layer in range(depth):
        row = lambda a: a.astype(F32)[None, :]
        if layer < n_a:
            w_all, bg, gn, wo = _prep_mlstm_weights(w_in_a[layer], b_gate_a[layer], g_norm_a[layer], w_out_a[layer])
            m0 = jnp.broadcast_to(m0s[layer].astype(F32)[:, :, None, None], (batch, M_HEADS, 8, LANES))
            x2d, c, nn, m = _mlstm_layer(
                x2d, c0s[layer].astype(F32), n0s[layer].astype(F32)[..., None], m0, w_all, bg, gn, wo,
                row(ln_g[layer, 0]), row(ln_b[layer, 0]), alpha=alpha, batch=batch, seq=seq, L=L, bb_n=bb_n, tt=tt)
            cs.append(c)
            ns.append(nn)
            ms.append(m)
        else:
            jb = layer - n_a
            if kv2d is None:
                n_v = N_KV_HEADS * HEAD_DIM
                kv2d, vt = _kv_proj(x2d, w_kv.astype(BF16), w_kv[:, n_v:].T.astype(BF16), tm=tm)
            wq = (w_q_b[jb] * (HEAD_DIM ** -0.5)).astype(BF16)
            wo = w_out_b[jb].astype(BF16)
            if is_prompt:
                bias, sink = _swa_prompt_tables(sinks_b[jb])
                x2d = _swa_prompt_layer(x2d, kv2d, vt, wq.T, wo, bias, sink,
                                        row(ln_g[layer, 0]), row(ln_b[layer, 0]),
                                        alpha=alpha, batch=batch, seq=seq, tq=512)
            else:
                w_rows = k_cache.shape[1]
                bias, sink = _attn_tables(sinks_b[jb], seq, w_rows + seq)
                x2d = _swa_sample_layer(
                    x2d, kv2d, k_cache.astype(F32).reshape(batch, w_rows, N_KV_HEADS * HEAD_DIM),
                    v_cache.astype(F32).reshape(batch, w_rows, N_KV_HEADS * HEAD_DIM), wq, wo, bias, sink,
                    row(ln_g[layer, 0]), row(ln_b[layer, 0]), alpha=alpha, batch=batch, seq=seq)
        x2d = _ffn(x2d, w_gu[layer][:, :d_ff].astype(BF16), w_gu[layer][:, d_ff:].astype(BF16),
                   w_down[layer].astype(BF16), row(ln_g[layer, 1]), row(ln_b[layer, 1]), alpha=alpha, tm=tm)
    kv = kv2d.reshape(batch, seq, 2, N_KV_HEADS, HEAD_DIM)
    return (x2d.reshape(batch, seq, d), jnp.stack(cs), jnp.stack(ns), jnp.stack(ms), kv[:, :, 0], kv[:, :, 1])


def kernel(x_prompt, x_sample, state_C, state_n, state_m, cache_k, cache_v, w_in_a, b_gate_a, g_norm_a,
           w_out_a, w_kv, w_q_b, sinks_b, w_out_b, w_gu, w_down, ln_g, ln_b):
    params = (w_in_a, b_gate_a, g_norm_a, w_out_a, w_kv, w_q_b, sinks_b, w_out_b, w_gu, w_down, ln_g, ln_b)
    bp = x_prompt.shape[0]
    n_a = w_in_a.shape[0]
    c0 = jnp.zeros((n_a, bp, M_HEADS, M_DK, M_DV), F32)
    n0 = jnp.zeros((n_a, bp, M_HEADS, M_DK), F32)
    m0 = jnp.zeros((n_a, bp, M_HEADS), F32)
    y_p, p_c, p_n, p_m, p_k, p_v = _trunk(x_prompt, c0, n0, m0, None, None, params, is_prompt=True)
    y_s, s_c, s_n, s_m, s_k, s_v = _trunk(x_sample, state_C, state_n, state_m, cache_k, cache_v, params,
                                          is_prompt=False)
    rows = min(WINDOW, x_prompt.shape[1])
    return (y_p, y_s, p_c, p_n, p_m, p_k[:, -rows:], p_v[:, -rows:], s_c, s_n, s_m, s_k, s_v)
```

```python
import functools

import jax
import jax.numpy as jnp
from jax import lax
from jax.experimental import pallas as pl
from jax.experimental.pallas import tpu as pltpu

F32 = jnp.float32
BF16 = jnp.bfloat16

CHUNK = 64
WINDOW = 128
M_HEADS = 8
M_DK = 64
M_DV = 128
N_Q_HEADS = 16
N_KV_HEADS = 2
HEAD_DIM = 64
GROUP = N_Q_HEADS // N_KV_HEADS
LN_EPS = 1e-5
HEAD_NORM_EPS = 1e-6

LANES = 128
VMEM_LIMIT_BYTES = 56 * 1024 * 1024

NEG_INF = float("-inf")
HIGHEST = lax.Precision.HIGHEST


def _const_spec(shape):
    nd = len(shape)
    return pl.BlockSpec(shape, lambda *_: (0,) * nd, pipeline_mode=pl.Buffered(1))


def _layer_norm(y, g, b):
    mu = jnp.mean(y, axis=-1, keepdims=True)
    yc = y - mu
    var = jnp.mean(yc * yc, axis=-1, keepdims=True)
    return yc * lax.rsqrt(var + LN_EPS) * g + b


def _dot(a, b):
    return jnp.dot(a, b, preferred_element_type=F32)


def _dot_nt(a, b):
    return lax.dot_general(a, b, (((1,), (1,)), ((), ())), preferred_element_type=F32)


def _dot_tn(a, b):
    return lax.dot_general(a, b, (((0,), (0,)), ((), ())), preferred_element_type=F32)


FF_BLOCK = 256


def _ffn_kernel(x_ref, wg_ref, wu_ref, wd_ref, g_ref, b_ref, o_ref, acc_ref, *, alpha, d_ff):
    x = x_ref[...]
    xb = x.astype(BF16)
    for j in range(d_ff // FF_BLOCK):
        sl = slice(j * FF_BLOCK, (j + 1) * FF_BLOCK)
        g = _dot(xb, wg_ref[:, sl])
        u = _dot(xb, wu_ref[:, sl])
        h = (g * jax.nn.sigmoid(g) * u).astype(BF16)
        part = _dot(h, wd_ref[sl, :])
        if j == 0:
            acc_ref[...] = part
        else:
            acc_ref[...] += part
    o_ref[...] = _layer_norm(alpha * x + acc_ref[...], g_ref[...], b_ref[...])


def _ffn(x2d, wg, wu, wd, ln_g, ln_b, *, alpha, tm):
    n, d = x2d.shape
    d_ff = wg.shape[1]
    assert n % tm == 0 and d_ff % FF_BLOCK == 0
    return pl.pallas_call(
        functools.partial(_ffn_kernel, alpha=alpha, d_ff=d_ff),
        grid=(n // tm,),
        in_specs=[
            pl.BlockSpec((tm, d), lambda i: (i, 0)),
            _const_spec(wg.shape),
            _const_spec(wu.shape),
            _const_spec(wd.shape),
            _const_spec(ln_g.shape),
            _const_spec(ln_b.shape),
        ],
        out_specs=pl.BlockSpec((tm, d), lambda i: (i, 0)),
        out_shape=jax.ShapeDtypeStruct((n, d), F32),
        scratch_shapes=[pltpu.VMEM((tm, d), F32)],
        compiler_params=pltpu.CompilerParams(
            dimension_semantics=("arbitrary",), vmem_limit_bytes=VMEM_LIMIT_BYTES),
        name="ffn",
    )(x2d, wg, wu, wd, ln_g, ln_b)


def _kv_kernel(x_ref, w_ref, wvt_ref, o_ref, vt_ref):
    xb = x_ref[...].astype(BF16)
    o_ref[...] = _dot(xb, w_ref[...])
    vt_ref[...] = _dot_nt(wvt_ref[...], xb).astype(BF16)


def _kv_proj(x2d, w_kv, w_vt, *, tm):
    n, d = x2d.shape
    nk = w_kv.shape[1]
    nv = w_vt.shape[0]
    assert n % tm == 0
    return pl.pallas_call(
        _kv_kernel,
        grid=(n // tm,),
        in_specs=[pl.BlockSpec((tm, d), lambda i: (i, 0)), _const_spec(w_kv.shape), _const_spec(w_vt.shape)],
        out_specs=[pl.BlockSpec((tm, nk), lambda i: (i, 0)), pl.BlockSpec((nv, tm), lambda i: (0, i))],
        out_shape=[jax.ShapeDtypeStruct((n, nk), F32), jax.ShapeDtypeStruct((nv, n), BF16)],
        compiler_params=pltpu.CompilerParams(
            dimension_semantics=("arbitrary",), vmem_limit_bytes=VMEM_LIMIT_BYTES),
        name="kv_proj",
    )(x2d, w_kv, w_vt)


A_QOFF = 0
A_KOFF = M_HEADS * LANES
A_VOFF = 2 * M_HEADS * LANES
A_OOFF = 3 * M_HEADS * LANES
A_GOFF = 4 * M_HEADS * LANES
A_COLS = A_GOFF + 2 * LANES
A_COL_BLOCK = 512


def _mlstm_kernel(x_ref, c0_ref, n0_ref, m0_ref, w_ref, bg_ref, gn_ref, wo_ref, lg_ref, lb_ref,
                  y_ref, cout_ref, mout_ref,
                  q_s, k_s, v_s, o_s, g_s, h_s, hb_s, c_s, m_s,
                  *, alpha, L, bb_n, tt, n_j):
    j = pl.program_id(1)
    rows = bb_n * tt
    units_per_b = tt // L
    hd = M_HEADS * LANES

    @pl.when(j == 0)
    def _():
        lane = lax.broadcasted_iota(jnp.int32, (1, 1, 1, LANES), 3)
        c_s[:, :, :, 0:LANES] = c0_ref[...]
        c_s[:, :, :, LANES:2 * LANES] = jnp.where(lane == 0, n0_ref[...], 0.0)
        m_s[...] = m0_ref[...]

    x = x_ref[...]
    xb = x.astype(BF16)
    for cb in range(0, hd, A_COL_BLOCK):
        q_s[:, cb:cb + A_COL_BLOCK] = _dot(xb, w_ref[:, A_QOFF + cb:A_QOFF + cb + A_COL_BLOCK]).astype(BF16)
        k_s[:, cb:cb + A_COL_BLOCK] = _dot(xb, w_ref[:, A_KOFF + cb:A_KOFF + cb + A_COL_BLOCK])
        v_s[:, cb:cb + A_COL_BLOCK] = _dot(xb, w_ref[:, A_VOFF + cb:A_VOFF + cb + A_COL_BLOCK]).astype(BF16)
        o_s[:, cb:cb + A_COL_BLOCK] = _dot(xb, w_ref[:, A_OOFF + cb:A_OOFF + cb + A_COL_BLOCK])
    g_s[...] = _dot(xb, w_ref[:, A_GOFF:A_GOFF + 2 * LANES]) + bg_ref[...]

    row_i = lax.broadcasted_iota(jnp.int32, (L, L), 0)
    col_i = lax.broadcasted_iota(jnp.int32, (L, L), 1)
    causal = row_i >= col_i
    tri = causal.astype(F32)
    ones_col = jnp.where(lax.broadcasted_iota(jnp.int32, (L, LANES), 1) == 0, 1.0, 0.0).astype(BF16)

    def unit(u, carry):
        r0 = pl.multiple_of(u * L, L)
        bb = u // units_per_b if bb_n > 1 else 0
        gi = g_s[pl.ds(r0, L), 0:LANES]
        gf = g_s[pl.ds(r0, L), LANES:2 * LANES]
        lf = jnp.minimum(gf, 0.0) - jnp.log1p(jnp.exp(-jnp.abs(gf)))
        bc = lax.dot_general(tri, lf, (((1,), (0,)), ((), ())), precision=HIGHEST,
                             preferred_element_type=F32)
        rs = gi - bc
        rs_t = rs.T
        b_last = bc[L - 1:L, :]
        for h in range(M_HEADS):
            hs = slice(h * LANES, (h + 1) * LANES)
            qh = q_s[pl.ds(r0, L), hs]
            kh = k_s[pl.ds(r0, L), hs]
            vh = v_s[pl.ds(r0, L), hs]
            m0 = m_s[bb, h, 0:1, 0:1]
            c_aug = c_s[bb, h]
            b_col = bc[:, h:h + 1]
            logd = jnp.where(causal, b_col + rs_t[h:h + 1, :], NEG_INF)
            inter = b_col + m0
            mx = jnp.maximum(inter, jnp.max(logd, axis=1, keepdims=True))
            d = jnp.exp(logd - mx)
            w_inter = jnp.exp(inter - mx)
            s = _dot_nt(qh, kh.astype(BF16))
            p = (s * d).astype(BF16)
            v_aug = jnp.concatenate([vh, ones_col], axis=1)
            numden = w_inter * _dot(qh[:, 0:M_DK], c_aug.astype(BF16)) + _dot(p, v_aug)
            num = numden[:, 0:M_DV]
            den = numden[:, M_DV:M_DV + 1]
            h_s[pl.ds(r0, L), hs] = num / jnp.maximum(jnp.abs(den), jnp.exp(-mx))
            m_new = mx[L - 1:L, :]
            bl = b_last[:, h:h + 1]
            w_end = jnp.exp(rs[:, h:h + 1] + (bl - m_new))
            decay = jnp.exp(bl + m0 - m_new)
            wk = (kh[:, 0:M_DK] * w_end).astype(BF16)
            c_s[bb, h] = decay * c_aug + _dot_tn(wk, v_aug)
            m_s[bb, h] = jnp.broadcast_to(m_new, (8, LANES))
        return carry

    lax.fori_loop(0, rows // L, unit, 0)

    for h in range(M_HEADS):
        hs = slice(h * LANES, (h + 1) * LANES)
        hh = h_s[:, hs]
        hn = hh * lax.rsqrt(jnp.mean(hh * hh, axis=-1, keepdims=True) + HEAD_NORM_EPS)
        hb_s[:, hs] = (hn * gn_ref[:, hs] * jax.nn.sigmoid(o_s[:, hs])).astype(BF16)
    mix = _dot(hb_s[...], wo_ref[...])
    y_ref[...] = _layer_norm(alpha * x + mix, lg_ref[...], lb_ref[...])

    @pl.when(j == n_j - 1)
    def _():
        cout_ref[...] = c_s[...]
        mout_ref[...] = m_s[...]


def _mlstm_layer(x2d, c0, n0, m0, w_all, b_gate, g_norm, w_out, ln_g, ln_b, *, alpha, batch, seq, L, bb_n, tt):
    n, d = x2d.shape
    assert n == batch * seq and seq % tt == 0 and tt % L == 0 and batch % bb_n == 0
    assert bb_n == 1 or tt == seq
    n_j = seq // tt
    rows = bb_n * tt
    hd = M_HEADS * LANES
    kern = functools.partial(_mlstm_kernel, alpha=alpha, L=L, bb_n=bb_n, tt=tt, n_j=n_j)
    st4 = lambda bi, j: (bi, 0, 0, 0)
    y, c_out, m_out = pl.pallas_call(
        kern,
        grid=(batch // bb_n, n_j),
        in_specs=[
            pl.BlockSpec((rows, d), lambda bi, j: (bi * n_j + j, 0)),
            pl.BlockSpec((bb_n, M_HEADS, M_DK, M_DV), st4),
            pl.BlockSpec((bb_n, M_HEADS, M_DK, 1), st4),
            pl.BlockSpec((bb_n, M_HEADS, 8, LANES), st4),
            _const_spec(w_all.shape),
            _const_spec(b_gate.shape),
            _const_spec(g_norm.shape),
            _const_spec(w_out.shape),
            _const_spec(ln_g.shape),
            _const_spec(ln_b.shape),
        ],
        out_specs=[
            pl.BlockSpec((rows, d), lambda bi, j: (bi * n_j + j, 0)),
            pl.BlockSpec((bb_n, M_HEADS, M_DK, 2 * LANES), st4),
            pl.BlockSpec((bb_n, M_HEADS, 8, LANES), st4),
        ],
        out_shape=[
            jax.ShapeDtypeStruct((n, d), F32),
            jax.ShapeDtypeStruct((batch, M_HEADS, M_DK, 2 * LANES), F32),
            jax.ShapeDtypeStruct((batch, M_HEADS, 8, LANES), F32),
        ],
        scratch_shapes=[
            pltpu.VMEM((rows, hd), BF16),
            pltpu.VMEM((rows, hd), F32),
            pltpu.VMEM((rows, hd), BF16),
            pltpu.VMEM((rows, hd), F32),
            pltpu.VMEM((rows, 2 * LANES), F32),
            pltpu.VMEM((rows, hd), F32),
            pltpu.VMEM((rows, hd), BF16),
            pltpu.VMEM((bb_n, M_HEADS, M_DK, 2 * LANES), F32),
            pltpu.VMEM((bb_n, M_HEADS, 8, LANES), F32),
        ],
        compiler_params=pltpu.CompilerParams(
            dimension_semantics=("arbitrary", "arbitrary"), vmem_limit_bytes=VMEM_LIMIT_BYTES),
        name="mlstm_layer",
    )(x2d, c0, n0, m0, w_all, b_gate, g_norm, w_out, ln_g, ln_b)
    c_new = c_out[..., 0:M_DV]
    n_new = c_out[..., M_DV]
    m_new = m_out[:, :, 0, 0]
    return y, c_new, n_new, m_new


def _prep_mlstm_weights(w_in, b_gate, g_norm, w_out):
    d = w_in.shape[0]
    hk = M_HEADS * M_DK
    hv = M_HEADS * M_DV
    wq = w_in[:, 0:hk].reshape(d, M_HEADS, M_DK)
    wk = w_in[:, hk:2 * hk].reshape(d, M_HEADS, M_DK) * (M_DK ** -0.5)
    pad = ((0, 0), (0, 0), (0, LANES - M_DK))
    wq = jnp.pad(wq, pad).reshape(d, M_HEADS * LANES)
    wk = jnp.pad(wk, pad).reshape(d, M_HEADS * LANES)
    wv = w_in[:, 2 * hk:2 * hk + hv]
    wo = w_in[:, 2 * hk + hv:2 * hk + 2 * hv]
    wi = jnp.pad(w_in[:, 2 * hk + 2 * hv:2 * hk + 2 * hv + M_HEADS], ((0, 0), (0, LANES - M_HEADS)))
    wf = jnp.pad(w_in[:, 2 * hk + 2 * hv + M_HEADS:], ((0, 0), (0, LANES - M_HEADS)))
    w_all = jnp.concatenate([wq, wk, wv, wo, wi, wf], axis=1).astype(BF16)
    bg = jnp.concatenate([jnp.pad(b_gate[0:M_HEADS], (0, LANES - M_HEADS)),
                          jnp.pad(b_gate[M_HEADS:], (0, LANES - M_HEADS))]).astype(F32)[None, :]
    return w_all, bg, g_norm.astype(F32)[None, :], w_out.astype(BF16)


P_L = LANES
P_AUG = M_DV + 16
P_GATE_COPIES = 3


def _mlstm_prompt_kernel(x_ref, wqt_ref, wk_ref, wvt_ref, wot_ref, wg_ref, bg_ref, gn_ref, wout_ref, lg_ref, lb_ref,
                         y_ref, cout_ref, mout_ref,
                         qt_s, k_s, vt_s, ot_s, g_s, ht_s, hn_s, c_s, m_s, *, alpha, tt, n_j):
    j = pl.program_id(1)
    L = P_L
    H = M_HEADS
    hv = H * M_DV

    @pl.when(j == 0)
    def _():
        c_s[...] = jnp.zeros_like(c_s)
        m_s[...] = jnp.zeros_like(m_s)

    x = x_ref[...]
    xb = x.astype(BF16)
    qt_s[...] = _dot_nt(wqt_ref[...], xb).astype(BF16)
    for cb in range(0, hv, A_COL_BLOCK):
        k_s[:, cb:cb + A_COL_BLOCK] = _dot(xb, wk_ref[:, cb:cb + A_COL_BLOCK]).astype(BF16)
        vt_s[cb:cb + A_COL_BLOCK, :] = _dot_nt(wvt_ref[cb:cb + A_COL_BLOCK, :], xb)
        ot_s[cb:cb + A_COL_BLOCK, :] = _dot_nt(wot_ref[cb:cb + A_COL_BLOCK, :], xb)
    g_s[...] = _dot(xb, wg_ref[...]) + bg_ref[...]

    row_i = lax.broadcasted_iota(jnp.int32, (L, LANES), 0)
    lane_i = lax.broadcasted_iota(jnp.int32, (L, LANES), 1)
    tri = (row_i >= lane_i).astype(F32)
    key8 = lax.broadcasted_iota(jnp.int32, (L, H * L), 0)
    qry8 = lax.broadcasted_iota(jnp.int32, (L, H * L), 1) & (L - 1)
    causal8 = key8 <= qry8
    ones_rows = jnp.ones((P_AUG - M_DV, L), F32)

    for u in range(tt // L):
        r0 = u * L
        ts = slice(r0, r0 + L)
        gi = g_s[ts, 0:LANES]
        gf = g_s[ts, LANES:2 * LANES]
        lf = jnp.minimum(gf, 0.0) - jnp.log1p(jnp.exp(-jnp.abs(gf)))
        bc = lax.dot_general(tri, lf, (((1,), (0,)), ((), ())), precision=HIGHEST,
                             preferred_element_type=F32)
        rs = gi - bc
        cm = rs
        k = 1
        while k < L:
            cm = jnp.maximum(cm, jnp.where(row_i >= k, pltpu.roll(cm, k, 0), NEG_INF))
            k *= 2
        packed = jnp.where(lane_i < H, rs, jnp.where(lane_i < 2 * H, cm, bc))
        packed_t = packed.T
        rs_t = packed_t[0:H]
        cm_t = packed_t[H:2 * H]
        b_t = packed_t[2 * H:3 * H]
        m0 = m_s[...]
        a_t = jnp.maximum(m0, cm_t)
        w_inter = jnp.exp(m0 - a_t)
        emx = jnp.exp(-(b_t + a_t))
        m_new = jnp.broadcast_to((b_t + a_t)[:, L - 1:L], (H, L))
        b_last = jnp.broadcast_to(b_t[:, L - 1:L], (H, L))
        w_end = jnp.exp(rs_t + (b_last - m_new))
        decay = jnp.exp(b_last + m0 - m_new)
        m_s[...] = m_new
        a_cat = jnp.concatenate([a_t[h:h + 1, :] for h in range(H)], axis=1)
        rs_cat = jnp.concatenate([jnp.broadcast_to(rs[:, h:h + 1], (L, L)) for h in range(H)], axis=1)
        s_cat = jnp.concatenate(
            [_dot(k_s[ts, h * LANES:h * LANES + M_DK], qt_s[h * M_DK:(h + 1) * M_DK, ts]) for h in range(H)], axis=1)
        p_cat = (s_cat * jnp.exp(jnp.where(causal8, rs_cat - a_cat, NEG_INF))).astype(BF16)
        for h in range(H):
            va = jnp.concatenate([vt_s[h * M_DV:(h + 1) * M_DV, ts], ones_rows], axis=0)
            ct = c_s[h]
            qw = (qt_s[h * M_DK:(h + 1) * M_DK, ts].astype(F32) * w_inter[h:h + 1, :]).astype(BF16)
            lhs = jnp.concatenate([va.astype(BF16), ct.astype(BF16)], axis=1)
            rhs = jnp.concatenate([p_cat[:, h * L:(h + 1) * L], qw], axis=0)
            nd = _dot(lhs, rhs)
            inv = 1.0 / jnp.maximum(jnp.abs(nd[M_DV:M_DV + 1, :]), emx[h:h + 1, :])
            ht_s[h * M_DV:(h + 1) * M_DV, ts] = nd[0:M_DV, :] * inv
            upd = _dot((va * w_end[h:h + 1, :]).astype(BF16), k_s[ts, h * LANES:h * LANES + M_DK])
            c_s[h] = decay[h:h + 1, 0:M_DK] * ct + upd

    gn = gn_ref[...]
    for h in range(H):
        hs = slice(h * M_DV, (h + 1) * M_DV)
        hh = ht_s[hs, :]
        scale = lax.rsqrt(jnp.mean(hh * hh, axis=0, keepdims=True) + HEAD_NORM_EPS)
        gcol = jnp.concatenate([gn[hs, :]] * (tt // LANES), axis=1)
        hn_s[hs, :] = (hh * scale * gcol * jax.nn.sigmoid(ot_s[hs, :])).astype(BF16)
    mix = _dot_tn(hn_s[...], wout_ref[...])
    y_ref[...] = _layer_norm(alpha * x + mix, lg_ref[...], lb_ref[...])

    @pl.when(j == n_j - 1)
    def _():
        cout_ref[0] = c_s[...]
        mout_ref[0] = m_s[...]


def _mlstm_prompt_layer(x2d, w_in, b_gate, g_norm, w_out, ln_g, ln_b, *, alpha, batch, seq, tt):
    n, d = x2d.shape
    assert n == batch * seq and seq % tt == 0 and tt % P_L == 0 and 3 * M_HEADS <= LANES
    n_j = seq // tt
    hk = M_HEADS * M_DK
    hv = M_HEADS * M_DV
    wqt = w_in[:, 0:hk].T.astype(BF16)
    wk = jnp.pad((w_in[:, hk:2 * hk] * (M_DK ** -0.5)).reshape(d, M_HEADS, M_DK),
                 ((0, 0), (0, 0), (0, LANES - M_DK))).reshape(d, M_HEADS * LANES).astype(BF16)
    wvt = w_in[:, 2 * hk:2 * hk + hv].T.astype(BF16)
    wot = w_in[:, 2 * hk + hv:2 * hk + 2 * hv].T.astype(BF16)
    rep = lambda a: jnp.pad(jnp.tile(a, (1, P_GATE_COPIES)), ((0, 0), (0, LANES - P_GATE_COPIES * M_HEADS)))
    g0 = 2 * hk + 2 * hv
    wg = jnp.concatenate([rep(w_in[:, g0:g0 + M_HEADS]), rep(w_in[:, g0 + M_HEADS:])], axis=1).astype(BF16)
    bg = jnp.concatenate([rep(b_gate[None, 0:M_HEADS]), rep(b_gate[None, M_HEADS:])], axis=1).astype(F32)
    gn = jnp.broadcast_to(g_norm.astype(F32)[:, None], (hv, LANES))
    wout = w_out.astype(BF16)
    consts = (wqt, wk, wvt, wot, wg, bg, gn, wout, ln_g, ln_b)
    y, c_out, m_out = pl.pallas_call(
        functools.partial(_mlstm_prompt_kernel, alpha=alpha, tt=tt, n_j=n_j),
        grid=(batch, n_j),
        in_specs=[pl.BlockSpec((tt, d), lambda b, j: (b * n_j + j, 0))] + [_const_spec(a.shape) for a in consts],
        out_specs=[
            pl.BlockSpec((tt, d), lambda b, j: (b * n_j + j, 0)),
            pl.BlockSpec((1, M_HEADS, P_AUG, M_DK), lambda b, j: (b, 0, 0, 0)),
            pl.BlockSpec((1, M_HEADS, LANES), lambda b, j: (b, 0, 0)),
        ],
        out_shape=[
            jax.ShapeDtypeStruct((n, d), F32),
            jax.ShapeDtypeStruct((batch, M_HEADS, P_AUG, M_DK), F32),
            jax.ShapeDtypeStruct((batch, M_HEADS, LANES), F32),
        ],
        scratch_shapes=[
            pltpu.VMEM((hk, tt), BF16),
            pltpu.VMEM((tt, M_HEADS * LANES), BF16),
            pltpu.VMEM((hv, tt), F32),
            pltpu.VMEM((hv, tt), F32),
            pltpu.VMEM((tt, 2 * LANES), F32),
            pltpu.VMEM((hv, tt), F32),
            pltpu.VMEM((hv, tt), BF16),
            pltpu.VMEM((M_HEADS, P_AUG, M_DK), F32),
            pltpu.VMEM((M_HEADS, LANES), F32),
        ],
        compiler_params=pltpu.CompilerParams(
            dimension_semantics=("arbitrary", "arbitrary"), vmem_limit_bytes=VMEM_LIMIT_BYTES),
        name="mlstm_prompt",
    )(x2d, *consts)
    c_new = jnp.swapaxes(c_out[:, :, 0:M_DV, :], -1, -2)
    n_new = c_out[:, :, M_DV, :]
    m_new = m_out[:, :, 0]
    return y, c_new, n_new, m_new


PAIRS = GROUP // 2


def _attn_unit(q_rows, k_lo, k_hi, v_lo, v_hi, bias_lo, bias_hi, sink_lo, sink_hi, key_ok):
    outs = []
    for k_op, v_op, bias, sink in ((k_lo, v_lo, bias_lo, sink_lo), (k_hi, v_hi, bias_hi, sink_hi)):
        s = _dot_nt(q_rows, k_op) + bias
        if key_ok is not None:
            s = jnp.where(key_ok, s, NEG_INF)
        mx = jnp.maximum(jnp.max(s, axis=-1, keepdims=True), sink)
        p = jnp.exp(s - mx)
        den = jnp.sum(p, axis=-1, keepdims=True) + jnp.exp(sink - mx)
        outs.append(_dot(p.astype(BF16), v_op) / den)
    return outs[0] + outs[1]


def _split_kv(kv):
    lane = lax.broadcasted_iota(jnp.int32, (1, LANES), 1)
    low = lane < HEAD_DIM
    res = []
    kk = kv[:, 0:LANES]
    vv = kv[:, LANES:2 * LANES]
    kk_r = pltpu.roll(kk, HEAD_DIM, 1)
    vv_r = pltpu.roll(vv, HEAD_DIM, 1)
    z = jnp.zeros_like(kk)
    res.append((jnp.where(low, kk, z), jnp.where(low, z, kk_r), jnp.where(low, vv, z), jnp.where(low, z, vv_r)))
    res.append((jnp.where(low, kk_r, z), jnp.where(low, z, kk), jnp.where(low, vv_r, z), jnp.where(low, z, vv)))
    return [tuple(a.astype(BF16) for a in grp) for grp in res]


SWA_UNIT = 2 * CHUNK
SWA_KEYS = WINDOW + SWA_UNIT


def _swa_prompt_kernel(x_ref, kc_ref, kp_ref, vtc_ref, vtp_ref, wqt_ref, wo_ref, bias_ref, sink_ref, lg_ref, lb_ref,
                       y_ref, qt_s, ot_s, k_s, vt_s, *, alpha, tq):
    j = pl.program_id(1)
    x = x_ref[...]
    xb = x.astype(BF16)
    qt_s[...] = _dot_nt(wqt_ref[...], xb).astype(BF16)
    k_s[0:WINDOW, :] = kp_ref[...].astype(BF16)
    k_s[WINDOW:WINDOW + tq, :] = kc_ref[...].astype(BF16)
    vt_s[:, 0:WINDOW] = vtp_ref[...]
    vt_s[:, WINDOW:WINDOW + tq] = vtc_ref[...]
    zeros = jnp.zeros((HEAD_DIM, GROUP * SWA_UNIT), BF16)
    key_i = lax.broadcasted_iota(jnp.int32, (SWA_KEYS, GROUP * SWA_UNIT), 0)
    for u in range(tq // SWA_UNIT):
        r0 = u * SWA_UNIT
        k_win = k_s[r0:r0 + SWA_KEYS, :]
        vt_win = vt_s[:, r0:r0 + SWA_KEYS]
        for g in range(N_KV_HEADS):
            heads = range(g * GROUP, (g + 1) * GROUP)
            qt_g = jnp.concatenate(
                [qt_s[h * HEAD_DIM:(h + 1) * HEAD_DIM, r0:r0 + SWA_UNIT] for h in heads], axis=1)
            qz = jnp.concatenate([qt_g, zeros] if g == 0 else [zeros, qt_g], axis=0)
            s_t = _dot(k_win, qz) + bias_ref[g]
            if r0 < WINDOW:
                s_t = jnp.where(j * tq + r0 - WINDOW + key_i >= 0, s_t, NEG_INF)
            sink = sink_ref[g]
            mx = jnp.maximum(jnp.max(s_t, axis=0, keepdims=True), sink)
            p = jnp.exp(s_t - mx)
            den = jnp.sum(p, axis=0, keepdims=True) + jnp.exp(sink - mx)
            o_t = (_dot(vt_win[g * HEAD_DIM:(g + 1) * HEAD_DIM, :], p.astype(BF16)) / den).astype(BF16)
            for i, h in enumerate(heads):
                ot_s[h * HEAD_DIM:(h + 1) * HEAD_DIM, r0:r0 + SWA_UNIT] = o_t[:, i * SWA_UNIT:(i + 1) * SWA_UNIT]
    mix = _dot_tn(ot_s[...], wo_ref[...])
    y_ref[...] = _layer_norm(alpha * x + mix, lg_ref[...], lb_ref[...])


def _swa_prompt_layer(x2d, kv2d, vt, wqt, wo, bias, sinks, ln_g, ln_b, *, alpha, batch, seq, tq):
    n, d = x2d.shape
    assert n == batch * seq and seq % tq == 0 and tq % SWA_UNIT == 0 and WINDOW == LANES
    n_j = seq // tq
    per = tq // WINDOW
    prev = lambda b, j: jnp.maximum((b * n_j + j) * per - 1, 0)
    return pl.pallas_call(
        functools.partial(_swa_prompt_kernel, alpha=alpha, tq=tq),
        grid=(batch, n_j),
        in_specs=[
            pl.BlockSpec((tq, d), lambda b, j: (b * n_j + j, 0)),
            pl.BlockSpec((tq, LANES), lambda b, j: (b * n_j + j, 0)),
            pl.BlockSpec((WINDOW, LANES), lambda b, j: (prev(b, j), 0)),
            pl.BlockSpec((LANES, tq), lambda b, j: (0, b * n_j + j)),
            pl.BlockSpec((LANES, WINDOW), lambda b, j: (0, prev(b, j))),
            _const_spec(wqt.shape),
            _const_spec(wo.shape),
            _const_spec(bias.shape),
            _const_spec(sinks.shape),
            _const_spec(ln_g.shape),
            _const_spec(ln_b.shape),
        ],
        out_specs=pl.BlockSpec((tq, d), lambda b, j: (b * n_j + j, 0)),
        out_shape=jax.ShapeDtypeStruct((n, d), F32),
        scratch_shapes=[
            pltpu.VMEM((d, tq), BF16),
            pltpu.VMEM((d, tq), BF16),
            pltpu.VMEM((WINDOW + tq, LANES), BF16),
            pltpu.VMEM((LANES, WINDOW + tq), BF16),
        ],
        compiler_params=pltpu.CompilerParams(
            dimension_semantics=("arbitrary", "arbitrary"), vmem_limit_bytes=VMEM_LIMIT_BYTES),
        name="swa_prompt",
    )(x2d, kv2d, kv2d, vt, vt, wqt, wo, bias, sinks, ln_g, ln_b)


def _swa_prompt_tables(sinks):
    slopes = jnp.exp2(-8.0 * jnp.arange(1, N_Q_HEADS + 1, dtype=F32) / N_Q_HEADS)
    q = jnp.arange(SWA_UNIT)[None, :]
    kx = jnp.arange(SWA_KEYS)[:, None]
    dist = jnp.abs(q + WINDOW - kx).astype(F32)
    first = q < CHUNK
    visible = (first & (kx < WINDOW + CHUNK)) | (~first & (kx >= CHUNK))
    bias = jnp.where(visible[None], -slopes[:, None, None] * dist[None], NEG_INF)
    bias = bias.reshape(N_KV_HEADS, GROUP, SWA_KEYS, SWA_UNIT).transpose(0, 2, 1, 3)
    sink = jnp.broadcast_to(sinks.astype(F32).reshape(N_KV_HEADS, 1, GROUP, 1), (N_KV_HEADS, 1, GROUP, SWA_UNIT))
    return (bias.reshape(N_KV_HEADS, SWA_KEYS, GROUP * SWA_UNIT), sink.reshape(N_KV_HEADS, 1, GROUP * SWA_UNIT))


def _swa_sample_kernel(x_ref, kvn_ref, kc_ref, vc_ref, wq_ref, wo_ref, bias_ref, sink_ref, lg_ref, lb_ref,
                       y_ref, q_s, o_s, *, alpha, batch, seq):
    x = x_ref[...]
    xb = x.astype(BF16)
    q_s[...] = _dot(xb, wq_ref[...]).astype(BF16)
    for b in range(batch):
        r0 = b * seq
        kv_new = kvn_ref[r0:r0 + seq, :]
        kv_old = jnp.concatenate([kc_ref[b], vc_ref[b]], axis=1)
        groups = _split_kv(jnp.concatenate([kv_old, kv_new], axis=0))
        for g in range(N_KV_HEADS):
            q_rows = jnp.concatenate(
                [q_s[r0:r0 + seq, (g * PAIRS + pp) * LANES:(g * PAIRS + pp + 1) * LANES] for pp in range(PAIRS)],
                axis=0)
            ks = groups[g]
            o = _attn_unit(q_rows, ks[0], ks[1], ks[2], ks[3], bias_ref[g, 0], bias_ref[g, 1],
                           sink_ref[g, 0], sink_ref[g, 1], None)
            for pp in range(PAIRS):
                o_s[r0:r0 + seq, (g * PAIRS + pp) * LANES:(g * PAIRS + pp + 1) * LANES] = (
                    o[pp * seq:(pp + 1) * seq, :].astype(BF16))
    mix = _dot(o_s[...], wo_ref[...])
    y_ref[...] = _layer_norm(alpha * x + mix, lg_ref[...], lb_ref[...])


def _swa_sample_layer(x2d, kv_new, k_cache, v_cache, wq, wo, bias, sink, ln_g, ln_b, *, alpha, batch, seq):
    n, d = x2d.shape
    assert n == batch * seq
    args = (x2d, kv_new, k_cache, v_cache, wq, wo, bias, sink, ln_g, ln_b)
    return pl.pallas_call(
        functools.partial(_swa_sample_kernel, alpha=alpha, batch=batch, seq=seq),
        grid=(1,),
        in_specs=[_const_spec(a.shape) for a in args],
        out_specs=pl.BlockSpec((n, d), lambda i: (0, 0)),
        out_shape=jax.ShapeDtypeStruct((n, d), F32),
        scratch_shapes=[pltpu.VMEM((n, d), BF16), pltpu.VMEM((n, d), BF16)],
        compiler_params=pltpu.CompilerParams(
            dimension_semantics=("arbitrary",), vmem_limit_bytes=VMEM_LIMIT_BYTES),
        name="swa_sample",
    )(*args)


def _attn_tables(sinks, ql, nk):
    slopes = jnp.exp2(-8.0 * jnp.arange(1, N_Q_HEADS + 1, dtype=F32) / N_Q_HEADS)
    dist = jnp.abs(jnp.arange(ql)[:, None] - jnp.arange(nk)[None, :] + (nk - ql)).astype(F32)
    head = (jnp.arange(N_KV_HEADS)[:, None, None] * GROUP + 2 * jnp.arange(PAIRS)[None, None, :]
            + jnp.arange(2)[None, :, None])
    bias = -slopes[head][..., None, None] * dist
    sink = jnp.broadcast_to(sinks.astype(F32)[head][..., None, None], head.shape + (ql, 1))
    return (bias.reshape(N_KV_HEADS, 2, PAIRS * ql, nk), sink.reshape(N_KV_HEADS, 2, PAIRS * ql, 1))


def _trunk(x, c0s, n0s, m0s, k_cache, v_cache, params, *, is_prompt):
    (w_in_a, b_gate_a, g_norm_a, w_out_a, w_kv, w_q_b, sinks_b, w_out_b, w_gu, w_down, ln_g, ln_b) = params
    batch, seq, d = x.shape
    depth = w_gu.shape[0]
    n_a = w_in_a.shape[0]
    alpha = (2 * depth) ** 0.25
    d_ff = w_down.shape[1]
    n = batch * seq
    x2d = x.reshape(n, d)
    tm = min(512, n)
    if is_prompt:
        L, bb_n, tt = 128, 1, 512
    else:
        L, bb_n, tt = seq, 8, seq
    cs, ns, ms = [], [], []
    kv2d = None
    for layer in range(depth):
        row = lambda a: a.astype(F32)[None, :]
        if layer < n_a and is_prompt:
            x2d, c, nn, m = _mlstm_prompt_layer(
                x2d, w_in_a[layer], b_gate_a[layer], g_norm_a[layer], w_out_a[layer],
                row(ln_g[layer, 0]), row(ln_b[layer, 0]), alpha=alpha, batch=batch, seq=seq, tt=tt)
            cs.append(c)
            ns.append(nn)
            ms.append(m)
        elif layer < n_a:
            w_all, bg, gn, wo = _prep_mlstm_weights(w_in_a[layer], b_gate_a[layer], g_norm_a[layer], w_out_a[layer])
            m0 = jnp.broadcast_to(m0s[layer].astype(F32)[:, :, None, None], (batch, M_HEADS, 8, LANES))
            x2d, c, nn, m = _mlstm_layer(
                x2d, c0s[layer].astype(F32), n0s[layer].astype(F32)[..., None], m0, w_all, bg, gn, wo,
                row(ln_g[layer, 0]), row(ln_b[layer, 0]), alpha=alpha, batch=batch, seq=seq, L=L, bb_n=bb_n, tt=tt)
            cs.append(c)
            ns.append(nn)
            ms.append(m)
        else:
            jb = layer - n_a
            if kv2d is None:
                n_v = N_KV_HEADS * HEAD_DIM
                kv2d, vt = _kv_proj(x2d, w_kv.astype(BF16), w_kv[:, n_v:].T.astype(BF16), tm=tm)
            wq = (w_q_b[jb] * (HEAD_DIM ** -0.5)).astype(BF16)
            wo = w_out_b[jb].astype(BF16)
            if is_prompt:
                bias, sink = _swa_prompt_tables(sinks_b[jb])
                x2d = _swa_prompt_layer(x2d, kv2d, vt, wq.T, wo, bias, sink,
                                        row(ln_g[layer, 0]), row(ln_b[layer, 0]),
                                        alpha=alpha, batch=batch, seq=seq, tq=512)
            else:
                w_rows = k_cache.shape[1]
                bias, sink = _attn_tables(sinks_b[jb], seq, w_rows + seq)
                x2d = _swa_sample_layer(
                    x2d, kv2d, k_cache.astype(F32).reshape(batch, w_rows, N_KV_HEADS * HEAD_DIM),
                    v_cache.astype(F32).reshape(batch, w_rows, N_KV_HEADS * HEAD_DIM), wq, wo, bias, sink,
                    row(ln_g[layer, 0]), row(ln_b[layer, 0]), alpha=alpha, batch=batch, seq=seq)
        x2d = _ffn(x2d, w_gu[layer][:, :d_ff].astype(BF16), w_gu[layer][:, d_ff:].astype(BF16),
                   w_down[layer].astype(BF16), row(ln_g[layer, 1]), row(ln_b[layer, 1]), alpha=alpha, tm=tm)
    kv = kv2d.reshape(batch, seq, 2, N_KV_HEADS, HEAD_DIM)
    return (x2d.reshape(batch, seq, d), jnp.stack(cs), jnp.stack(ns), jnp.stack(ms), kv[:, :, 0], kv[:, :, 1])


def kernel(x_prompt, x_sample, state_C, state_n, state_m, cache_k, cache_v, w_in_a, b_gate_a, g_norm_a,
           w_out_a, w_kv, w_q_b, sinks_b, w_out_b, w_gu, w_down, ln_g, ln_b):
    params = (w_in_a, b_gate_a, g_norm_a, w_out_a, w_kv, w_q_b, sinks_b, w_out_b, w_gu, w_down, ln_g, ln_b)
    y_p, p_c, p_n, p_m, p_k, p_v = _trunk(x_prompt, None, None, None, None, None, params, is_prompt=True)
    y_s, s_c, s_n, s_m, s_k, s_v = _trunk(x_sample, state_C, state_n, state_m, cache_k, cache_v, params,
                                          is_prompt=False)
    rows = min(WINDOW, x_prompt.shape[1])
    return (y_p, y_s, p_c, p_n, p_m, p_k[:, -rows:], p_v[:, -rows:], s_c, s_n, s_m, s_k, s_v)
```

```python
import functools

import jax
import jax.numpy as jnp
from jax import lax
from jax.experimental import pallas as pl
from jax.experimental.pallas import tpu as pltpu

F32 = jnp.float32
BF16 = jnp.bfloat16

CHUNK = 64
WINDOW = 128
M_HEADS = 8
M_DK = 64
M_DV = 128
N_Q_HEADS = 16
N_KV_HEADS = 2
HEAD_DIM = 64
GROUP = N_Q_HEADS // N_KV_HEADS
LN_EPS = 1e-5
HEAD_NORM_EPS = 1e-6

LANES = 128
LOG2_E = 1.4426950408889634
VMEM_LIMIT_BYTES = 56 * 1024 * 1024

NEG_INF = float("-inf")
HIGHEST = lax.Precision.HIGHEST


def _const_spec(shape):
    nd = len(shape)
    return pl.BlockSpec(shape, lambda *_: (0,) * nd, pipeline_mode=pl.Buffered(1))


def _layer_norm(y, g, b):
    mu = jnp.mean(y, axis=-1, keepdims=True)
    yc = y - mu
    var = jnp.mean(yc * yc, axis=-1, keepdims=True)
    return yc * lax.rsqrt(var + LN_EPS) * g + b


def _dot(a, b):
    return jnp.dot(a, b, preferred_element_type=F32)


def _dot_nt(a, b):
    return lax.dot_general(a, b, (((1,), (1,)), ((), ())), preferred_element_type=F32)


def _dot_tn(a, b):
    return lax.dot_general(a, b, (((0,), (0,)), ((), ())), preferred_element_type=F32)


FF_BLOCK = 256


def _ffn_kernel(x_ref, wgu_ref, wd_ref, g_ref, b_ref, o_ref, acc_ref, *, alpha, d_ff):
    x = x_ref[...]
    xb = x.astype(BF16)
    for j in range(d_ff // FF_BLOCK):
        sl = slice(j * FF_BLOCK, (j + 1) * FF_BLOCK)
        su = slice(d_ff + j * FF_BLOCK, d_ff + (j + 1) * FF_BLOCK)
        g = _dot(xb, wgu_ref[:, sl])
        u = _dot(xb, wgu_ref[:, su])
        h = (g * jax.nn.sigmoid(g) * u).astype(BF16)
        part = _dot(h, wd_ref[sl, :])
        if j == 0:
            acc_ref[...] = part
        else:
            acc_ref[...] += part
    o_ref[...] = _layer_norm(alpha * x + acc_ref[...], g_ref[...], b_ref[...])


def _layer_spec(shape, layer):
    nd = len(shape) - 1
    return pl.BlockSpec((None,) + tuple(shape[1:]), lambda *_: (layer,) + (0,) * nd, pipeline_mode=pl.Buffered(1))


def _ffn(x2d, w_gu, w_down, ln_g, ln_b, *, layer, alpha, tm):
    n, d = x2d.shape
    d_ff = w_down.shape[1]
    assert n % tm == 0 and d_ff % FF_BLOCK == 0 and w_gu.shape[2] == 2 * d_ff
    return pl.pallas_call(
        functools.partial(_ffn_kernel, alpha=alpha, d_ff=d_ff),
        grid=(n // tm,),
        in_specs=[
            pl.BlockSpec((tm, d), lambda i: (i, 0)),
            _layer_spec(w_gu.shape, layer),
            _layer_spec(w_down.shape, layer),
            _const_spec(ln_g.shape),
            _const_spec(ln_b.shape),
        ],
        out_specs=pl.BlockSpec((tm, d), lambda i: (i, 0)),
        out_shape=jax.ShapeDtypeStruct((n, d), F32),
        scratch_shapes=[pltpu.VMEM((tm, d), F32)],
        compiler_params=pltpu.CompilerParams(
            dimension_semantics=("arbitrary",), vmem_limit_bytes=VMEM_LIMIT_BYTES),
        name="ffn",
    )(x2d, w_gu, w_down, ln_g, ln_b)


def _kv_kernel(x_ref, w_ref, o_ref):
    o_ref[...] = _dot(x_ref[...].astype(BF16), w_ref[...])


def _kv_t_kernel(x_ref, w_ref, wvt_ref, o_ref, vt_ref):
    xb = x_ref[...].astype(BF16)
    o_ref[...] = _dot(xb, w_ref[...])
    vt_ref[...] = _dot_nt(wvt_ref[...], xb).astype(BF16)


def _kv_proj(x2d, w_kv, *, tm, with_vt):
    n, d = x2d.shape
    nk = w_kv.shape[1]
    assert n % tm == 0
    params = pltpu.CompilerParams(dimension_semantics=("arbitrary",), vmem_limit_bytes=VMEM_LIMIT_BYTES)
    x_spec = pl.BlockSpec((tm, d), lambda i: (i, 0))
    kv_spec = pl.BlockSpec((tm, nk), lambda i: (i, 0))
    w_kv_b = w_kv.astype(BF16)
    if not with_vt:
        kv = pl.pallas_call(
            _kv_kernel, grid=(n // tm,), in_specs=[x_spec, _const_spec(w_kv.shape)], out_specs=kv_spec,
            out_shape=jax.ShapeDtypeStruct((n, nk), F32), compiler_params=params, name="kv_proj",
        )(x2d, w_kv_b)
        return kv, None
    nv = nk // 2
    w_vt = w_kv[:, nv:].T.astype(BF16)
    return pl.pallas_call(
        _kv_t_kernel,
        grid=(n // tm,),
        in_specs=[x_spec, _const_spec(w_kv.shape), _const_spec(w_vt.shape)],
        out_specs=[kv_spec, pl.BlockSpec((nv, tm), lambda i: (0, i))],
        out_shape=[jax.ShapeDtypeStruct((n, nk), F32), jax.ShapeDtypeStruct((nv, n), BF16)],
        compiler_params=params,
        name="kv_proj_t",
    )(x2d, w_kv_b, w_vt)


A_QOFF = 0
A_KOFF = M_HEADS * LANES
A_VOFF = 2 * M_HEADS * LANES
A_OOFF = 3 * M_HEADS * LANES
A_GOFF = 4 * M_HEADS * LANES
A_COLS = A_GOFF + 2 * LANES
A_COL_BLOCK = 512


def _mlstm_kernel(x_ref, c0_ref, n0_ref, m0_ref, w_ref, bg_ref, gn_ref, wo_ref, lg_ref, lb_ref,
                  y_ref, cout_ref, mout_ref,
                  q_s, k_s, v_s, o_s, g_s, h_s, hb_s, c_s, m_s,
                  *, alpha, L, bb_n, tt, n_j):
    j = pl.program_id(1)
    rows = bb_n * tt
    units_per_b = tt // L
    hd = M_HEADS * LANES

    @pl.when(j == 0)
    def _():
        lane = lax.broadcasted_iota(jnp.int32, (1, 1, 1, LANES), 3)
        c_s[:, :, :, 0:LANES] = c0_ref[...]
        c_s[:, :, :, LANES:2 * LANES] = jnp.where(lane == 0, n0_ref[...], 0.0)
        m_s[...] = m0_ref[...]

    x = x_ref[...]
    xb = x.astype(BF16)
    for cb in range(0, hd, A_COL_BLOCK):
        q_s[:, cb:cb + A_COL_BLOCK] = _dot(xb, w_ref[:, A_QOFF + cb:A_QOFF + cb + A_COL_BLOCK]).astype(BF16)
        k_s[:, cb:cb + A_COL_BLOCK] = _dot(xb, w_ref[:, A_KOFF + cb:A_KOFF + cb + A_COL_BLOCK])
        v_s[:, cb:cb + A_COL_BLOCK] = _dot(xb, w_ref[:, A_VOFF + cb:A_VOFF + cb + A_COL_BLOCK]).astype(BF16)
        o_s[:, cb:cb + A_COL_BLOCK] = _dot(xb, w_ref[:, A_OOFF + cb:A_OOFF + cb + A_COL_BLOCK])
    g_s[...] = _dot(xb, w_ref[:, A_GOFF:A_GOFF + 2 * LANES]) + bg_ref[...]

    row_i = lax.broadcasted_iota(jnp.int32, (L, L), 0)
    col_i = lax.broadcasted_iota(jnp.int32, (L, L), 1)
    causal = row_i >= col_i
    tri = causal.astype(F32)
    ones_col = jnp.where(lax.broadcasted_iota(jnp.int32, (L, LANES), 1) == 0, 1.0, 0.0).astype(BF16)

    def unit(u, carry):
        r0 = pl.multiple_of(u * L, L)
        bb = u // units_per_b if bb_n > 1 else 0
        gi = g_s[pl.ds(r0, L), 0:LANES]
        gf = g_s[pl.ds(r0, L), LANES:2 * LANES]
        lf = jnp.minimum(gf, 0.0) - jnp.log1p(jnp.exp(-jnp.abs(gf)))
        bc = lax.dot_general(tri, lf, (((1,), (0,)), ((), ())), precision=HIGHEST,
                             preferred_element_type=F32)
        rs = gi - bc
        rs_t = rs.T
        b_last = bc[L - 1:L, :]
        for h in range(M_HEADS):
            hs = slice(h * LANES, (h + 1) * LANES)
            qh = q_s[pl.ds(r0, L), hs]
            kh = k_s[pl.ds(r0, L), hs]
            vh = v_s[pl.ds(r0, L), hs]
            m0 = m_s[bb, h, 0:1, 0:1]
            c_aug = c_s[bb, h]
            b_col = bc[:, h:h + 1]
            logd = jnp.where(causal, b_col + rs_t[h:h + 1, :], NEG_INF)
            inter = b_col + m0
            mx = jnp.maximum(inter, jnp.max(logd, axis=1, keepdims=True))
            d = jnp.exp(logd - mx)
            w_inter = jnp.exp(inter - mx)
            s = _dot_nt(qh, kh.astype(BF16))
            p = (s * d).astype(BF16)
            v_aug = jnp.concatenate([vh, ones_col], axis=1)
            numden = w_inter * _dot(qh[:, 0:M_DK], c_aug.astype(BF16)) + _dot(p, v_aug)
            num = numden[:, 0:M_DV]
            den = numden[:, M_DV:M_DV + 1]
            h_s[pl.ds(r0, L), hs] = num / jnp.maximum(jnp.abs(den), jnp.exp(-mx))
            m_new = mx[L - 1:L, :]
            bl = b_last[:, h:h + 1]
            w_end = jnp.exp(rs[:, h:h + 1] + (bl - m_new))
            decay = jnp.exp(bl + m0 - m_new)
            wk = (kh[:, 0:M_DK] * w_end).astype(BF16)
            c_s[bb, h] = decay * c_aug + _dot_tn(wk, v_aug)
            m_s[bb, h] = jnp.broadcast_to(m_new, (8, LANES))
        return carry

    lax.fori_loop(0, rows // L, unit, 0)

    for h in range(M_HEADS):
        hs = slice(h * LANES, (h + 1) * LANES)
        hh = h_s[:, hs]
        hn = hh * lax.rsqrt(jnp.mean(hh * hh, axis=-1, keepdims=True) + HEAD_NORM_EPS)
        hb_s[:, hs] = (hn * gn_ref[:, hs] * jax.nn.sigmoid(o_s[:, hs])).astype(BF16)
    mix = _dot(hb_s[...], wo_ref[...])
    y_ref[...] = _layer_norm(alpha * x + mix, lg_ref[...], lb_ref[...])

    @pl.when(j == n_j - 1)
    def _():
        cout_ref[...] = c_s[...]
        mout_ref[...] = m_s[...]


def _mlstm_layer(x2d, c0, n0, m0, w_all, b_gate, g_norm, w_out, ln_g, ln_b, *, alpha, batch, seq, L, bb_n, tt):
    n, d = x2d.shape
    assert n == batch * seq and seq % tt == 0 and tt % L == 0 and batch % bb_n == 0
    assert bb_n == 1 or tt == seq
    n_j = seq // tt
    rows = bb_n * tt
    hd = M_HEADS * LANES
    kern = functools.partial(_mlstm_kernel, alpha=alpha, L=L, bb_n=bb_n, tt=tt, n_j=n_j)
    st4 = lambda bi, j: (bi, 0, 0, 0)
    y, c_out, m_out = pl.pallas_call(
        kern,
        grid=(batch // bb_n, n_j),
        in_specs=[
            pl.BlockSpec((rows, d), lambda bi, j: (bi * n_j + j, 0)),
            pl.BlockSpec((bb_n, M_HEADS, M_DK, M_DV), st4),
            pl.BlockSpec((bb_n, M_HEADS, M_DK, 1), st4),
            pl.BlockSpec((bb_n, M_HEADS, 8, LANES), st4),
            _const_spec(w_all.shape),
            _const_spec(b_gate.shape),
            _const_spec(g_norm.shape),
            _const_spec(w_out.shape),
            _const_spec(ln_g.shape),
            _const_spec(ln_b.shape),
        ],
        out_specs=[
            pl.BlockSpec((rows, d), lambda bi, j: (bi * n_j + j, 0)),
            pl.BlockSpec((bb_n, M_HEADS, M_DK, 2 * LANES), st4),
            pl.BlockSpec((bb_n, M_HEADS, 8, LANES), st4),
        ],
        out_shape=[
            jax.ShapeDtypeStruct((n, d), F32),
            jax.ShapeDtypeStruct((batch, M_HEADS, M_DK, 2 * LANES), F32),
            jax.ShapeDtypeStruct((batch, M_HEADS, 8, LANES), F32),
        ],
        scratch_shapes=[
            pltpu.VMEM((rows, hd), BF16),
            pltpu.VMEM((rows, hd), F32),
            pltpu.VMEM((rows, hd), BF16),
            pltpu.VMEM((rows, hd), F32),
            pltpu.VMEM((rows, 2 * LANES), F32),
            pltpu.VMEM((rows, hd), F32),
            pltpu.VMEM((rows, hd), BF16),
            pltpu.VMEM((bb_n, M_HEADS, M_DK, 2 * LANES), F32),
            pltpu.VMEM((bb_n, M_HEADS, 8, LANES), F32),
        ],
        compiler_params=pltpu.CompilerParams(
            dimension_semantics=("arbitrary", "arbitrary"), vmem_limit_bytes=VMEM_LIMIT_BYTES),
        name="mlstm_layer",
    )(x2d, c0, n0, m0, w_all, b_gate, g_norm, w_out, ln_g, ln_b)
    c_new = c_out[..., 0:M_DV]
    n_new = c_out[..., M_DV]
    m_new = m_out[:, :, 0, 0]
    return y, c_new, n_new, m_new


def _prep_mlstm_weights(w_in, b_gate, g_norm, w_out):
    d = w_in.shape[0]
    hk = M_HEADS * M_DK
    hv = M_HEADS * M_DV
    wq = w_in[:, 0:hk].reshape(d, M_HEADS, M_DK)
    wk = w_in[:, hk:2 * hk].reshape(d, M_HEADS, M_DK) * (M_DK ** -0.5)
    pad = ((0, 0), (0, 0), (0, LANES - M_DK))
    wq = jnp.pad(wq, pad).reshape(d, M_HEADS * LANES)
    wk = jnp.pad(wk, pad).reshape(d, M_HEADS * LANES)
    wv = w_in[:, 2 * hk:2 * hk + hv]
    wo = w_in[:, 2 * hk + hv:2 * hk + 2 * hv]
    wi = jnp.pad(w_in[:, 2 * hk + 2 * hv:2 * hk + 2 * hv + M_HEADS], ((0, 0), (0, LANES - M_HEADS)))
    wf = jnp.pad(w_in[:, 2 * hk + 2 * hv + M_HEADS:], ((0, 0), (0, LANES - M_HEADS)))
    w_all = jnp.concatenate([wq, wk, wv, wo, wi, wf], axis=1).astype(BF16)
    bg = jnp.concatenate([jnp.pad(b_gate[0:M_HEADS], (0, LANES - M_HEADS)),
                          jnp.pad(b_gate[M_HEADS:], (0, LANES - M_HEADS))]).astype(F32)[None, :]
    return w_all, bg, g_norm.astype(F32)[None, :], w_out.astype(BF16)


P_L = LANES
P_AUG = M_DV + 16
P_GATE_COPIES = 3


def _mlstm_prompt_kernel(x_ref, wqt_ref, wk_ref, wvt_ref, wot_ref, wg_ref, bg_ref, gn_ref, wout_ref,
                         lg_ref, lb_ref, y_ref, cout_ref, mout_ref,
                         qt_s, k_s, vt_s, ot_s, g_s, ht_s, hn_s, c_s, m_s, *, alpha, tt, n_j, section):
    j = pl.program_id(1)
    L = P_L
    H = M_HEADS
    hv = H * M_DV

    @pl.when(j == 0)
    def _():
        c_s[...] = jnp.zeros_like(c_s)
        m_s[...] = jnp.zeros_like(m_s)

    row_i = lax.broadcasted_iota(jnp.int32, (L, LANES), 0)
    lane_i = lax.broadcasted_iota(jnp.int32, (L, LANES), 1)
    tri = (row_i >= lane_i).astype(F32)
    key8 = lax.broadcasted_iota(jnp.int32, (L, H * L), 0)
    qry8 = lax.broadcasted_iota(jnp.int32, (L, H * L), 1) & (L - 1)
    causal8 = key8 <= qry8
    ones_rows = jnp.ones((P_AUG - M_DV, L), F32)
    gn = gn_ref[...]

    for sec in range(tt // section):
        _mlstm_prompt_section(
            slice(sec * section, (sec + 1) * section),
            x_ref, wqt_ref, wk_ref, wvt_ref, wot_ref, wg_ref, bg_ref, wout_ref, lg_ref, lb_ref, y_ref,
            qt_s, k_s, vt_s, ot_s, g_s, ht_s, hn_s, c_s, m_s,
            alpha=alpha, consts=(row_i, lane_i, tri, causal8, ones_rows, gn))

    @pl.when(j == n_j - 1)
    def _():
        cout_ref[0] = c_s[...]
        mout_ref[0] = m_s[...]


def _mlstm_prompt_section(rs_, x_ref, wqt_ref, wk_ref, wvt_ref, wot_ref, wg_ref, bg_ref, wout_ref, lg_ref, lb_ref,
                          y_ref, qt_s, k_s, vt_s, ot_s, g_s, ht_s, hn_s, c_s, m_s, *, alpha, consts):
    row_i, lane_i, tri, causal8, ones_rows, gn = consts
    L = P_L
    H = M_HEADS
    hv = H * M_DV
    x = x_ref[rs_, :]
    xb = x.astype(BF16)
    qt_s[:, rs_] = _dot_nt(wqt_ref[...], xb).astype(BF16)
    for cb in range(0, hv, A_COL_BLOCK):
        k_s[rs_, cb:cb + A_COL_BLOCK] = _dot(xb, wk_ref[:, cb:cb + A_COL_BLOCK]).astype(BF16)
        vt_s[cb:cb + A_COL_BLOCK, rs_] = _dot_nt(wvt_ref[cb:cb + A_COL_BLOCK, :], xb)
        ot_s[cb:cb + A_COL_BLOCK, rs_] = _dot_nt(wot_ref[cb:cb + A_COL_BLOCK, :], xb)
    g_s[rs_, :] = _dot(xb, wg_ref[...]) + bg_ref[...]

    for u in range(rs_.start // L, rs_.stop // L):
        r0 = u * L
        ts = slice(r0, r0 + L)
        gi = g_s[ts, 0:LANES]
        gf = g_s[ts, LANES:2 * LANES]
        lf = jnp.minimum(gf, 0.0) - jnp.log1p(jnp.exp(-jnp.abs(gf)))
        bc = lax.dot_general(tri, lf, (((1,), (0,)), ((), ())), precision=HIGHEST,
                             preferred_element_type=F32)
        rs = gi - bc
        cm = rs
        k = 1
        while k < L:
            cm = jnp.maximum(cm, jnp.where(row_i >= k, pltpu.roll(cm, k, 0), NEG_INF))
            k *= 2
        packed = jnp.where(lane_i < H, rs, jnp.where(lane_i < 2 * H, cm, bc))
        packed_t = packed.T
        rs_t = packed_t[0:H]
        cm_t = packed_t[H:2 * H]
        b_t = packed_t[2 * H:3 * H]
        m0 = m_s[...]
        a_t = jnp.maximum(m0, cm_t)
        w_inter = jnp.exp(m0 - a_t)
        emx = jnp.exp(-(b_t + a_t))
        m_new = jnp.broadcast_to((b_t + a_t)[:, L - 1:L], (H, L))
        b_last = jnp.broadcast_to(b_t[:, L - 1:L], (H, L))
        w_end = jnp.exp(rs_t + (b_last - m_new))
        decay = jnp.exp(b_last + m0 - m_new)
        m_s[...] = m_new
        a_cat = jnp.concatenate([a_t[h:h + 1, :] for h in range(H)], axis=1)
        rs_cat = jnp.concatenate([jnp.broadcast_to(rs[:, h:h + 1], (L, L)) for h in range(H)], axis=1)
        s_cat = jnp.concatenate(
            [_dot(k_s[ts, h * LANES:h * LANES + M_DK], qt_s[h * M_DK:(h + 1) * M_DK, ts]) for h in range(H)], axis=1)
        p_cat = (s_cat * jnp.exp(jnp.where(causal8, rs_cat - a_cat, NEG_INF))).astype(BF16)
        for h in range(H):
            va = jnp.concatenate([vt_s[h * M_DV:(h + 1) * M_DV, ts], ones_rows], axis=0)
            ct = c_s[h]
            qw = (qt_s[h * M_DK:(h + 1) * M_DK, ts].astype(F32) * w_inter[h:h + 1, :]).astype(BF16)
            lhs = jnp.concatenate([va.astype(BF16), ct.astype(BF16)], axis=1)
            rhs = jnp.concatenate([p_cat[:, h * L:(h + 1) * L], qw], axis=0)
            nd = _dot(lhs, rhs)
            inv = 1.0 / jnp.maximum(jnp.abs(nd[M_DV:M_DV + 1, :]), emx[h:h + 1, :])
            ht_s[h * M_DV:(h + 1) * M_DV, ts] = nd[0:M_DV, :] * inv
            upd = _dot((va * w_end[h:h + 1, :]).astype(BF16), k_s[ts, h * LANES:h * LANES + M_DK])
            c_s[h] = decay[h:h + 1, 0:M_DK] * ct + upd

    for h in range(H):
        hs = slice(h * M_DV, (h + 1) * M_DV)
        hh = ht_s[hs, rs_]
        scale = lax.rsqrt(jnp.mean(hh * hh, axis=0, keepdims=True) + HEAD_NORM_EPS)
        gcol = jnp.concatenate([gn[hs, :]] * ((rs_.stop - rs_.start) // LANES), axis=1)
        hn_s[hs, rs_] = (hh * scale * gcol * jax.nn.sigmoid(ot_s[hs, rs_])).astype(BF16)
    mix = _dot_tn(hn_s[:, rs_], wout_ref[...])
    y_ref[rs_, :] = _layer_norm(alpha * x + mix, lg_ref[...], lb_ref[...])


P_SECTION = 256


def _mlstm_pipe_kernel(x_ref, wqt_ref, wk_ref, wvt_ref, wot_ref, wg_ref, bg_ref, gn_ref, wout_ref,
                       lg_ref, lb_ref, y_ref, cout_ref, mout_ref,
                       xb_s, qt_s, k_s, vt_s, ot_s, g_s, ht_s, hn_s, c_s, m_s, *, alpha, tt, n_j):
    j = pl.program_id(1)
    L = P_L
    H = M_HEADS
    hv = H * M_DV
    d = x_ref.shape[1]
    section = P_SECTION
    n_sec = tt // section
    units = tt // L
    rows_of = lambda sec: slice(sec * section, (sec + 1) * section)

    @pl.when(j == 0)
    def _():
        c_s[...] = jnp.zeros_like(c_s)
        m_s[...] = jnp.zeros_like(m_s)

    row_i = lax.broadcasted_iota(jnp.int32, (L, LANES), 0)
    lane_i = lax.broadcasted_iota(jnp.int32, (L, LANES), 1)
    tri = (row_i >= lane_i).astype(F32)
    key8 = lax.broadcasted_iota(jnp.int32, (L, H * L), 0)
    qry8 = lax.broadcasted_iota(jnp.int32, (L, H * L), 1) & (L - 1)
    causal8 = key8 <= qry8
    ones_rows = jnp.ones((P_AUG - M_DV, L), F32)
    gn = gn_ref[...]

    def proj_pieces(sec):
        rs = rows_of(sec)

        def q_and_gates():
            xb_s[rs, :] = x_ref[rs, :].astype(BF16)
            qt_s[:, rs] = _dot_nt(wqt_ref[...], xb_s[rs, :]).astype(BF16)
            g_s[rs, :] = _dot(xb_s[rs, :], wg_ref[...]) + bg_ref[...]

        def k_block(cb):
            k_s[rs, cb:cb + A_COL_BLOCK] = _dot(xb_s[rs, :], wk_ref[:, cb:cb + A_COL_BLOCK]).astype(BF16)

        def vt_block(cb):
            vt_s[cb:cb + A_COL_BLOCK, rs] = _dot_nt(wvt_ref[cb:cb + A_COL_BLOCK, :], xb_s[rs, :])

        def ot_block(cb):
            ot_s[cb:cb + A_COL_BLOCK, rs] = _dot_nt(wot_ref[cb:cb + A_COL_BLOCK, :], xb_s[rs, :])

        blocks = range(0, hv, A_COL_BLOCK)
        return ([q_and_gates] + [functools.partial(k_block, cb) for cb in blocks],
                [functools.partial(f, cb) for f in (vt_block, ot_block) for cb in blocks])

    def post_pieces(sec):
        rs = rows_of(sec)

        def norm_gate(heads):
            for h in heads:
                hs = slice(h * M_DV, (h + 1) * M_DV)
                hh = ht_s[hs, rs]
                scale = lax.rsqrt(jnp.mean(hh * hh, axis=0, keepdims=True) + HEAD_NORM_EPS)
                gcol = jnp.concatenate([gn[hs, :]] * (section // LANES), axis=1)
                hn_s[hs, rs] = (hh * scale * gcol * jax.nn.sigmoid(ot_s[hs, rs])).astype(BF16)

        def out_block(i):
            blk = slice(i * (d // 4), (i + 1) * (d // 4))
            y_ref[rs, blk] = _dot_tn(hn_s[:, rs], wout_ref[:, blk])

        def deep_norm():
            y_ref[rs, :] = _layer_norm(alpha * x_ref[rs, :] + y_ref[rs, :], lg_ref[...], lb_ref[...])

        return ([functools.partial(norm_gate, range(0, H // 2)), functools.partial(norm_gate, range(H // 2, H)),
                 functools.partial(out_block, 0), functools.partial(out_block, 1)],
                [functools.partial(out_block, 2), functools.partial(out_block, 3), deep_norm])

    def pre(u):
        ts = slice(u * L, (u + 1) * L)
        gi = g_s[ts, 0:LANES]
        gf = g_s[ts, LANES:2 * LANES]
        lf = jnp.minimum(gf, 0.0) - jnp.log1p(jnp.exp(-jnp.abs(gf)))
        bc = lax.dot_general(tri, lf, (((1,), (0,)), ((), ())), precision=HIGHEST,
                             preferred_element_type=F32)
        rs = gi - bc
        cm = rs
        k = 1
        while k < L:
            cm = jnp.maximum(cm, jnp.where(row_i >= k, pltpu.roll(cm, k, 0), NEG_INF))
            k *= 2
        packed = jnp.where(lane_i < H, rs, jnp.where(lane_i < 2 * H, cm, bc))
        rows = packed.T[0:3 * H]
        rs_cat = jnp.concatenate([jnp.broadcast_to(rs[:, h:h + 1], (L, L)) for h in range(H)], axis=1)
        e_cat = jnp.where(causal8, rs_cat, NEG_INF)
        s_cat = jnp.concatenate(
            [_dot(k_s[ts, h * LANES:h * LANES + M_DK], qt_s[h * M_DK:(h + 1) * M_DK, ts]) for h in range(H)], axis=1)
        return rows, e_cat, s_cat

    def rec(u, pre_u, fillers):
        ts = slice(u * L, (u + 1) * L)
        rows, e_cat, s_cat = pre_u
        rs_t = rows[0:H]
        cm_t = rows[H:2 * H]
        b_t = rows[2 * H:3 * H]
        m0 = m_s[...]
        a_t = jnp.maximum(m0, cm_t)
        w_inter = jnp.exp(m0 - a_t)
        emx = jnp.exp(-(b_t + a_t))
        m_new = jnp.broadcast_to((b_t + a_t)[:, L - 1:L], (H, L))
        b_last = jnp.broadcast_to(b_t[:, L - 1:L], (H, L))
        w_end = jnp.exp(rs_t + (b_last - m_new))
        decay = jnp.exp(b_last + m0 - m_new)
        m_s[...] = m_new
        a_cat = jnp.concatenate([a_t[h:h + 1, :] for h in range(H)], axis=1)
        p_cat = (s_cat * jnp.exp(e_cat - a_cat)).astype(BF16)
        for h in range(H):
            if fillers:
                fillers.pop(0)()
            va = jnp.concatenate([vt_s[h * M_DV:(h + 1) * M_DV, ts], ones_rows], axis=0)
            ct = c_s[h]
            qw = (qt_s[h * M_DK:(h + 1) * M_DK, ts].astype(F32) * w_inter[h:h + 1, :]).astype(BF16)
            lhs = jnp.concatenate([va.astype(BF16), ct.astype(BF16)], axis=1)
            rhs = jnp.concatenate([p_cat[:, h * L:(h + 1) * L], qw], axis=0)
            nd = _dot(lhs, rhs)
            inv = 1.0 / jnp.maximum(jnp.abs(nd[M_DV:M_DV + 1, :]), emx[h:h + 1, :])
            ht_s[h * M_DV:(h + 1) * M_DV, ts] = nd[0:M_DV, :] * inv
            upd = _dot((va * w_end[h:h + 1, :]).astype(BF16), k_s[ts, h * LANES:h * LANES + M_DK])
            c_s[h] = decay[h:h + 1, 0:M_DK] * ct + upd
        while fillers:
            fillers.pop(0)()

    first, rest = proj_pieces(0)
    for piece in first + rest:
        piece()
    pre_next = pre(0)
    for u in range(units):
        sec, slot = divmod(u, section // L)
        fillers = []
        if sec + 1 < n_sec:
            fillers += proj_pieces(sec + 1)[slot]
        if sec >= 1:
            fillers += post_pieces(sec - 1)[slot]
        pre_cur = pre_next
        if u + 1 < units:
            pre_next = pre(u + 1)
        rec(u, pre_cur, fillers)
    for half in post_pieces(n_sec - 1):
        for piece in half:
            piece()

    @pl.when(j == n_j - 1)
    def _():
        cout_ref[0] = c_s[...]
        mout_ref[0] = m_s[...]


def _mlstm_prompt_layer(x2d, w_in, b_gate, g_norm, w_out, ln_g, ln_b, *, alpha, batch, seq, tt):
    n, d = x2d.shape
    assert n == batch * seq and seq % tt == 0 and tt % P_SECTION == 0 and P_SECTION == 2 * P_L
    assert P_GATE_COPIES * M_HEADS <= LANES and d % 4 == 0
    n_j = seq // tt
    hk = M_HEADS * M_DK
    hv = M_HEADS * M_DV
    wqt = w_in[:, 0:hk].T.astype(BF16)
    wk = jnp.pad((w_in[:, hk:2 * hk] * (M_DK ** -0.5)).reshape(d, M_HEADS, M_DK),
                 ((0, 0), (0, 0), (0, LANES - M_DK))).reshape(d, M_HEADS * LANES).astype(BF16)
    wvt = w_in[:, 2 * hk:2 * hk + hv].T.astype(BF16)
    wot = w_in[:, 2 * hk + hv:2 * hk + 2 * hv].T.astype(BF16)
    rep = lambda a: jnp.pad(jnp.tile(a, (1, P_GATE_COPIES)), ((0, 0), (0, LANES - P_GATE_COPIES * M_HEADS)))
    g0 = 2 * hk + 2 * hv
    wg = jnp.concatenate([rep(w_in[:, g0:g0 + M_HEADS]), rep(w_in[:, g0 + M_HEADS:])], axis=1).astype(BF16)
    bg = jnp.concatenate([rep(b_gate[None, 0:M_HEADS]), rep(b_gate[None, M_HEADS:])], axis=1).astype(F32)
    gn = jnp.broadcast_to(g_norm.astype(F32)[:, None], (hv, LANES))
    wout = w_out.astype(BF16)
    consts = (wqt, wk, wvt, wot, wg, bg, gn, wout, ln_g, ln_b)
    y, c_out, m_out = pl.pallas_call(
        functools.partial(_mlstm_pipe_kernel, alpha=alpha, tt=tt, n_j=n_j),
        grid=(batch, n_j),
        in_specs=[pl.BlockSpec((tt, d), lambda b, j: (b * n_j + j, 0))] + [_const_spec(a.shape) for a in consts],
        out_specs=[
            pl.BlockSpec((tt, d), lambda b, j: (b * n_j + j, 0)),
            pl.BlockSpec((1, M_HEADS, P_AUG, M_DK), lambda b, j: (b, 0, 0, 0)),
            pl.BlockSpec((1, M_HEADS, LANES), lambda b, j: (b, 0, 0)),
        ],
        out_shape=[
            jax.ShapeDtypeStruct((n, d), F32),
            jax.ShapeDtypeStruct((batch, M_HEADS, P_AUG, M_DK), F32),
            jax.ShapeDtypeStruct((batch, M_HEADS, LANES), F32),
        ],
        scratch_shapes=[
            pltpu.VMEM((tt, d), BF16),
            pltpu.VMEM((hk, tt), BF16),
            pltpu.VMEM((tt, M_HEADS * LANES), BF16),
            pltpu.VMEM((hv, tt), F32),
            pltpu.VMEM((hv, tt), F32),
            pltpu.VMEM((tt, 2 * LANES), F32),
            pltpu.VMEM((hv, tt), F32),
            pltpu.VMEM((hv, tt), BF16),
            pltpu.VMEM((M_HEADS, P_AUG, M_DK), F32),
            pltpu.VMEM((M_HEADS, LANES), F32),
        ],
        compiler_params=pltpu.CompilerParams(
            dimension_semantics=("arbitrary", "arbitrary"), vmem_limit_bytes=VMEM_LIMIT_BYTES),
        name="mlstm_prompt",
    )(x2d, *consts)
    c_new = jnp.swapaxes(c_out[:, :, 0:M_DV, :], -1, -2)
    n_new = c_out[:, :, M_DV, :]
    m_new = m_out[:, :, 0]
    return y, c_new, n_new, m_new


PAIRS = GROUP // 2


def _attn_unit(q_rows, k_lo, k_hi, v_lo, v_hi, bias_lo, bias_hi, sink_lo, sink_hi, key_ok):
    outs = []
    for k_op, v_op, bias, sink in ((k_lo, v_lo, bias_lo, sink_lo), (k_hi, v_hi, bias_hi, sink_hi)):
        s = _dot_nt(q_rows, k_op) + bias
        if key_ok is not None:
            s = jnp.where(key_ok, s, NEG_INF)
        mx = jnp.maximum(jnp.max(s, axis=-1, keepdims=True), sink)
        p = jnp.exp(s - mx)
        den = jnp.sum(p, axis=-1, keepdims=True) + jnp.exp(sink - mx)
        outs.append(_dot(p.astype(BF16), v_op) / den)
    return outs[0] + outs[1]


def _split_kv(kv):
    lane = lax.broadcasted_iota(jnp.int32, (1, LANES), 1)
    low = lane < HEAD_DIM
    res = []
    kk = kv[:, 0:LANES]
    vv = kv[:, LANES:2 * LANES]
    kk_r = pltpu.roll(kk, HEAD_DIM, 1)
    vv_r = pltpu.roll(vv, HEAD_DIM, 1)
    z = jnp.zeros_like(kk)
    res.append((jnp.where(low, kk, z), jnp.where(low, z, kk_r), jnp.where(low, vv, z), jnp.where(low, z, vv_r)))
    res.append((jnp.where(low, kk_r, z), jnp.where(low, z, kk), jnp.where(low, vv_r, z), jnp.where(low, z, vv)))
    return [tuple(a.astype(BF16) for a in grp) for grp in res]


SWA_UNIT = 2 * CHUNK
SWA_KEYS = WINDOW + SWA_UNIT
SWA_SECTION = 256


def _swa_prompt_kernel(x_ref, kc_ref, kp_ref, vtc_ref, vtp_ref, wqt_ref, wo_ref, bias_ref, sink_ref,
                       lg_ref, lb_ref, y_ref, xb_s, qt_s, ot_s, k_s, vt_s, *, alpha, tq, section):
    j = pl.program_id(1)
    k_s[0:WINDOW, :] = kp_ref[...].astype(BF16)
    k_s[WINDOW:WINDOW + tq, :] = kc_ref[...].astype(BF16)
    vt_s[:, 0:WINDOW] = vtp_ref[...]
    vt_s[:, WINDOW:WINDOW + tq] = vtc_ref[...]
    zeros = jnp.zeros((HEAD_DIM, GROUP * SWA_UNIT), BF16)
    key_i = lax.broadcasted_iota(jnp.int32, (SWA_KEYS, GROUP * SWA_UNIT), 0)
    d = x_ref.shape[1]
    n_sec = tq // section
    per_sec = (section // SWA_UNIT) * N_KV_HEADS
    rows_of = lambda sec: slice(sec * section, (sec + 1) * section)

    def project(sec, piece):
        rs = rows_of(sec)
        if piece == 0:
            xb_s[rs, :] = x_ref[rs, :].astype(BF16)
        blk = slice(piece * (d // per_sec), (piece + 1) * (d // per_sec))
        qt_s[blk, rs] = _dot_nt(wqt_ref[blk, :], xb_s[rs, :]).astype(BF16)

    def out_project(sec, piece):
        rs = rows_of(sec)
        blk = slice(piece * (d // per_sec), (piece + 1) * (d // per_sec))
        y_ref[rs, blk] = _dot_tn(ot_s[:, rs], wo_ref[:, blk])

    def normalise(sec):
        rs = rows_of(sec)
        y_ref[rs, :] = _layer_norm(alpha * x_ref[rs, :] + y_ref[rs, :], lg_ref[...], lb_ref[...])

    def scores(u, g):
        r0 = u * SWA_UNIT
        qt_g = jnp.concatenate(
            [qt_s[h * HEAD_DIM:(h + 1) * HEAD_DIM, r0:r0 + SWA_UNIT] for h in range(g * GROUP, (g + 1) * GROUP)],
            axis=1)
        qz = jnp.concatenate([qt_g, zeros] if g == 0 else [zeros, qt_g], axis=0)
        s_t = _dot(k_s[r0:r0 + SWA_KEYS, :], qz) + bias_ref[g]
        if r0 < WINDOW:
            s_t = jnp.where(j * tq + r0 - WINDOW + key_i >= 0, s_t, NEG_INF)
        return s_t

    def finish(u, g, s_t):
        r0 = u * SWA_UNIT
        sink = sink_ref[g]
        mx = jnp.maximum(jnp.max(s_t, axis=0, keepdims=True), sink)
        p = jnp.exp2(s_t - mx)
        den = jnp.sum(p, axis=0, keepdims=True) + jnp.exp2(sink - mx)
        vt_g = vt_s[g * HEAD_DIM:(g + 1) * HEAD_DIM, r0:r0 + SWA_KEYS]
        o_t = (_dot(vt_g, p.astype(BF16)) / den).astype(BF16)
        for i in range(GROUP):
            h = g * GROUP + i
            ot_s[h * HEAD_DIM:(h + 1) * HEAD_DIM, r0:r0 + SWA_UNIT] = o_t[:, i * SWA_UNIT:(i + 1) * SWA_UNIT]

    todo = [(u, g) for u in range(tq // SWA_UNIT) for g in range(N_KV_HEADS)]
    for piece in range(per_sec):
        project(0, piece)
    s_next = scores(*todo[0])
    for idx, (u, g) in enumerate(todo):
        sec, slot = divmod(idx, per_sec)
        if sec + 1 < n_sec:
            project(sec + 1, slot)
        if sec >= 1:
            out_project(sec - 1, slot)
        s_cur = s_next
        if idx + 1 < len(todo):
            s_next = scores(*todo[idx + 1])
        finish(u, g, s_cur)
        if sec >= 1 and slot == per_sec - 1:
            normalise(sec - 1)
    for piece in range(per_sec):
        out_project(n_sec - 1, piece)
    normalise(n_sec - 1)


def _swa_prompt_layer(x2d, kv2d, vt, wqt, wo, bias, sinks, ln_g, ln_b, *, alpha, batch, seq, tq):
    n, d = x2d.shape
    section = SWA_SECTION
    assert n == batch * seq and seq % tq == 0 and tq % section == 0 and section % SWA_UNIT == 0 and WINDOW == LANES
    assert d % ((section // SWA_UNIT) * N_KV_HEADS) == 0
    n_j = seq // tq
    per = tq // WINDOW
    prev = lambda b, j: jnp.maximum((b * n_j + j) * per - 1, 0)
    return pl.pallas_call(
        functools.partial(_swa_prompt_kernel, alpha=alpha, tq=tq, section=section),
        grid=(batch, n_j),
        in_specs=[
            pl.BlockSpec((tq, d), lambda b, j: (b * n_j + j, 0)),
            pl.BlockSpec((tq, LANES), lambda b, j: (b * n_j + j, 0)),
            pl.BlockSpec((WINDOW, LANES), lambda b, j: (prev(b, j), 0)),
            pl.BlockSpec((LANES, tq), lambda b, j: (0, b * n_j + j)),
            pl.BlockSpec((LANES, WINDOW), lambda b, j: (0, prev(b, j))),
            _const_spec(wqt.shape),
            _const_spec(wo.shape),
            _const_spec(bias.shape),
            _const_spec(sinks.shape),
            _const_spec(ln_g.shape),
            _const_spec(ln_b.shape),
        ],
        out_specs=pl.BlockSpec((tq, d), lambda b, j: (b * n_j + j, 0)),
        out_shape=jax.ShapeDtypeStruct((n, d), F32),
        scratch_shapes=[
            pltpu.VMEM((tq, d), BF16),
            pltpu.VMEM((d, tq), BF16),
            pltpu.VMEM((d, tq), BF16),
            pltpu.VMEM((WINDOW + tq, LANES), BF16),
            pltpu.VMEM((LANES, WINDOW + tq), BF16),
        ],
        compiler_params=pltpu.CompilerParams(
            dimension_semantics=("arbitrary", "arbitrary"), vmem_limit_bytes=VMEM_LIMIT_BYTES),
        name="swa_prompt",
    )(x2d, kv2d, kv2d, vt, vt, wqt, wo, bias, sinks, ln_g, ln_b)


def _swa_prompt_tables(sinks):
    slopes = jnp.exp2(-8.0 * jnp.arange(1, N_Q_HEADS + 1, dtype=F32) / N_Q_HEADS)
    q = jnp.arange(SWA_UNIT)[None, :]
    kx = jnp.arange(SWA_KEYS)[:, None]
    dist = jnp.abs(q + WINDOW - kx).astype(F32)
    first = q < CHUNK
    visible = (first & (kx < WINDOW + CHUNK)) | (~first & (kx >= CHUNK))
    bias = jnp.where(visible[None], -(LOG2_E * slopes)[:, None, None] * dist[None], NEG_INF)
    sinks = sinks.astype(F32) * LOG2_E
    bias = bias.reshape(N_KV_HEADS, GROUP, SWA_KEYS, SWA_UNIT).transpose(0, 2, 1, 3)
    sink = jnp.broadcast_to(sinks.astype(F32).reshape(N_KV_HEADS, 1, GROUP, 1), (N_KV_HEADS, 1, GROUP, SWA_UNIT))
    return (bias.reshape(N_KV_HEADS, SWA_KEYS, GROUP * SWA_UNIT), sink.reshape(N_KV_HEADS, 1, GROUP * SWA_UNIT))


def _swa_sample_kernel(x_ref, kvn_ref, kc_ref, vc_ref, wq_ref, wo_ref, bias_ref, sink_ref, lg_ref, lb_ref,
                       y_ref, q_s, o_s, *, alpha, batch, seq):
    x = x_ref[...]
    xb = x.astype(BF16)
    q_s[...] = _dot(xb, wq_ref[...]).astype(BF16)
    for b in range(batch):
        r0 = b * seq
        kv_new = kvn_ref[r0:r0 + seq, :]
        kv_old = jnp.concatenate([kc_ref[b], vc_ref[b]], axis=1)
        groups = _split_kv(jnp.concatenate([kv_old, kv_new], axis=0))
        for g in range(N_KV_HEADS):
            q_rows = jnp.concatenate(
                [q_s[r0:r0 + seq, (g * PAIRS + pp) * LANES:(g * PAIRS + pp + 1) * LANES] for pp in range(PAIRS)],
                axis=0)
            ks = groups[g]
            o = _attn_unit(q_rows, ks[0], ks[1], ks[2], ks[3], bias_ref[g, 0], bias_ref[g, 1],
                           sink_ref[g, 0], sink_ref[g, 1], None)
            for pp in range(PAIRS):
                o_s[r0:r0 + seq, (g * PAIRS + pp) * LANES:(g * PAIRS + pp + 1) * LANES] = (
                    o[pp * seq:(pp + 1) * seq, :].astype(BF16))
    mix = _dot(o_s[...], wo_ref[...])
    y_ref[...] = _layer_norm(alpha * x + mix, lg_ref[...], lb_ref[...])


def _swa_sample_layer(x2d, kv_new, k_cache, v_cache, wq, wo, bias, sink, ln_g, ln_b, *, alpha, batch, seq):
    n, d = x2d.shape
    assert n == batch * seq
    args = (x2d, kv_new, k_cache, v_cache, wq, wo, bias, sink, ln_g, ln_b)
    return pl.pallas_call(
        functools.partial(_swa_sample_kernel, alpha=alpha, batch=batch, seq=seq),
        grid=(1,),
        in_specs=[_const_spec(a.shape) for a in args],
        out_specs=pl.BlockSpec((n, d), lambda i: (0, 0)),
        out_shape=jax.ShapeDtypeStruct((n, d), F32),
        scratch_shapes=[pltpu.VMEM((n, d), BF16), pltpu.VMEM((n, d), BF16)],
        compiler_params=pltpu.CompilerParams(
            dimension_semantics=("arbitrary",), vmem_limit_bytes=VMEM_LIMIT_BYTES),
        name="swa_sample",
    )(*args)


def _attn_tables(sinks, ql, nk):
    slopes = jnp.exp2(-8.0 * jnp.arange(1, N_Q_HEADS + 1, dtype=F32) / N_Q_HEADS)
    dist = jnp.abs(jnp.arange(ql)[:, None] - jnp.arange(nk)[None, :] + (nk - ql)).astype(F32)
    head = (jnp.arange(N_KV_HEADS)[:, None, None] * GROUP + 2 * jnp.arange(PAIRS)[None, None, :]
            + jnp.arange(2)[None, :, None])
    bias = -slopes[head][..., None, None] * dist
    sink = jnp.broadcast_to(sinks.astype(F32)[head][..., None, None], head.shape + (ql, 1))
    return (bias.reshape(N_KV_HEADS, 2, PAIRS * ql, nk), sink.reshape(N_KV_HEADS, 2, PAIRS * ql, 1))


def _trunk(x, c0s, n0s, m0s, k_cache, v_cache, params, *, is_prompt, kv_rows):
    (w_in_a, b_gate_a, g_norm_a, w_out_a, w_kv, w_q_b, sinks_b, w_out_b, w_gu, w_down, ln_g, ln_b) = params
    batch, seq, d = x.shape
    depth = w_gu.shape[0]
    n_a = w_in_a.shape[0]
    alpha = (2 * depth) ** 0.25
    d_ff = w_down.shape[1]
    n = batch * seq
    x2d = x.reshape(n, d)
    tm = min(512, n)
    if is_prompt:
        L, bb_n, tt = 128, 1, min(1024, seq)
    else:
        L, bb_n, tt = seq, 8, seq
    cs, ns, ms = [], [], []
    kv2d = None
    for layer in range(depth):
        row = lambda a: a.astype(F32)[None, :]
        if layer < n_a and is_prompt:
            x2d, c, nn, m = _mlstm_prompt_layer(
                x2d, w_in_a[layer], b_gate_a[layer], g_norm_a[layer], w_out_a[layer],
                row(ln_g[layer, 0]), row(ln_b[layer, 0]), alpha=alpha, batch=batch, seq=seq, tt=tt)
            cs.append(c)
            ns.append(nn)
            ms.append(m)
        elif layer < n_a:
            w_all, bg, gn, wo = _prep_mlstm_weights(w_in_a[layer], b_gate_a[layer], g_norm_a[layer], w_out_a[layer])
            m0 = jnp.broadcast_to(m0s[layer].astype(F32)[:, :, None, None], (batch, M_HEADS, 8, LANES))
            x2d, c, nn, m = _mlstm_layer(
                x2d, c0s[layer].astype(F32), n0s[layer].astype(F32)[..., None], m0, w_all, bg, gn, wo,
                row(ln_g[layer, 0]), row(ln_b[layer, 0]), alpha=alpha, batch=batch, seq=seq, L=L, bb_n=bb_n, tt=tt)
            cs.append(c)
            ns.append(nn)
            ms.append(m)
        else:
            jb = layer - n_a
            if kv2d is None:
                kv2d, vt = _kv_proj(x2d, w_kv, tm=tm, with_vt=is_prompt)
            wq = (w_q_b[jb] * (HEAD_DIM ** -0.5)).astype(BF16)
            wo = w_out_b[jb].astype(BF16)
            if is_prompt:
                bias, sink = _swa_prompt_tables(sinks_b[jb])
                wqt = (w_q_b[jb] * (HEAD_DIM ** -0.5 * LOG2_E)).T.astype(BF16)
                x2d = _swa_prompt_layer(x2d, kv2d, vt, wqt, wo, bias, sink,
                                        row(ln_g[layer, 0]), row(ln_b[layer, 0]),
                                        alpha=alpha, batch=batch, seq=seq, tq=min(1024, seq))
            else:
                w_rows = k_cache.shape[1]
                bias, sink = _attn_tables(sinks_b[jb], seq, w_rows + seq)
                x2d = _swa_sample_layer(
                    x2d, kv2d, k_cache.astype(F32).reshape(batch, w_rows, N_KV_HEADS * HEAD_DIM),
                    v_cache.astype(F32).reshape(batch, w_rows, N_KV_HEADS * HEAD_DIM), wq, wo, bias, sink,
                    row(ln_g[layer, 0]), row(ln_b[layer, 0]), alpha=alpha, batch=batch, seq=seq)
        x2d = _ffn(x2d, w_gu, w_down, row(ln_g[layer, 1]), row(ln_b[layer, 1]), layer=layer, alpha=alpha, tm=tm)
    kv = kv2d.reshape(batch, seq, kv2d.shape[1])[:, seq - kv_rows:].reshape(batch, kv_rows, 2, N_KV_HEADS, HEAD_DIM)
    return (x2d.reshape(batch, seq, d), jnp.stack(cs), jnp.stack(ns), jnp.stack(ms), kv[:, :, 0], kv[:, :, 1])


def kernel(x_prompt, x_sample, state_C, state_n, state_m, cache_k, cache_v, w_in_a, b_gate_a, g_norm_a,
           w_out_a, w_kv, w_q_b, sinks_b, w_out_b, w_gu, w_down, ln_g, ln_b):
    params = (w_in_a, b_gate_a, g_norm_a, w_out_a, w_kv, w_q_b, sinks_b, w_out_b,
              w_gu.astype(BF16), w_down.astype(BF16), ln_g, ln_b)
    y_p, p_c, p_n, p_m, p_k, p_v = _trunk(x_prompt, None, None, None, None, None, params, is_prompt=True,
                                          kv_rows=min(WINDOW, x_prompt.shape[1]))
    y_s, s_c, s_n, s_m, s_k, s_v = _trunk(x_sample, state_C, state_n, state_m, cache_k, cache_v, params,
                                          is_prompt=False, kv_rows=x_sample.shape[1])
    return (y_p, y_s, p_c, p_n, p_m, p_k, p_v, s_c, s_n, s_m, s_k, s_v)
```

```python
import functools

import jax
import jax.numpy as jnp
from jax import lax
from jax.experimental import pallas as pl
from jax.experimental.pallas import tpu as pltpu

F32 = jnp.float32
BF16 = jnp.bfloat16

CHUNK = 64
WINDOW = 128
M_HEADS = 8
M_DK = 64
M_DV = 128
N_Q_HEADS = 16
N_KV_HEADS = 2
HEAD_DIM = 64
GROUP = N_Q_HEADS // N_KV_HEADS
LN_EPS = 1e-5
HEAD_NORM_EPS = 1e-6

LANES = 128
LOG2_E = 1.4426950408889634
VMEM_LIMIT_BYTES = 56 * 1024 * 1024

NEG_INF = float("-inf")
HIGHEST = lax.Precision.HIGHEST


def _const_spec(shape):
    nd = len(shape)
    return pl.BlockSpec(shape, lambda *_: (0,) * nd, pipeline_mode=pl.Buffered(1))


def _layer_norm(y, g, b):
    mu = jnp.mean(y, axis=-1, keepdims=True)
    yc = y - mu
    var = jnp.mean(yc * yc, axis=-1, keepdims=True)
    return yc * lax.rsqrt(var + LN_EPS) * g + b


def _dot(a, b):
    return jnp.dot(a, b, preferred_element_type=F32)


def _dot_nt(a, b):
    return lax.dot_general(a, b, (((1,), (1,)), ((), ())), preferred_element_type=F32)


def _dot_tn(a, b):
    return lax.dot_general(a, b, (((0,), (0,)), ((), ())), preferred_element_type=F32)


FF_BLOCK = 256


def _ffn_kernel(*refs, alpha, d_ff, sub, with_kv, with_vt):
    x_ref, wgu_ref, wd_ref, g_ref, b_ref = refs[:5]
    rest = list(refs[5:])
    wkv_ref = rest.pop(0) if with_kv else None
    wvt_ref = rest.pop(0) if with_vt else None
    o_ref = rest.pop(0)
    kv_ref = rest.pop(0) if with_kv else None
    vt_ref = rest.pop(0) if with_vt else None
    acc_ref = rest.pop(0)
    tm = x_ref.shape[0]

    def finish(i):
        rs = slice(i * sub, (i + 1) * sub)
        y = _layer_norm(alpha * x_ref[rs, :] + acc_ref[rs, :], g_ref[...], b_ref[...])
        o_ref[rs, :] = y
        if with_kv:
            yb = y.astype(BF16)
            kv_ref[rs, :] = _dot(yb, wkv_ref[...])
            if with_vt:
                vt_ref[:, rs] = _dot_nt(wvt_ref[...], yb).astype(BF16)

    for i in range(tm // sub):
        rs = slice(i * sub, (i + 1) * sub)
        xb = x_ref[rs, :].astype(BF16)
        for j in range(d_ff // FF_BLOCK):
            sl = slice(j * FF_BLOCK, (j + 1) * FF_BLOCK)
            su = slice(d_ff + j * FF_BLOCK, d_ff + (j + 1) * FF_BLOCK)
            g = _dot(xb, wgu_ref[:, sl])
            u = _dot(xb, wgu_ref[:, su])
            h = (g * jax.nn.sigmoid(g) * u).astype(BF16)
            part = _dot(h, wd_ref[sl, :])
            if j == 0:
                acc_ref[rs, :] = part
            else:
                acc_ref[rs, :] += part
            if j == 0 and i > 0:
                finish(i - 1)
    finish(tm // sub - 1)


def _layer_spec(shape, layer):
    nd = len(shape) - 1
    return pl.BlockSpec((None,) + tuple(shape[1:]), lambda *_: (layer,) + (0,) * nd, pipeline_mode=pl.Buffered(1))


def _ffn(x2d, w_gu, w_down, ln_g, ln_b, *, layer, alpha, tm, sub, w_kv=None, with_vt=False):
    n, d = x2d.shape
    d_ff = w_down.shape[1]
    assert n % tm == 0 and tm % sub == 0 and d_ff % FF_BLOCK == 0 and w_gu.shape[2] == 2 * d_ff
    with_kv = w_kv is not None
    assert with_kv or not with_vt
    args = [x2d, w_gu, w_down, ln_g, ln_b]
    in_specs = [pl.BlockSpec((tm, d), lambda i: (i, 0)), _layer_spec(w_gu.shape, layer),
                _layer_spec(w_down.shape, layer), _const_spec(ln_g.shape), _const_spec(ln_b.shape)]
    out_specs = [pl.BlockSpec((tm, d), lambda i: (i, 0))]
    out_shape = [jax.ShapeDtypeStruct((n, d), F32)]
    if with_kv:
        nk = w_kv.shape[1]
        args.append(w_kv.astype(BF16))
        in_specs.append(_const_spec(w_kv.shape))
        out_specs.append(pl.BlockSpec((tm, nk), lambda i: (i, 0)))
        out_shape.append(jax.ShapeDtypeStruct((n, nk), F32))
        if with_vt:
            w_vt = w_kv[:, nk // 2:].T.astype(BF16)
            args.append(w_vt)
            in_specs.insert(len(in_specs), _const_spec(w_vt.shape))
            out_specs.append(pl.BlockSpec((nk // 2, tm), lambda i: (0, i)))
            out_shape.append(jax.ShapeDtypeStruct((nk // 2, n), BF16))
    outs = pl.pallas_call(
        functools.partial(_ffn_kernel, alpha=alpha, d_ff=d_ff, sub=sub, with_kv=with_kv, with_vt=with_vt),
        grid=(n // tm,),
        in_specs=in_specs,
        out_specs=out_specs,
        out_shape=out_shape,
        scratch_shapes=[pltpu.VMEM((tm, d), F32)],
        compiler_params=pltpu.CompilerParams(
            dimension_semantics=("arbitrary",), vmem_limit_bytes=VMEM_LIMIT_BYTES),
        name="ffn_kv" if with_kv else "ffn",
    )(*args)
    return outs[0], (outs[1] if with_kv else None), (outs[2] if with_vt else None)


def _kv_kernel(x_ref, w_ref, o_ref):
    o_ref[...] = _dot(x_ref[...].astype(BF16), w_ref[...])


def _kv_t_kernel(x_ref, w_ref, wvt_ref, o_ref, vt_ref):
    xb = x_ref[...].astype(BF16)
    o_ref[...] = _dot(xb, w_ref[...])
    vt_ref[...] = _dot_nt(wvt_ref[...], xb).astype(BF16)


def _kv_proj(x2d, w_kv, *, tm, with_vt):
    n, d = x2d.shape
    nk = w_kv.shape[1]
    assert n % tm == 0
    params = pltpu.CompilerParams(dimension_semantics=("arbitrary",), vmem_limit_bytes=VMEM_LIMIT_BYTES)
    x_spec = pl.BlockSpec((tm, d), lambda i: (i, 0))
    kv_spec = pl.BlockSpec((tm, nk), lambda i: (i, 0))
    w_kv_b = w_kv.astype(BF16)
    if not with_vt:
        kv = pl.pallas_call(
            _kv_kernel, grid=(n // tm,), in_specs=[x_spec, _const_spec(w_kv.shape)], out_specs=kv_spec,
            out_shape=jax.ShapeDtypeStruct((n, nk), F32), compiler_params=params, name="kv_proj",
        )(x2d, w_kv_b)
        return kv, None
    nv = nk // 2
    w_vt = w_kv[:, nv:].T.astype(BF16)
    return pl.pallas_call(
        _kv_t_kernel,
        grid=(n // tm,),
        in_specs=[x_spec, _const_spec(w_kv.shape), _const_spec(w_vt.shape)],
        out_specs=[kv_spec, pl.BlockSpec((nv, tm), lambda i: (0, i))],
        out_shape=[jax.ShapeDtypeStruct((n, nk), F32), jax.ShapeDtypeStruct((nv, n), BF16)],
        compiler_params=params,
        name="kv_proj_t",
    )(x2d, w_kv_b, w_vt)


A_QOFF = 0
A_KOFF = M_HEADS * LANES
A_VOFF = 2 * M_HEADS * LANES
A_OOFF = 3 * M_HEADS * LANES
A_GOFF = 4 * M_HEADS * LANES
A_COLS = A_GOFF + 2 * LANES
A_COL_BLOCK = 512


def _mlstm_kernel(x_ref, c0_ref, n0_ref, m0_ref, w_ref, bg_ref, gn_ref, wo_ref, lg_ref, lb_ref,
                  y_ref, cout_ref, mout_ref,
                  q_s, k_s, v_s, o_s, g_s, h_s, hb_s, c_s, m_s,
                  *, alpha, L, bb_n, tt, n_j):
    j = pl.program_id(1)
    rows = bb_n * tt
    units_per_b = tt // L
    hd = M_HEADS * LANES

    @pl.when(j == 0)
    def _():
        lane = lax.broadcasted_iota(jnp.int32, (1, 1, 1, LANES), 3)
        c_s[:, :, :, 0:LANES] = c0_ref[...]
        c_s[:, :, :, LANES:2 * LANES] = jnp.where(lane == 0, n0_ref[...], 0.0)
        m_s[...] = m0_ref[...]

    x = x_ref[...]
    xb = x.astype(BF16)
    for cb in range(0, hd, A_COL_BLOCK):
        q_s[:, cb:cb + A_COL_BLOCK] = _dot(xb, w_ref[:, A_QOFF + cb:A_QOFF + cb + A_COL_BLOCK]).astype(BF16)
        k_s[:, cb:cb + A_COL_BLOCK] = _dot(xb, w_ref[:, A_KOFF + cb:A_KOFF + cb + A_COL_BLOCK])
        v_s[:, cb:cb + A_COL_BLOCK] = _dot(xb, w_ref[:, A_VOFF + cb:A_VOFF + cb + A_COL_BLOCK]).astype(BF16)
        o_s[:, cb:cb + A_COL_BLOCK] = _dot(xb, w_ref[:, A_OOFF + cb:A_OOFF + cb + A_COL_BLOCK])
    g_s[...] = _dot(xb, w_ref[:, A_GOFF:A_GOFF + 2 * LANES]) + bg_ref[...]

    row_i = lax.broadcasted_iota(jnp.int32, (L, L), 0)
    col_i = lax.broadcasted_iota(jnp.int32, (L, L), 1)
    causal = row_i >= col_i
    tri = causal.astype(F32)
    ones_col = jnp.where(lax.broadcasted_iota(jnp.int32, (L, LANES), 1) == 0, 1.0, 0.0).astype(BF16)

    def unit(u, carry):
        r0 = pl.multiple_of(u * L, L)
        bb = u // units_per_b if bb_n > 1 else 0
        gi = g_s[pl.ds(r0, L), 0:LANES]
        gf = g_s[pl.ds(r0, L), LANES:2 * LANES]
        lf = jnp.minimum(gf, 0.0) - jnp.log1p(jnp.exp(-jnp.abs(gf)))
        bc = lax.dot_general(tri, lf, (((1,), (0,)), ((), ())), precision=HIGHEST,
                             preferred_element_type=F32)
        rs = gi - bc
        rs_t = rs.T
        b_last = bc[L - 1:L, :]
        for h in range(M_HEADS):
            hs = slice(h * LANES, (h + 1) * LANES)
            qh = q_s[pl.ds(r0, L), hs]
            kh = k_s[pl.ds(r0, L), hs]
            vh = v_s[pl.ds(r0, L), hs]
            m0 = m_s[bb, h, 0:1, 0:1]
            c_aug = c_s[bb, h]
            b_col = bc[:, h:h + 1]
            logd = jnp.where(causal, b_col + rs_t[h:h + 1, :], NEG_INF)
            inter = b_col + m0
            mx = jnp.maximum(inter, jnp.max(logd, axis=1, keepdims=True))
            d = jnp.exp(logd - mx)
            w_inter = jnp.exp(inter - mx)
            s = _dot_nt(qh, kh.astype(BF16))
            p = (s * d).astype(BF16)
            v_aug = jnp.concatenate([vh, ones_col], axis=1)
            numden = w_inter * _dot(qh[:, 0:M_DK], c_aug.astype(BF16)) + _dot(p, v_aug)
            num = numden[:, 0:M_DV]
            den = numden[:, M_DV:M_DV + 1]
            h_s[pl.ds(r0, L), hs] = num / jnp.maximum(jnp.abs(den), jnp.exp(-mx))
            m_new = mx[L - 1:L, :]
            bl = b_last[:, h:h + 1]
            w_end = jnp.exp(rs[:, h:h + 1] + (bl - m_new))
            decay = jnp.exp(bl + m0 - m_new)
            wk = (kh[:, 0:M_DK] * w_end).astype(BF16)
            c_s[bb, h] = decay * c_aug + _dot_tn(wk, v_aug)
            m_s[bb, h] = jnp.broadcast_to(m_new, (8, LANES))
        return carry

    lax.fori_loop(0, rows // L, unit, 0)

    for h in range(M_HEADS):
        hs = slice(h * LANES, (h + 1) * LANES)
        hh = h_s[:, hs]
        hn = hh * lax.rsqrt(jnp.mean(hh * hh, axis=-1, keepdims=True) + HEAD_NORM_EPS)
        hb_s[:, hs] = (hn * gn_ref[:, hs] * jax.nn.sigmoid(o_s[:, hs])).astype(BF16)
    mix = _dot(hb_s[...], wo_ref[...])
    y_ref[...] = _layer_norm(alpha * x + mix, lg_ref[...], lb_ref[...])

    @pl.when(j == n_j - 1)
    def _():
        cout_ref[...] = c_s[...]
        mout_ref[...] = m_s[...]


def _mlstm_layer(x2d, c0, n0, m0, w_all, b_gate, g_norm, w_out, ln_g, ln_b, *, alpha, batch, seq, L, bb_n, tt):
    n, d = x2d.shape
    assert n == batch * seq and seq % tt == 0 and tt % L == 0 and batch % bb_n == 0
    assert bb_n == 1 or tt == seq
    n_j = seq // tt
    rows = bb_n * tt
    hd = M_HEADS * LANES
    kern = functools.partial(_mlstm_kernel, alpha=alpha, L=L, bb_n=bb_n, tt=tt, n_j=n_j)
    st4 = lambda bi, j: (bi, 0, 0, 0)
    y, c_out, m_out = pl.pallas_call(
        kern,
        grid=(batch // bb_n, n_j),
        in_specs=[
            pl.BlockSpec((rows, d), lambda bi, j: (bi * n_j + j, 0)),
            pl.BlockSpec((bb_n, M_HEADS, M_DK, M_DV), st4),
            pl.BlockSpec((bb_n, M_HEADS, M_DK, 1), st4),
            pl.BlockSpec((bb_n, M_HEADS, 8, LANES), st4),
            _const_spec(w_all.shape),
            _const_spec(b_gate.shape),
            _const_spec(g_norm.shape),
            _const_spec(w_out.shape),
            _const_spec(ln_g.shape),
            _const_spec(ln_b.shape),
        ],
        out_specs=[
            pl.BlockSpec((rows, d), lambda bi, j: (bi * n_j + j, 0)),
            pl.BlockSpec((bb_n, M_HEADS, M_DK, 2 * LANES), st4),
            pl.BlockSpec((bb_n, M_HEADS, 8, LANES), st4),
        ],
        out_shape=[
            jax.ShapeDtypeStruct((n, d), F32),
            jax.ShapeDtypeStruct((batch, M_HEADS, M_DK, 2 * LANES), F32),
            jax.ShapeDtypeStruct((batch, M_HEADS, 8, LANES), F32),
        ],
        scratch_shapes=[
            pltpu.VMEM((rows, hd), BF16),
            pltpu.VMEM((rows, hd), F32),
            pltpu.VMEM((rows, hd), BF16),
            pltpu.VMEM((rows, hd), F32),
            pltpu.VMEM((rows, 2 * LANES), F32),
            pltpu.VMEM((rows, hd), F32),
            pltpu.VMEM((rows, hd), BF16),
            pltpu.VMEM((bb_n, M_HEADS, M_DK, 2 * LANES), F32),
            pltpu.VMEM((bb_n, M_HEADS, 8, LANES), F32),
        ],
        compiler_params=pltpu.CompilerParams(
            dimension_semantics=("arbitrary", "arbitrary"), vmem_limit_bytes=VMEM_LIMIT_BYTES),
        name="mlstm_layer",
    )(x2d, c0, n0, m0, w_all, b_gate, g_norm, w_out, ln_g, ln_b)
    c_new = c_out[..., 0:M_DV]
    n_new = c_out[..., M_DV]
    m_new = m_out[:, :, 0, 0]
    return y, c_new, n_new, m_new


def _prep_mlstm_weights(w_in, b_gate, g_norm, w_out):
    d = w_in.shape[0]
    hk = M_HEADS * M_DK
    hv = M_HEADS * M_DV
    wq = w_in[:, 0:hk].reshape(d, M_HEADS, M_DK)
    wk = w_in[:, hk:2 * hk].reshape(d, M_HEADS, M_DK) * (M_DK ** -0.5)
    pad = ((0, 0), (0, 0), (0, LANES - M_DK))
    wq = jnp.pad(wq, pad).reshape(d, M_HEADS * LANES)
    wk = jnp.pad(wk, pad).reshape(d, M_HEADS * LANES)
    wv = w_in[:, 2 * hk:2 * hk + hv]
    wo = w_in[:, 2 * hk + hv:2 * hk + 2 * hv]
    wi = jnp.pad(w_in[:, 2 * hk + 2 * hv:2 * hk + 2 * hv + M_HEADS], ((0, 0), (0, LANES - M_HEADS)))
    wf = jnp.pad(w_in[:, 2 * hk + 2 * hv + M_HEADS:], ((0, 0), (0, LANES - M_HEADS)))
    w_all = jnp.concatenate([wq, wk, wv, wo, wi, wf], axis=1).astype(BF16)
    bg = jnp.concatenate([jnp.pad(b_gate[0:M_HEADS], (0, LANES - M_HEADS)),
                          jnp.pad(b_gate[M_HEADS:], (0, LANES - M_HEADS))]).astype(F32)[None, :]
    return w_all, bg, g_norm.astype(F32)[None, :], w_out.astype(BF16)


P_L = LANES
P_AUG = M_DV + 16
P_GATE_COPIES = 3


P_SECTION = 256


def _mlstm_pipe_kernel(x_ref, wqt_ref, wk_ref, wvt_ref, wot_ref, wg_ref, bg_ref, gn_ref, wout_ref,
                       lg_ref, lb_ref, y_ref, cout_ref, mout_ref,
                       xb_s, qt_s, k_s, vt_s, ot_s, g_s, ht_s, hn_s, c_s, m_s, *, alpha, tt, n_j):
    j = pl.program_id(1)
    L = P_L
    H = M_HEADS
    hv = H * M_DV
    d = x_ref.shape[1]
    section = P_SECTION
    n_sec = tt // section
    units = tt // L
    rows_of = lambda sec: slice(sec * section, (sec + 1) * section)

    @pl.when(j == 0)
    def _():
        c_s[...] = jnp.zeros_like(c_s)
        m_s[...] = jnp.zeros_like(m_s)

    row_i = lax.broadcasted_iota(jnp.int32, (L, LANES), 0)
    lane_i = lax.broadcasted_iota(jnp.int32, (L, LANES), 1)
    tri = (row_i >= lane_i).astype(F32)
    key8 = lax.broadcasted_iota(jnp.int32, (L, H * L), 0)
    qry8 = lax.broadcasted_iota(jnp.int32, (L, H * L), 1) & (L - 1)
    causal8 = key8 <= qry8
    ones_rows = jnp.ones((P_AUG - M_DV, L), F32)
    gn = gn_ref[...]

    zeros_dk = jnp.zeros((M_DK, L), BF16)

    def k_pair(h, ts):
        return k_s[ts, (h // 2) * LANES:(h // 2 + 1) * LANES]

    def pad_head(h, a):
        return jnp.concatenate([a, zeros_dk] if h % 2 == 0 else [zeros_dk, a], axis=0)

    def proj_pieces(sec):
        rs = rows_of(sec)

        def q_and_gates():
            xb_s[rs, :] = x_ref[rs, :].astype(BF16)
            qt_s[:, rs] = _dot_nt(wqt_ref[...], xb_s[rs, :]).astype(BF16)
            g_s[rs, :] = _dot(xb_s[rs, :], wg_ref[...]) + bg_ref[...]

        def keys():
            k_s[rs, :] = _dot(xb_s[rs, :], wk_ref[...]).astype(BF16)

        def vt_block(cb):
            vt_s[cb:cb + A_COL_BLOCK, rs] = _dot_nt(wvt_ref[cb:cb + A_COL_BLOCK, :], xb_s[rs, :])

        def ot_block(cb):
            ot_s[cb:cb + A_COL_BLOCK, rs] = _dot_nt(wot_ref[cb:cb + A_COL_BLOCK, :], xb_s[rs, :])

        blocks = range(0, hv, A_COL_BLOCK)
        return ([q_and_gates, keys],
                [functools.partial(f, cb) for f in (vt_block, ot_block) for cb in blocks])

    def post_pieces(sec):
        rs = rows_of(sec)

        def norm_gate(heads):
            for h in heads:
                hs = slice(h * M_DV, (h + 1) * M_DV)
                hh = ht_s[hs, rs]
                scale = lax.rsqrt(jnp.mean(hh * hh, axis=0, keepdims=True) + HEAD_NORM_EPS)
                gcol = jnp.concatenate([gn[hs, :]] * (section // LANES), axis=1)
                hn_s[hs, rs] = (hh * scale * gcol * jax.nn.sigmoid(ot_s[hs, rs])).astype(BF16)

        def out_block(i):
            blk = slice(i * (d // 4), (i + 1) * (d // 4))
            y_ref[rs, blk] = _dot_tn(hn_s[:, rs], wout_ref[:, blk])

        def deep_norm():
            y_ref[rs, :] = _layer_norm(alpha * x_ref[rs, :] + y_ref[rs, :], lg_ref[...], lb_ref[...])

        return ([functools.partial(norm_gate, range(0, H // 2)), functools.partial(norm_gate, range(H // 2, H)),
                 functools.partial(out_block, 0), functools.partial(out_block, 1)],
                [functools.partial(out_block, 2), functools.partial(out_block, 3), deep_norm])

    def pre(u):
        ts = slice(u * L, (u + 1) * L)
        gi = g_s[ts, 0:LANES]
        gf = g_s[ts, LANES:2 * LANES]
        lf = jnp.minimum(gf, 0.0) - jnp.log1p(jnp.exp(-jnp.abs(gf)))
        bc = lax.dot_general(tri, lf, (((1,), (0,)), ((), ())), precision=HIGHEST,
                             preferred_element_type=F32)
        rs = gi - bc
        cm = rs
        k = 1
        while k < L:
            cm = jnp.maximum(cm, jnp.where(row_i >= k, pltpu.roll(cm, k, 0), NEG_INF))
            k *= 2
        packed = jnp.where(lane_i < H, rs, jnp.where(lane_i < 2 * H, cm, bc))
        rows = packed.T[0:3 * H]
        rs_cat = jnp.concatenate([jnp.broadcast_to(rs[:, h:h + 1], (L, L)) for h in range(H)], axis=1)
        e_cat = jnp.where(causal8, rs_cat, NEG_INF)
        s_cat = jnp.concatenate([_dot(k_pair(h, ts), pad_head(h, qt_s[h * M_DK:(h + 1) * M_DK, ts]))
                                 for h in range(H)], axis=1)
        return rows, e_cat, s_cat

    def rec(u, pre_u, fillers):
        ts = slice(u * L, (u + 1) * L)
        rows, e_cat, s_cat = pre_u
        rs_t = rows[0:H]
        cm_t = rows[H:2 * H]
        b_t = rows[2 * H:3 * H]
        m0 = m_s[...]
        a_t = jnp.maximum(m0, cm_t)
        w_inter = jnp.exp(m0 - a_t)
        emx = jnp.exp(-(b_t + a_t))
        m_new = jnp.broadcast_to((b_t + a_t)[:, L - 1:L], (H, L))
        b_last = jnp.broadcast_to(b_t[:, L - 1:L], (H, L))
        w_end = jnp.exp(rs_t + (b_last - m_new))
        decay = jnp.exp(b_last + m0 - m_new)
        m_s[...] = m_new
        a_cat = jnp.concatenate([a_t[h:h + 1, :] for h in range(H)], axis=1)
        p_cat = (s_cat * jnp.exp(e_cat - a_cat)).astype(BF16)
        for h in range(H):
            if fillers:
                fillers.pop(0)()
            va = jnp.concatenate([vt_s[h * M_DV:(h + 1) * M_DV, ts], ones_rows], axis=0)
            ct = c_s[h]
            qw = (qt_s[h * M_DK:(h + 1) * M_DK, ts].astype(F32) * w_inter[h:h + 1, :]).astype(BF16)
            lhs = jnp.concatenate([va.astype(BF16), ct.astype(BF16)], axis=1)
            rhs = jnp.concatenate([p_cat[:, h * L:(h + 1) * L], pad_head(h, qw)], axis=0)
            nd = _dot(lhs, rhs)
            inv = 1.0 / jnp.maximum(jnp.abs(nd[M_DV:M_DV + 1, :]), emx[h:h + 1, :])
            ht_s[h * M_DV:(h + 1) * M_DV, ts] = nd[0:M_DV, :] * inv
            upd = _dot((va * w_end[h:h + 1, :]).astype(BF16), k_pair(h, ts))
            c_s[h] = decay[h:h + 1, :] * ct + upd
        while fillers:
            fillers.pop(0)()

    first, rest = proj_pieces(0)
    for piece in first + rest:
        piece()
    pre_next = pre(0)
    for u in range(units):
        sec, slot = divmod(u, section // L)
        fillers = []
        if sec + 1 < n_sec:
            fillers += proj_pieces(sec + 1)[slot]
        if sec >= 1:
            fillers += post_pieces(sec - 1)[slot]
        pre_cur = pre_next
        if u + 1 < units:
            pre_next = pre(u + 1)
        rec(u, pre_cur, fillers)
    for half in post_pieces(n_sec - 1):
        for piece in half:
            piece()

    @pl.when(j == n_j - 1)
    def _():
        cout_ref[0] = c_s[...]
        mout_ref[0] = m_s[...]


def _mlstm_prompt_layer(x2d, w_in, b_gate, g_norm, w_out, ln_g, ln_b, *, alpha, batch, seq, tt):
    n, d = x2d.shape
    assert n == batch * seq and seq % tt == 0 and tt % P_SECTION == 0 and P_SECTION == 2 * P_L
    assert P_GATE_COPIES * M_HEADS <= LANES and d % 4 == 0
    n_j = seq // tt
    hk = M_HEADS * M_DK
    hv = M_HEADS * M_DV
    wqt = w_in[:, 0:hk].T.astype(BF16)
    wk = (w_in[:, hk:2 * hk] * (M_DK ** -0.5)).astype(BF16)
    wvt = w_in[:, 2 * hk:2 * hk + hv].T.astype(BF16)
    wot = w_in[:, 2 * hk + hv:2 * hk + 2 * hv].T.astype(BF16)
    rep = lambda a: jnp.pad(jnp.tile(a, (1, P_GATE_COPIES)), ((0, 0), (0, LANES - P_GATE_COPIES * M_HEADS)))
    g0 = 2 * hk + 2 * hv
    wg = jnp.concatenate([rep(w_in[:, g0:g0 + M_HEADS]), rep(w_in[:, g0 + M_HEADS:])], axis=1).astype(BF16)
    bg = jnp.concatenate([rep(b_gate[None, 0:M_HEADS]), rep(b_gate[None, M_HEADS:])], axis=1).astype(F32)
    gn = jnp.broadcast_to(g_norm.astype(F32)[:, None], (hv, LANES))
    wout = w_out.astype(BF16)
    consts = (wqt, wk, wvt, wot, wg, bg, gn, wout, ln_g, ln_b)
    y, c_out, m_out = pl.pallas_call(
        functools.partial(_mlstm_pipe_kernel, alpha=alpha, tt=tt, n_j=n_j),
        grid=(batch, n_j),
        in_specs=[pl.BlockSpec((tt, d), lambda b, j: (b * n_j + j, 0))] + [_const_spec(a.shape) for a in consts],
        out_specs=[
            pl.BlockSpec((tt, d), lambda b, j: (b * n_j + j, 0)),
            pl.BlockSpec((1, M_HEADS, P_AUG, 2 * M_DK), lambda b, j: (b, 0, 0, 0)),
            pl.BlockSpec((1, M_HEADS, LANES), lambda b, j: (b, 0, 0)),
        ],
        out_shape=[
            jax.ShapeDtypeStruct((n, d), F32),
            jax.ShapeDtypeStruct((batch, M_HEADS, P_AUG, 2 * M_DK), F32),
            jax.ShapeDtypeStruct((batch, M_HEADS, LANES), F32),
        ],
        scratch_shapes=[
            pltpu.VMEM((tt, d), BF16),
            pltpu.VMEM((hk, tt), BF16),
            pltpu.VMEM((tt, hk), BF16),
            pltpu.VMEM((hv, tt), F32),
            pltpu.VMEM((hv, tt), F32),
            pltpu.VMEM((tt, 2 * LANES), F32),
            pltpu.VMEM((hv, tt), F32),
            pltpu.VMEM((hv, tt), BF16),
            pltpu.VMEM((M_HEADS, P_AUG, 2 * M_DK), F32),
            pltpu.VMEM((M_HEADS, LANES), F32),
        ],
        compiler_params=pltpu.CompilerParams(
            dimension_semantics=("arbitrary", "arbitrary"), vmem_limit_bytes=VMEM_LIMIT_BYTES),
        name="mlstm_prompt",
    )(x2d, *consts)
    ct = jnp.stack([c_out[:, 0::2, :, 0:M_DK], c_out[:, 1::2, :, M_DK:]], axis=2)
    ct = ct.reshape(batch, M_HEADS, P_AUG, M_DK)
    c_new = jnp.swapaxes(ct[:, :, 0:M_DV, :], -1, -2)
    n_new = ct[:, :, M_DV, :]
    m_new = m_out[:, :, 0]
    return y, c_new, n_new, m_new


PAIRS = GROUP // 2


def _attn_unit(q_rows, k_lo, k_hi, v_lo, v_hi, bias_lo, bias_hi, sink_lo, sink_hi, key_ok):
    outs = []
    for k_op, v_op, bias, sink in ((k_lo, v_lo, bias_lo, sink_lo), (k_hi, v_hi, bias_hi, sink_hi)):
        s = _dot_nt(q_rows, k_op) + bias
        if key_ok is not None:
            s = jnp.where(key_ok, s, NEG_INF)
        mx = jnp.maximum(jnp.max(s, axis=-1, keepdims=True), sink)
        p = jnp.exp(s - mx)
        den = jnp.sum(p, axis=-1, keepdims=True) + jnp.exp(sink - mx)
        outs.append(_dot(p.astype(BF16), v_op) / den)
    return outs[0] + outs[1]


def _split_kv(kv):
    lane = lax.broadcasted_iota(jnp.int32, (1, LANES), 1)
    low = lane < HEAD_DIM
    res = []
    kk = kv[:, 0:LANES]
    vv = kv[:, LANES:2 * LANES]
    kk_r = pltpu.roll(kk, HEAD_DIM, 1)
    vv_r = pltpu.roll(vv, HEAD_DIM, 1)
    z = jnp.zeros_like(kk)
    res.append((jnp.where(low, kk, z), jnp.where(low, z, kk_r), jnp.where(low, vv, z), jnp.where(low, z, vv_r)))
    res.append((jnp.where(low, kk_r, z), jnp.where(low, z, kk), jnp.where(low, vv_r, z), jnp.where(low, z, vv)))
    return [tuple(a.astype(BF16) for a in grp) for grp in res]


SWA_UNIT = 2 * CHUNK
SWA_KEYS = WINDOW + SWA_UNIT
SWA_SECTION = 256


def _swa_prompt_kernel(x_ref, kc_ref, kp_ref, vtc_ref, vtp_ref, wqt_ref, wo_ref, bias_ref, sink_ref,
                       lg_ref, lb_ref, y_ref, xb_s, qt_s, ot_s, k_s, vt_s, *, alpha, tq, section):
    j = pl.program_id(1)
    k_s[0:WINDOW, :] = kp_ref[...].astype(BF16)
    k_s[WINDOW:WINDOW + tq, :] = kc_ref[...].astype(BF16)
    vt_s[:, 0:WINDOW] = vtp_ref[...]
    vt_s[:, WINDOW:WINDOW + tq] = vtc_ref[...]
    zeros = jnp.zeros((HEAD_DIM, GROUP * SWA_UNIT), BF16)
    key_i = lax.broadcasted_iota(jnp.int32, (SWA_KEYS, GROUP * SWA_UNIT), 0)
    d = x_ref.shape[1]
    n_sec = tq // section
    per_sec = (section // SWA_UNIT) * N_KV_HEADS
    rows_of = lambda sec: slice(sec * section, (sec + 1) * section)

    def project(sec, piece):
        rs = rows_of(sec)
        if piece == 0:
            xb_s[rs, :] = x_ref[rs, :].astype(BF16)
        blk = slice(piece * (d // per_sec), (piece + 1) * (d // per_sec))
        qt_s[blk, rs] = _dot_nt(wqt_ref[blk, :], xb_s[rs, :]).astype(BF16)

    def out_project(sec, piece):
        rs = rows_of(sec)
        blk = slice(piece * (d // per_sec), (piece + 1) * (d // per_sec))
        y_ref[rs, blk] = _dot_tn(ot_s[:, rs], wo_ref[:, blk])

    def normalise(sec):
        rs = rows_of(sec)
        y_ref[rs, :] = _layer_norm(alpha * x_ref[rs, :] + y_ref[rs, :], lg_ref[...], lb_ref[...])

    def scores(u, g):
        r0 = u * SWA_UNIT
        qt_g = jnp.concatenate(
            [qt_s[h * HEAD_DIM:(h + 1) * HEAD_DIM, r0:r0 + SWA_UNIT] for h in range(g * GROUP, (g + 1) * GROUP)],
            axis=1)
        qz = jnp.concatenate([qt_g, zeros] if g == 0 else [zeros, qt_g], axis=0)
        s_t = _dot(k_s[r0:r0 + SWA_KEYS, :], qz) + bias_ref[g]
        if r0 < WINDOW:
            s_t = jnp.where(j * tq + r0 - WINDOW + key_i >= 0, s_t, NEG_INF)
        return s_t

    def finish(u, g, s_t):
        r0 = u * SWA_UNIT
        sink = sink_ref[g]
        mx = jnp.maximum(jnp.max(s_t, axis=0, keepdims=True), sink)
        p = jnp.exp2(s_t - mx)
        den = jnp.sum(p, axis=0, keepdims=True) + jnp.exp2(sink - mx)
        vt_g = vt_s[g * HEAD_DIM:(g + 1) * HEAD_DIM, r0:r0 + SWA_KEYS]
        o_t = (_dot(vt_g, p.astype(BF16)) / den).astype(BF16)
        for i in range(GROUP):
            h = g * GROUP + i
            ot_s[h * HEAD_DIM:(h + 1) * HEAD_DIM, r0:r0 + SWA_UNIT] = o_t[:, i * SWA_UNIT:(i + 1) * SWA_UNIT]

    todo = [(u, g) for u in range(tq // SWA_UNIT) for g in range(N_KV_HEADS)]
    for piece in range(per_sec):
        project(0, piece)
    s_next = scores(*todo[0])
    for idx, (u, g) in enumerate(todo):
        sec, slot = divmod(idx, per_sec)
        if sec + 1 < n_sec:
            project(sec + 1, slot)
        if sec >= 1:
            out_project(sec - 1, slot)
        s_cur = s_next
        if idx + 1 < len(todo):
            s_next = scores(*todo[idx + 1])
        finish(u, g, s_cur)
        if sec >= 1 and slot == per_sec - 1:
            normalise(sec - 1)
    for piece in range(per_sec):
        out_project(n_sec - 1, piece)
    normalise(n_sec - 1)


def _swa_prompt_layer(x2d, kv2d, vt, wqt, wo, bias, sinks, ln_g, ln_b, *, alpha, batch, seq, tq):
    n, d = x2d.shape
    section = SWA_SECTION
    assert n == batch * seq and seq % tq == 0 and tq % section == 0 and section % SWA_UNIT == 0 and WINDOW == LANES
    assert d % ((section // SWA_UNIT) * N_KV_HEADS) == 0
    n_j = seq // tq
    per = tq // WINDOW
    prev = lambda b, j: jnp.maximum((b * n_j + j) * per - 1, 0)
    return pl.pallas_call(
        functools.partial(_swa_prompt_kernel, alpha=alpha, tq=tq, section=section),
        grid=(batch, n_j),
        in_specs=[
            pl.BlockSpec((tq, d), lambda b, j: (b * n_j + j, 0)),
            pl.BlockSpec((tq, LANES), lambda b, j: (b * n_j + j, 0)),
            pl.BlockSpec((WINDOW, LANES), lambda b, j: (prev(b, j), 0)),
            pl.BlockSpec((LANES, tq), lambda b, j: (0, b * n_j + j)),
            pl.BlockSpec((LANES, WINDOW), lambda b, j: (0, prev(b, j))),
            _const_spec(wqt.shape),
            _const_spec(wo.shape),
            _const_spec(bias.shape),
            _const_spec(sinks.shape),
            _const_spec(ln_g.shape),
            _const_spec(ln_b.shape),
        ],
        out_specs=pl.BlockSpec((tq, d), lambda b, j: (b * n_j + j, 0)),
        out_shape=jax.ShapeDtypeStruct((n, d), F32),
        scratch_shapes=[
            pltpu.VMEM((tq, d), BF16),
            pltpu.VMEM((d, tq), BF16),
            pltpu.VMEM((d, tq), BF16),
            pltpu.VMEM((WINDOW + tq, LANES), BF16),
            pltpu.VMEM((LANES, WINDOW + tq), BF16),
        ],
        compiler_params=pltpu.CompilerParams(
            dimension_semantics=("arbitrary", "arbitrary"), vmem_limit_bytes=VMEM_LIMIT_BYTES),
        name="swa_prompt",
    )(x2d, kv2d, kv2d, vt, vt, wqt, wo, bias, sinks, ln_g, ln_b)


def _swa_prompt_tables(sinks):
    slopes = jnp.exp2(-8.0 * jnp.arange(1, N_Q_HEADS + 1, dtype=F32) / N_Q_HEADS)
    q = jnp.arange(SWA_UNIT)[None, :]
    kx = jnp.arange(SWA_KEYS)[:, None]
    dist = jnp.abs(q + WINDOW - kx).astype(F32)
    first = q < CHUNK
    visible = (first & (kx < WINDOW + CHUNK)) | (~first & (kx >= CHUNK))
    bias = jnp.where(visible[None], -(LOG2_E * slopes)[:, None, None] * dist[None], NEG_INF)
    sinks = sinks.astype(F32) * LOG2_E
    bias = bias.reshape(N_KV_HEADS, GROUP, SWA_KEYS, SWA_UNIT).transpose(0, 2, 1, 3)
    sink = jnp.broadcast_to(sinks.astype(F32).reshape(N_KV_HEADS, 1, GROUP, 1), (N_KV_HEADS, 1, GROUP, SWA_UNIT))
    return (bias.reshape(N_KV_HEADS, SWA_KEYS, GROUP * SWA_UNIT), sink.reshape(N_KV_HEADS, 1, GROUP * SWA_UNIT))


def _swa_sample_kernel(x_ref, kvn_ref, kc_ref, vc_ref, wq_ref, wo_ref, bias_ref, sink_ref, lg_ref, lb_ref,
                       y_ref, q_s, o_s, *, alpha, batch, seq):
    x = x_ref[...]
    xb = x.astype(BF16)
    q_s[...] = _dot(xb, wq_ref[...]).astype(BF16)
    for b in range(batch):
        r0 = b * seq
        kv_new = kvn_ref[r0:r0 + seq, :]
        kv_old = jnp.concatenate([kc_ref[b], vc_ref[b]], axis=1)
        groups = _split_kv(jnp.concatenate([kv_old, kv_new], axis=0))
        for g in range(N_KV_HEADS):
            q_rows = jnp.concatenate(
                [q_s[r0:r0 + seq, (g * PAIRS + pp) * LANES:(g * PAIRS + pp + 1) * LANES] for pp in range(PAIRS)],
                axis=0)
            ks = groups[g]
            o = _attn_unit(q_rows, ks[0], ks[1], ks[2], ks[3], bias_ref[g, 0], bias_ref[g, 1],
                           sink_ref[g, 0], sink_ref[g, 1], None)
            for pp in range(PAIRS):
                o_s[r0:r0 + seq, (g * PAIRS + pp) * LANES:(g * PAIRS + pp + 1) * LANES] = (
                    o[pp * seq:(pp + 1) * seq, :].astype(BF16))
    mix = _dot(o_s[...], wo_ref[...])
    y_ref[...] = _layer_norm(alpha * x + mix, lg_ref[...], lb_ref[...])


def _swa_sample_layer(x2d, kv_new, k_cache, v_cache, wq, wo, bias, sink, ln_g, ln_b, *, alpha, batch, seq):
    n, d = x2d.shape
    assert n == batch * seq
    args = (x2d, kv_new, k_cache, v_cache, wq, wo, bias, sink, ln_g, ln_b)
    return pl.pallas_call(
        functools.partial(_swa_sample_kernel, alpha=alpha, batch=batch, seq=seq),
        grid=(1,),
        in_specs=[_const_spec(a.shape) for a in args],
        out_specs=pl.BlockSpec((n, d), lambda i: (0, 0)),
        out_shape=jax.ShapeDtypeStruct((n, d), F32),
        scratch_shapes=[pltpu.VMEM((n, d), BF16), pltpu.VMEM((n, d), BF16)],
        compiler_params=pltpu.CompilerParams(
            dimension_semantics=("arbitrary",), vmem_limit_bytes=VMEM_LIMIT_BYTES),
        name="swa_sample",
    )(*args)


def _attn_tables(sinks, ql, nk):
    slopes = jnp.exp2(-8.0 * jnp.arange(1, N_Q_HEADS + 1, dtype=F32) / N_Q_HEADS)
    dist = jnp.abs(jnp.arange(ql)[:, None] - jnp.arange(nk)[None, :] + (nk - ql)).astype(F32)
    head = (jnp.arange(N_KV_HEADS)[:, None, None] * GROUP + 2 * jnp.arange(PAIRS)[None, None, :]
            + jnp.arange(2)[None, :, None])
    bias = -slopes[head][..., None, None] * dist
    sink = jnp.broadcast_to(sinks.astype(F32)[head][..., None, None], head.shape + (ql, 1))
    return (bias.reshape(N_KV_HEADS, 2, PAIRS * ql, nk), sink.reshape(N_KV_HEADS, 2, PAIRS * ql, 1))


def _trunk(x, c0s, n0s, m0s, k_cache, v_cache, params, *, is_prompt, kv_rows):
    (w_in_a, b_gate_a, g_norm_a, w_out_a, w_kv, w_q_b, sinks_b, w_out_b, w_gu, w_down, ln_g, ln_b) = params
    batch, seq, d = x.shape
    depth = w_gu.shape[0]
    n_a = w_in_a.shape[0]
    alpha = (2 * depth) ** 0.25
    d_ff = w_down.shape[1]
    n = batch * seq
    x2d = x.reshape(n, d)
    tm = min(512, n)
    if is_prompt:
        L, bb_n, tt = 128, 1, min(1024, seq)
    else:
        L, bb_n, tt = seq, 8, seq
    cs, ns, ms = [], [], []
    kv2d = None
    for layer in range(depth):
        row = lambda a: a.astype(F32)[None, :]
        if layer < n_a and is_prompt:
            x2d, c, nn, m = _mlstm_prompt_layer(
                x2d, w_in_a[layer], b_gate_a[layer], g_norm_a[layer], w_out_a[layer],
                row(ln_g[layer, 0]), row(ln_b[layer, 0]), alpha=alpha, batch=batch, seq=seq, tt=tt)
            cs.append(c)
            ns.append(nn)
            ms.append(m)
        elif layer < n_a:
            w_all, bg, gn, wo = _prep_mlstm_weights(w_in_a[layer], b_gate_a[layer], g_norm_a[layer], w_out_a[layer])
            m0 = jnp.broadcast_to(m0s[layer].astype(F32)[:, :, None, None], (batch, M_HEADS, 8, LANES))
            x2d, c, nn, m = _mlstm_layer(
                x2d, c0s[layer].astype(F32), n0s[layer].astype(F32)[..., None], m0, w_all, bg, gn, wo,
                row(ln_g[layer, 0]), row(ln_b[layer, 0]), alpha=alpha, batch=batch, seq=seq, L=L, bb_n=bb_n, tt=tt)
            cs.append(c)
            ns.append(nn)
            ms.append(m)
        else:
            jb = layer - n_a
            if kv2d is None:
                kv2d, vt = _kv_proj(x2d, w_kv, tm=tm, with_vt=is_prompt)
            wq = (w_q_b[jb] * (HEAD_DIM ** -0.5)).astype(BF16)
            wo = w_out_b[jb].astype(BF16)
            if is_prompt:
                bias, sink = _swa_prompt_tables(sinks_b[jb])
                wqt = (w_q_b[jb] * (HEAD_DIM ** -0.5 * LOG2_E)).T.astype(BF16)
                x2d = _swa_prompt_layer(x2d, kv2d, vt, wqt, wo, bias, sink,
                                        row(ln_g[layer, 0]), row(ln_b[layer, 0]),
                                        alpha=alpha, batch=batch, seq=seq, tq=min(1024, seq))
            else:
                w_rows = k_cache.shape[1]
                bias, sink = _attn_tables(sinks_b[jb], seq, w_rows + seq)
                x2d = _swa_sample_layer(
                    x2d, kv2d, k_cache.astype(F32).reshape(batch, w_rows, N_KV_HEADS * HEAD_DIM),
                    v_cache.astype(F32).reshape(batch, w_rows, N_KV_HEADS * HEAD_DIM), wq, wo, bias, sink,
                    row(ln_g[layer, 0]), row(ln_b[layer, 0]), alpha=alpha, batch=batch, seq=seq)
        feeds_kv = layer + 1 == n_a and n_a < depth
        x2d, kv_new, vt_new = _ffn(
            x2d, w_gu, w_down, row(ln_g[layer, 1]), row(ln_b[layer, 1]), layer=layer, alpha=alpha,
            tm=min(2 * tm, n), sub=tm, w_kv=w_kv if feeds_kv else None, with_vt=feeds_kv and is_prompt)
        if feeds_kv:
            kv2d, vt = kv_new, vt_new
    kv = kv2d.reshape(batch, seq, kv2d.shape[1])[:, seq - kv_rows:].reshape(batch, kv_rows, 2, N_KV_HEADS, HEAD_DIM)
    return (x2d.reshape(batch, seq, d), jnp.stack(cs), jnp.stack(ns), jnp.stack(ms), kv[:, :, 0], kv[:, :, 1])


def kernel(x_prompt, x_sample, state_C, state_n, state_m, cache_k, cache_v, w_in_a, b_gate_a, g_norm_a,
           w_out_a, w_kv, w_q_b, sinks_b, w_out_b, w_gu, w_down, ln_g, ln_b):
    params = (w_in_a, b_gate_a, g_norm_a, w_out_a, w_kv, w_q_b, sinks_b, w_out_b,
              w_gu.astype(BF16), w_down.astype(BF16), ln_g, ln_b)
    y_p, p_c, p_n, p_m, p_k, p_v = _trunk(x_prompt, None, None, None, None, None, params, is_prompt=True,
                                          kv_rows=min(WINDOW, x_prompt.shape[1]))
    y_s, s_c, s_n, s_m, s_k, s_v = _trunk(x_sample, state_C, state_n, state_m, cache_k, cache_v, params,
                                          is_prompt=False, kv_rows=x_sample.shape[1])
    return (y_p, y_s, p_c, p_n, p_m, p_k, p_v, s_c, s_n, s_m, s_k, s_v)
```

```python
import functools
from typing import NamedTuple

import jax
import jax.numpy as jnp
from jax import lax
from jax.experimental import pallas as pl
from jax.experimental.pallas import tpu as pltpu

F32 = jnp.float32
BF16 = jnp.bfloat16

CHUNK = 64
WINDOW = 128
M_HEADS = 8
M_DK = 64
M_DV = 128
N_Q_HEADS = 16
N_KV_HEADS = 2
HEAD_DIM = 64
GROUP = N_Q_HEADS // N_KV_HEADS
LN_EPS = 1e-5
HEAD_NORM_EPS = 1e-6

LANES = 128
LOG2_E = 1.4426950408889634
VMEM_LIMIT_BYTES = 56 * 1024 * 1024

NEG_INF = float("-inf")
HIGHEST = lax.Precision.HIGHEST


def _const_spec(shape):
    nd = len(shape)
    return pl.BlockSpec(shape, lambda *_: (0,) * nd, pipeline_mode=pl.Buffered(1))


def _layer_norm(y, g, b):
    mu = jnp.mean(y, axis=-1, keepdims=True)
    yc = y - mu
    var = jnp.mean(yc * yc, axis=-1, keepdims=True)
    return yc * lax.rsqrt(var + LN_EPS) * g + b


def _dot(a, b):
    return jnp.dot(a, b, preferred_element_type=F32)


def _dot_nt(a, b):
    return lax.dot_general(a, b, (((1,), (1,)), ((), ())), preferred_element_type=F32)


def _dot_tn(a, b):
    return lax.dot_general(a, b, (((0,), (0,)), ((), ())), preferred_element_type=F32)


FF_BLOCK = 256


def _ffn_kernel(*refs, alpha, d_ff, sub, with_kv, with_vt):
    x_ref, wgu_ref, wd_ref, g_ref, b_ref = refs[:5]
    rest = list(refs[5:])
    wkv_ref = rest.pop(0) if with_kv else None
    wvt_ref = rest.pop(0) if with_vt else None
    o_ref = rest.pop(0)
    kv_ref = rest.pop(0) if with_kv else None
    vt_ref = rest.pop(0) if with_vt else None
    acc_ref = rest.pop(0)
    h_ref = rest.pop(0)
    tm = x_ref.shape[0]

    n_fin = 4

    def finish(i, part):
        step = sub // n_fin
        rs = slice(i * sub + part * step, i * sub + (part + 1) * step)
        y = _layer_norm(alpha * x_ref[rs, :] + acc_ref[rs, :], g_ref[...], b_ref[...])
        o_ref[rs, :] = y
        if with_kv:
            yb = y.astype(BF16)
            kv_ref[rs, :] = _dot(yb, wkv_ref[...])
            if with_vt:
                vt_ref[:, rs] = _dot_nt(wvt_ref[...], yb).astype(BF16)

    for i in range(tm // sub):
        rs = slice(i * sub, (i + 1) * sub)
        xb = x_ref[rs, :].astype(BF16)
        for j in range(d_ff // FF_BLOCK):
            sl = slice(j * FF_BLOCK, (j + 1) * FF_BLOCK)
            su = slice(d_ff + j * FF_BLOCK, d_ff + (j + 1) * FF_BLOCK)
            g = _dot(xb, wgu_ref[:, sl])
            u = _dot(xb, wgu_ref[:, su])
            h_ref[rs, sl] = (g * jax.nn.sigmoid(g) * u).astype(BF16)
            if i > 0 and j % 2 == 0 and j // 2 < n_fin:
                finish(i - 1, j // 2)
        acc_ref[rs, :] = _dot(h_ref[rs, :], wd_ref[...])
    for part in range(n_fin):
        finish(tm // sub - 1, part)


def _layer_spec(shape, layer):
    nd = len(shape) - 1
    return pl.BlockSpec((None,) + tuple(shape[1:]), lambda *_: (layer,) + (0,) * nd, pipeline_mode=pl.Buffered(1))


def _ffn(x2d, w_gu, w_down, ln_g, ln_b, *, layer, alpha, tm, sub, w_kv=None, with_vt=False):
    n, d = x2d.shape
    d_ff = w_down.shape[1]
    assert n % tm == 0 and tm % sub == 0 and d_ff % FF_BLOCK == 0 and w_gu.shape[2] == 2 * d_ff
    with_kv = w_kv is not None
    assert with_kv or not with_vt
    args = [x2d, w_gu, w_down, ln_g, ln_b]
    in_specs = [pl.BlockSpec((tm, d), lambda i: (i, 0)), _layer_spec(w_gu.shape, layer),
                _layer_spec(w_down.shape, layer), _const_spec(ln_g.shape), _const_spec(ln_b.shape)]
    out_specs = [pl.BlockSpec((tm, d), lambda i: (i, 0))]
    out_shape = [jax.ShapeDtypeStruct((n, d), F32)]
    if with_kv:
        nk = w_kv.shape[1]
        args.append(w_kv.astype(BF16))
        in_specs.append(_const_spec(w_kv.shape))
        out_specs.append(pl.BlockSpec((tm, nk), lambda i: (i, 0)))
        out_shape.append(jax.ShapeDtypeStruct((n, nk), F32))
        if with_vt:
            w_vt = w_kv[:, nk // 2:].T.astype(BF16)
            args.append(w_vt)
            in_specs.insert(len(in_specs), _const_spec(w_vt.shape))
            out_specs.append(pl.BlockSpec((nk // 2, tm), lambda i: (0, i)))
            out_shape.append(jax.ShapeDtypeStruct((nk // 2, n), BF16))
    outs = pl.pallas_call(
        functools.partial(_ffn_kernel, alpha=alpha, d_ff=d_ff, sub=sub, with_kv=with_kv, with_vt=with_vt),
        grid=(n // tm,),
        in_specs=in_specs,
        out_specs=out_specs,
        out_shape=out_shape,
        scratch_shapes=[pltpu.VMEM((tm, d), F32), pltpu.VMEM((tm, d_ff), BF16)],
        compiler_params=pltpu.CompilerParams(
            dimension_semantics=("arbitrary",), vmem_limit_bytes=VMEM_LIMIT_BYTES),
        name="ffn_kv" if with_kv else "ffn",
    )(*args)
    return outs[0], (outs[1] if with_kv else None), (outs[2] if with_vt else None)


def _kv_kernel(x_ref, w_ref, o_ref):
    o_ref[...] = _dot(x_ref[...].astype(BF16), w_ref[...])


def _kv_t_kernel(x_ref, w_ref, wvt_ref, o_ref, vt_ref):
    xb = x_ref[...].astype(BF16)
    o_ref[...] = _dot(xb, w_ref[...])
    vt_ref[...] = _dot_nt(wvt_ref[...], xb).astype(BF16)


def _kv_proj(x2d, w_kv, *, tm, with_vt):
    n, d = x2d.shape
    nk = w_kv.shape[1]
    assert n % tm == 0
    params = pltpu.CompilerParams(dimension_semantics=("arbitrary",), vmem_limit_bytes=VMEM_LIMIT_BYTES)
    x_spec = pl.BlockSpec((tm, d), lambda i: (i, 0))
    kv_spec = pl.BlockSpec((tm, nk), lambda i: (i, 0))
    w_kv_b = w_kv.astype(BF16)
    if not with_vt:
        kv = pl.pallas_call(
            _kv_kernel, grid=(n // tm,), in_specs=[x_spec, _const_spec(w_kv.shape)], out_specs=kv_spec,
            out_shape=jax.ShapeDtypeStruct((n, nk), F32), compiler_params=params, name="kv_proj",
        )(x2d, w_kv_b)
        return kv, None
    nv = nk // 2
    w_vt = w_kv[:, nv:].T.astype(BF16)
    return pl.pallas_call(
        _kv_t_kernel,
        grid=(n // tm,),
        in_specs=[x_spec, _const_spec(w_kv.shape), _const_spec(w_vt.shape)],
        out_specs=[kv_spec, pl.BlockSpec((nv, tm), lambda i: (0, i))],
        out_shape=[jax.ShapeDtypeStruct((n, nk), F32), jax.ShapeDtypeStruct((nv, n), BF16)],
        compiler_params=params,
        name="kv_proj_t",
    )(x2d, w_kv_b, w_vt)


A_QOFF = 0
A_KOFF = M_HEADS * LANES
A_VOFF = 2 * M_HEADS * LANES
A_OOFF = 3 * M_HEADS * LANES
A_GOFF = 4 * M_HEADS * LANES
A_COLS = A_GOFF + 2 * LANES
A_COL_BLOCK = 512


def _mlstm_kernel(x_ref, c0_ref, n0_ref, m0_ref, w_ref, bg_ref, gn_ref, wo_ref, lg_ref, lb_ref,
                  y_ref, cout_ref, mout_ref,
                  q_s, k_s, v_s, o_s, g_s, h_s, hb_s, c_s, m_s,
                  *, alpha, L, bb_n, tt, n_j):
    j = pl.program_id(1)
    rows = bb_n * tt
    units_per_b = tt // L
    hd = M_HEADS * LANES

    @pl.when(j == 0)
    def _():
        lane = lax.broadcasted_iota(jnp.int32, (1, 1, 1, LANES), 3)
        c_s[:, :, :, 0:LANES] = c0_ref[...]
        c_s[:, :, :, LANES:2 * LANES] = jnp.where(lane == 0, n0_ref[...], 0.0)
        m_s[...] = m0_ref[...]

    x = x_ref[...]
    xb = x.astype(BF16)
    for cb in range(0, hd, A_COL_BLOCK):
        q_s[:, cb:cb + A_COL_BLOCK] = _dot(xb, w_ref[:, A_QOFF + cb:A_QOFF + cb + A_COL_BLOCK]).astype(BF16)
        k_s[:, cb:cb + A_COL_BLOCK] = _dot(xb, w_ref[:, A_KOFF + cb:A_KOFF + cb + A_COL_BLOCK])
        v_s[:, cb:cb + A_COL_BLOCK] = _dot(xb, w_ref[:, A_VOFF + cb:A_VOFF + cb + A_COL_BLOCK]).astype(BF16)
        o_s[:, cb:cb + A_COL_BLOCK] = _dot(xb, w_ref[:, A_OOFF + cb:A_OOFF + cb + A_COL_BLOCK])
    g_s[...] = _dot(xb, w_ref[:, A_GOFF:A_GOFF + 2 * LANES]) + bg_ref[...]

    row_i = lax.broadcasted_iota(jnp.int32, (L, L), 0)
    col_i = lax.broadcasted_iota(jnp.int32, (L, L), 1)
    causal = row_i >= col_i
    tri = causal.astype(F32)
    ones_col = jnp.where(lax.broadcasted_iota(jnp.int32, (L, LANES), 1) == 0, 1.0, 0.0).astype(BF16)

    def unit(u, carry):
        r0 = pl.multiple_of(u * L, L)
        bb = u // units_per_b if bb_n > 1 else 0
        gi = g_s[pl.ds(r0, L), 0:LANES]
        gf = g_s[pl.ds(r0, L), LANES:2 * LANES]
        lf = jnp.minimum(gf, 0.0) - jnp.log1p(jnp.exp(-jnp.abs(gf)))
        bc = lax.dot_general(tri, lf, (((1,), (0,)), ((), ())), precision=HIGHEST,
                             preferred_element_type=F32)
        rs = gi - bc
        rs_t = rs.T
        b_last = bc[L - 1:L, :]
        heads = range(M_HEADS)
        hs_of = lambda h: slice(h * LANES, (h + 1) * LANES)
        qs = [q_s[pl.ds(r0, L), hs_of(h)] for h in heads]
        ks = [k_s[pl.ds(r0, L), hs_of(h)] for h in heads]
        vas = [jnp.concatenate([v_s[pl.ds(r0, L), hs_of(h)], ones_col], axis=1) for h in heads]
        m0s = [m_s[bb, h, 0:1, 0:1] for h in heads]
        cs = [c_s[bb, h] for h in heads]
        ss = [_dot_nt(qs[h], ks[h].astype(BF16)) for h in heads]
        qcs = [_dot(qs[h][:, 0:M_DK], cs[h].astype(BF16)) for h in heads]
        b_cols = [bc[:, h:h + 1] for h in heads]
        logds = [jnp.where(causal, b_cols[h] + rs_t[h:h + 1, :], NEG_INF) for h in heads]
        inters = [b_cols[h] + m0s[h] for h in heads]
        mxs = [jnp.maximum(inters[h], jnp.max(logds[h], axis=1, keepdims=True)) for h in heads]
        ps = [(ss[h] * jnp.exp(logds[h] - mxs[h])).astype(BF16) for h in heads]
        nds = [jnp.exp(inters[h] - mxs[h]) * qcs[h] + _dot(ps[h], vas[h]) for h in heads]
        for h in heads:
            num = nds[h][:, 0:M_DV]
            den = nds[h][:, M_DV:M_DV + 1]
            h_s[pl.ds(r0, L), hs_of(h)] = num / jnp.maximum(jnp.abs(den), jnp.exp(-mxs[h]))
        m_news = [mxs[h][L - 1:L, :] for h in heads]
        bls = [b_last[:, h:h + 1] for h in heads]
        wks = [(ks[h][:, 0:M_DK] * jnp.exp(rs[:, h:h + 1] + (bls[h] - m_news[h]))).astype(BF16) for h in heads]
        upds = [_dot_tn(wks[h], vas[h]) for h in heads]
        for h in heads:
            c_s[bb, h] = jnp.exp(bls[h] + m0s[h] - m_news[h]) * cs[h] + upds[h]
            m_s[bb, h] = jnp.broadcast_to(m_news[h], (8, LANES))
        return carry

    lax.fori_loop(0, rows // L, unit, 0)

    for h in range(M_HEADS):
        hs = slice(h * LANES, (h + 1) * LANES)
        hh = h_s[:, hs]
        hn = hh * lax.rsqrt(jnp.mean(hh * hh, axis=-1, keepdims=True) + HEAD_NORM_EPS)
        hb_s[:, hs] = (hn * gn_ref[:, hs] * jax.nn.sigmoid(o_s[:, hs])).astype(BF16)
    mix = _dot(hb_s[...], wo_ref[...])
    y_ref[...] = _layer_norm(alpha * x + mix, lg_ref[...], lb_ref[...])

    @pl.when(j == n_j - 1)
    def _():
        cout_ref[...] = c_s[...]
        mout_ref[...] = m_s[...]


def _mlstm_layer(x2d, c0, n0, m0, w_all, b_gate, g_norm, w_out, ln_g, ln_b, *, alpha, batch, seq, L, bb_n, tt):
    n, d = x2d.shape
    assert n == batch * seq and seq % tt == 0 and tt % L == 0 and batch % bb_n == 0
    assert bb_n == 1 or tt == seq
    n_j = seq // tt
    rows = bb_n * tt
    hd = M_HEADS * LANES
    kern = functools.partial(_mlstm_kernel, alpha=alpha, L=L, bb_n=bb_n, tt=tt, n_j=n_j)
    st4 = lambda bi, j: (bi, 0, 0, 0)
    y, c_out, m_out = pl.pallas_call(
        kern,
        grid=(batch // bb_n, n_j),
        in_specs=[
            pl.BlockSpec((rows, d), lambda bi, j: (bi * n_j + j, 0)),
            pl.BlockSpec((bb_n, M_HEADS, M_DK, M_DV), st4),
            pl.BlockSpec((bb_n, M_HEADS, M_DK, 1), st4),
            pl.BlockSpec((bb_n, M_HEADS, 8, LANES), st4),
            _const_spec(w_all.shape),
            _const_spec(b_gate.shape),
            _const_spec(g_norm.shape),
            _const_spec(w_out.shape),
            _const_spec(ln_g.shape),
            _const_spec(ln_b.shape),
        ],
        out_specs=[
            pl.BlockSpec((rows, d), lambda bi, j: (bi * n_j + j, 0)),
            pl.BlockSpec((bb_n, M_HEADS, M_DK, 2 * LANES), st4),
            pl.BlockSpec((bb_n, M_HEADS, 8, LANES), st4),
        ],
        out_shape=[
            jax.ShapeDtypeStruct((n, d), F32),
            jax.ShapeDtypeStruct((batch, M_HEADS, M_DK, 2 * LANES), F32),
            jax.ShapeDtypeStruct((batch, M_HEADS, 8, LANES), F32),
        ],
        scratch_shapes=[
            pltpu.VMEM((rows, hd), BF16),
            pltpu.VMEM((rows, hd), F32),
            pltpu.VMEM((rows, hd), BF16),
            pltpu.VMEM((rows, hd), F32),
            pltpu.VMEM((rows, 2 * LANES), F32),
            pltpu.VMEM((rows, hd), F32),
            pltpu.VMEM((rows, hd), BF16),
            pltpu.VMEM((bb_n, M_HEADS, M_DK, 2 * LANES), F32),
            pltpu.VMEM((bb_n, M_HEADS, 8, LANES), F32),
        ],
        compiler_params=pltpu.CompilerParams(
            dimension_semantics=("arbitrary", "arbitrary"), vmem_limit_bytes=VMEM_LIMIT_BYTES),
        name="mlstm_layer",
    )(x2d, c0, n0, m0, w_all, b_gate, g_norm, w_out, ln_g, ln_b)
    c_new = c_out[..., 0:M_DV]
    n_new = c_out[..., M_DV]
    m_new = m_out[:, :, 0, 0]
    return y, c_new, n_new, m_new


def _prep_mlstm_weights(w_in, b_gate, g_norm, w_out):
    d = w_in.shape[0]
    hk = M_HEADS * M_DK
    hv = M_HEADS * M_DV
    wq = w_in[:, 0:hk].reshape(d, M_HEADS, M_DK)
    wk = w_in[:, hk:2 * hk].reshape(d, M_HEADS, M_DK) * (M_DK ** -0.5)
    pad = ((0, 0), (0, 0), (0, LANES - M_DK))
    wq = jnp.pad(wq, pad).reshape(d, M_HEADS * LANES)
    wk = jnp.pad(wk, pad).reshape(d, M_HEADS * LANES)
    wv = w_in[:, 2 * hk:2 * hk + hv]
    wo = w_in[:, 2 * hk + hv:2 * hk + 2 * hv]
    wi = jnp.pad(w_in[:, 2 * hk + 2 * hv:2 * hk + 2 * hv + M_HEADS], ((0, 0), (0, LANES - M_HEADS)))
    wf = jnp.pad(w_in[:, 2 * hk + 2 * hv + M_HEADS:], ((0, 0), (0, LANES - M_HEADS)))
    w_all = jnp.concatenate([wq, wk, wv, wo, wi, wf], axis=1).astype(BF16)
    bg = jnp.concatenate([jnp.pad(b_gate[0:M_HEADS], (0, LANES - M_HEADS)),
                          jnp.pad(b_gate[M_HEADS:], (0, LANES - M_HEADS))]).astype(F32)[None, :]
    return w_all, bg, g_norm.astype(F32)[None, :], w_out.astype(BF16)


P_L = LANES
P_AUG = M_DV + 16
P_GATE_COPIES = 3
P_FGATE_LANE = 32


P_SECTION = 256


def _mlstm_pipe_kernel(x_ref, wqt_ref, wk_ref, wvt_ref, wot_ref, wg_ref, bg_ref, gn_ref, wout_ref,
                       lg_ref, lb_ref, y_ref, cout_ref, mout_ref,
                       xb_s, qt_s, k_s, vt_s, ot_s, g_s, ht_s, hn_s, c_s, m_s, *, alpha, tt, n_j):
    j = pl.program_id(1)
    L = P_L
    H = M_HEADS
    hv = H * M_DV
    d = x_ref.shape[1]
    section = P_SECTION
    n_sec = tt // section
    units = tt // L
    rows_of = lambda sec: slice(sec * section, (sec + 1) * section)

    @pl.when(j == 0)
    def _():
        c_s[...] = jnp.zeros_like(c_s)
        m_s[...] = jnp.zeros_like(m_s)

    row_i = lax.broadcasted_iota(jnp.int32, (L, LANES), 0)
    lane_i = lax.broadcasted_iota(jnp.int32, (L, LANES), 1)
    tri = (row_i >= lane_i).astype(F32)
    key8 = lax.broadcasted_iota(jnp.int32, (L, H * L), 0)
    qry8 = lax.broadcasted_iota(jnp.int32, (L, H * L), 1) & (L - 1)
    causal8 = key8 <= qry8
    ones_rows = jnp.ones((P_AUG - M_DV, L), F32)
    gn = gn_ref[...]

    zeros_dk = jnp.zeros((M_DK, L), BF16)

    def k_pair(h, ts):
        return k_s[ts, (h // 2) * LANES:(h // 2 + 1) * LANES]

    def pad_head(h, a):
        return jnp.concatenate([a, zeros_dk] if h % 2 == 0 else [zeros_dk, a], axis=0)

    def proj_pieces(sec):
        rs = rows_of(sec)

        def q_and_gates():
            xb_s[rs, :] = x_ref[rs, :].astype(BF16)
            qt_s[:, rs] = _dot_nt(wqt_ref[...], xb_s[rs, :]).astype(BF16)
            g_s[rs, :] = _dot(xb_s[rs, :], wg_ref[...]) + bg_ref[...]

        def keys():
            k_s[rs, :] = _dot(xb_s[rs, :], wk_ref[...]).astype(BF16)

        def vt_block(cb):
            vt_s[cb:cb + A_COL_BLOCK, rs] = _dot_nt(wvt_ref[cb:cb + A_COL_BLOCK, :], xb_s[rs, :])

        def ot_block(cb):
            ot_s[cb:cb + A_COL_BLOCK, rs] = _dot_nt(wot_ref[cb:cb + A_COL_BLOCK, :], xb_s[rs, :])

        blocks = range(0, hv, A_COL_BLOCK)
        return ([q_and_gates, keys],
                [functools.partial(f, cb) for f in (vt_block, ot_block) for cb in blocks])

    def post_pieces(sec):
        rs = rows_of(sec)

        def norm_gate(heads):
            for h in heads:
                hs = slice(h * M_DV, (h + 1) * M_DV)
                hh = ht_s[hs, rs]
                scale = lax.rsqrt(jnp.mean(hh * hh, axis=0, keepdims=True) + HEAD_NORM_EPS)
                gcol = jnp.concatenate([gn[hs, :]] * (section // LANES), axis=1)
                hn_s[hs, rs] = (hh * scale * gcol * jax.nn.sigmoid(ot_s[hs, rs])).astype(BF16)

        def out_block(i):
            blk = slice(i * (d // 4), (i + 1) * (d // 4))
            y_ref[rs, blk] = _dot_tn(hn_s[:, rs], wout_ref[:, blk])

        def deep_norm():
            y_ref[rs, :] = _layer_norm(alpha * x_ref[rs, :] + y_ref[rs, :], lg_ref[...], lb_ref[...])

        return ([functools.partial(norm_gate, range(0, H // 2)), functools.partial(norm_gate, range(H // 2, H)),
                 functools.partial(out_block, 0), functools.partial(out_block, 1)],
                [functools.partial(out_block, 2), functools.partial(out_block, 3), deep_norm])

    def pre(u):
        ts = slice(u * L, (u + 1) * L)
        gi = g_s[ts, :]
        gf = pltpu.roll(gi, LANES - P_FGATE_LANE, 1)
        lf = jnp.minimum(gf, 0.0) - jnp.log1p(jnp.exp(-jnp.abs(gf)))
        bc = lax.dot_general(tri, lf, (((1,), (0,)), ((), ())), precision=HIGHEST,
                             preferred_element_type=F32)
        rs = gi - bc
        cm = rs
        k = 1
        while k < L:
            cm = jnp.maximum(cm, jnp.where(row_i >= k, pltpu.roll(cm, k, 0), NEG_INF))
            k *= 2
        packed = jnp.where(lane_i < H, rs, jnp.where(lane_i < 2 * H, cm, bc))
        rows = packed.T[0:3 * H]
        rs_cat = jnp.concatenate([jnp.broadcast_to(rs[:, h:h + 1], (L, L)) for h in range(H)], axis=1)
        e_cat = jnp.where(causal8, rs_cat, NEG_INF)
        s_cat = jnp.concatenate(
            [_dot(k_pair(h, ts), jnp.concatenate([pad_head(h + i, qt_s[(h + i) * M_DK:(h + i + 1) * M_DK, ts])
                                                  for i in range(2)], axis=1))
             for h in range(0, H, 2)], axis=1)
        return rows, e_cat, s_cat

    def rec(u, pre_u, fillers):
        ts = slice(u * L, (u + 1) * L)
        rows, e_cat, s_cat = pre_u
        rs_t = rows[0:H]
        cm_t = rows[H:2 * H]
        b_t = rows[2 * H:3 * H]
        m0 = m_s[...]
        a_t = jnp.maximum(m0, cm_t)
        w_inter = jnp.exp(m0 - a_t)
        emx = jnp.exp(-(b_t + a_t))
        m_new = jnp.broadcast_to((b_t + a_t)[:, L - 1:L], (H, L))
        b_last = jnp.broadcast_to(b_t[:, L - 1:L], (H, L))
        w_end = jnp.exp(rs_t + (b_last - m_new))
        decay = jnp.exp(b_last + m0 - m_new)
        m_s[...] = m_new
        a_cat = jnp.concatenate([a_t[h:h + 1, :] for h in range(H)], axis=1)
        p_cat = (s_cat * jnp.exp(e_cat - a_cat)).astype(BF16)
        for h in range(H):
            if fillers:
                fillers.pop(0)()
            va = jnp.concatenate([vt_s[h * M_DV:(h + 1) * M_DV, ts], ones_rows], axis=0)
            ct = c_s[h]
            qw = (qt_s[h * M_DK:(h + 1) * M_DK, ts].astype(F32) * w_inter[h:h + 1, :]).astype(BF16)
            lhs = jnp.concatenate([va.astype(BF16), ct.astype(BF16)], axis=1)
            rhs = jnp.concatenate([p_cat[:, h * L:(h + 1) * L], pad_head(h, qw)], axis=0)
            nd = _dot(lhs, rhs)
            inv = 1.0 / jnp.maximum(jnp.abs(nd[M_DV:M_DV + 1, :]), emx[h:h + 1, :])
            ht_s[h * M_DV:(h + 1) * M_DV, ts] = nd[0:M_DV, :] * inv
            upd = _dot((va * w_end[h:h + 1, :]).astype(BF16), k_pair(h, ts))
            c_s[h] = decay[h:h + 1, :] * ct + upd
        while fillers:
            fillers.pop(0)()

    first, rest = proj_pieces(0)
    for piece in first + rest:
        piece()
    pre_next = pre(0)
    for u in range(units):
        sec, slot = divmod(u, section // L)
        fillers = []
        if sec + 1 < n_sec:
            fillers += proj_pieces(sec + 1)[slot]
        if sec >= 1:
            fillers += post_pieces(sec - 1)[slot]
        pre_cur = pre_next
        if u + 1 < units:
            pre_next = pre(u + 1)
        rec(u, pre_cur, fillers)
    for half in post_pieces(n_sec - 1):
        for piece in half:
            piece()

    @pl.when(j == n_j - 1)
    def _():
        cout_ref[0] = c_s[...]
        mout_ref[0] = m_s[...]


def _mlstm_prompt_layer(x2d, w_in, b_gate, g_norm, w_out, ln_g, ln_b, *, alpha, batch, seq, tt):
    n, d = x2d.shape
    assert n == batch * seq and seq % tt == 0 and tt % P_SECTION == 0 and P_SECTION == 2 * P_L
    assert P_GATE_COPIES * M_HEADS <= P_FGATE_LANE <= LANES - P_GATE_COPIES * M_HEADS and d % 4 == 0
    n_j = seq // tt
    hk = M_HEADS * M_DK
    hv = M_HEADS * M_DV
    wqt = w_in[:, 0:hk].T.astype(BF16)
    wk = (w_in[:, hk:2 * hk] * (M_DK ** -0.5)).astype(BF16)
    wvt = w_in[:, 2 * hk:2 * hk + hv].T.astype(BF16)
    wot = w_in[:, 2 * hk + hv:2 * hk + 2 * hv].T.astype(BF16)
    rep = lambda a, width: jnp.pad(jnp.tile(a, (1, P_GATE_COPIES)), ((0, 0), (0, width - P_GATE_COPIES * M_HEADS)))
    both = lambda i, f: jnp.concatenate([rep(i, P_FGATE_LANE), rep(f, LANES - P_FGATE_LANE)], axis=1)
    g0 = 2 * hk + 2 * hv
    wg = both(w_in[:, g0:g0 + M_HEADS], w_in[:, g0 + M_HEADS:]).astype(BF16)
    bg = both(b_gate[None, 0:M_HEADS], b_gate[None, M_HEADS:]).astype(F32)
    gn = jnp.broadcast_to(g_norm.astype(F32)[:, None], (hv, LANES))
    wout = w_out.astype(BF16)
    consts = (wqt, wk, wvt, wot, wg, bg, gn, wout, ln_g, ln_b)
    y, c_out, m_out = pl.pallas_call(
        functools.partial(_mlstm_pipe_kernel, alpha=alpha, tt=tt, n_j=n_j),
        grid=(batch, n_j),
        in_specs=[pl.BlockSpec((tt, d), lambda b, j: (b * n_j + j, 0))] + [_const_spec(a.shape) for a in consts],
        out_specs=[
            pl.BlockSpec((tt, d), lambda b, j: (b * n_j + j, 0)),
            pl.BlockSpec((1, M_HEADS, P_AUG, 2 * M_DK), lambda b, j: (b, 0, 0, 0)),
            pl.BlockSpec((1, M_HEADS, LANES), lambda b, j: (b, 0, 0)),
        ],
        out_shape=[
            jax.ShapeDtypeStruct((n, d), F32),
            jax.ShapeDtypeStruct((batch, M_HEADS, P_AUG, 2 * M_DK), F32),
            jax.ShapeDtypeStruct((batch, M_HEADS, LANES), F32),
        ],
        scratch_shapes=[
            pltpu.VMEM((tt, d), BF16),
            pltpu.VMEM((hk, tt), BF16),
            pltpu.VMEM((tt, hk), BF16),
            pltpu.VMEM((hv, tt), F32),
            pltpu.VMEM((hv, tt), F32),
            pltpu.VMEM((tt, LANES), F32),
            pltpu.VMEM((hv, tt), F32),
            pltpu.VMEM((hv, tt), BF16),
            pltpu.VMEM((M_HEADS, P_AUG, 2 * M_DK), F32),
            pltpu.VMEM((M_HEADS, LANES), F32),
        ],
        compiler_params=pltpu.CompilerParams(
            dimension_semantics=("arbitrary", "arbitrary"), vmem_limit_bytes=VMEM_LIMIT_BYTES),
        name="mlstm_prompt",
    )(x2d, *consts)
    ct = jnp.stack([c_out[:, 0::2, :, 0:M_DK], c_out[:, 1::2, :, M_DK:]], axis=2)
    ct = ct.reshape(batch, M_HEADS, P_AUG, M_DK)
    c_new = jnp.swapaxes(ct[:, :, 0:M_DV, :], -1, -2)
    n_new = ct[:, :, M_DV, :]
    m_new = m_out[:, :, 0]
    return y, c_new, n_new, m_new


PAIRS = GROUP // 2


def _attn_units(units):
    halves = [(q, kv[i], kv[2 + i], bias[i], sink[i]) for q, kv, bias, sink in units for i in range(2)]
    ss = [_dot_nt(q, k_op) + bias for q, k_op, _, bias, _ in halves]
    mxs = [jnp.maximum(jnp.max(s, axis=-1, keepdims=True), hf[4]) for s, hf in zip(ss, halves)]
    ps = [jnp.exp(s - mx) for s, mx in zip(ss, mxs)]
    dens = [jnp.sum(p, axis=-1, keepdims=True) + jnp.exp(hf[4] - mx) for p, mx, hf in zip(ps, mxs, halves)]
    outs = [_dot(p.astype(BF16), hf[2]) / den for p, den, hf in zip(ps, dens, halves)]
    return [outs[2 * i] + outs[2 * i + 1] for i in range(len(units))]


def _split_kv(kv):
    lane = lax.broadcasted_iota(jnp.int32, (1, LANES), 1)
    low = lane < HEAD_DIM
    res = []
    kk = kv[:, 0:LANES]
    vv = kv[:, LANES:2 * LANES]
    kk_r = pltpu.roll(kk, HEAD_DIM, 1)
    vv_r = pltpu.roll(vv, HEAD_DIM, 1)
    z = jnp.zeros_like(kk)
    res.append((jnp.where(low, kk, z), jnp.where(low, z, kk_r), jnp.where(low, vv, z), jnp.where(low, z, vv_r)))
    res.append((jnp.where(low, kk_r, z), jnp.where(low, z, kk), jnp.where(low, vv_r, z), jnp.where(low, z, vv)))
    return [tuple(a.astype(BF16) for a in grp) for grp in res]


SWA_UNIT = 2 * CHUNK
SWA_KEYS = WINDOW + SWA_UNIT
SWA_SECTION = 256


def _swa_prompt_kernel(x_ref, kc_ref, kp_ref, vtc_ref, vtp_ref, wqt_ref, wo_ref, bias_ref, sink_ref,
                       lg_ref, lb_ref, y_ref, xb_s, qt_s, o_s, k_s, vt_s, *, alpha, tq, section):
    j = pl.program_id(1)
    k_s[0:WINDOW, :] = kp_ref[...].astype(BF16)
    k_s[WINDOW:WINDOW + tq, :] = kc_ref[...].astype(BF16)
    vt_s[:, 0:WINDOW] = vtp_ref[...]
    vt_s[:, WINDOW:WINDOW + tq] = vtc_ref[...]
    zeros = jnp.zeros((HEAD_DIM, GROUP * SWA_UNIT), BF16)
    key_i = lax.broadcasted_iota(jnp.int32, (SWA_KEYS, GROUP * SWA_UNIT), 0)
    d = x_ref.shape[1]
    n_sec = tq // section
    per_sec = (section // SWA_UNIT) * N_KV_HEADS
    rows_of = lambda sec: slice(sec * section, (sec + 1) * section)

    def project(sec, piece):
        rs = rows_of(sec)
        if piece == 0:
            xb_s[rs, :] = x_ref[rs, :].astype(BF16)
        blk = slice(piece * (d // per_sec), (piece + 1) * (d // per_sec))
        qt_s[blk, rs] = _dot_nt(wqt_ref[blk, :], xb_s[rs, :]).astype(BF16)

    def out_project(sec, piece):
        rs = rows_of(sec)
        blk = slice(piece * (d // per_sec), (piece + 1) * (d // per_sec))
        y_ref[rs, blk] = _dot(o_s[rs, :], wo_ref[:, blk])

    def normalise(sec):
        rs = rows_of(sec)
        y_ref[rs, :] = _layer_norm(alpha * x_ref[rs, :] + y_ref[rs, :], lg_ref[...], lb_ref[...])

    def scores(u, g):
        r0 = u * SWA_UNIT
        qt_g = jnp.concatenate(
            [qt_s[h * HEAD_DIM:(h + 1) * HEAD_DIM, r0:r0 + SWA_UNIT] for h in range(g * GROUP, (g + 1) * GROUP)],
            axis=1)
        qz = jnp.concatenate([qt_g, zeros] if g == 0 else [zeros, qt_g], axis=0)
        s_t = _dot(k_s[r0:r0 + SWA_KEYS, :], qz) + bias_ref[g]
        if r0 < WINDOW:
            s_t = jnp.where(j * tq + r0 - WINDOW + key_i >= 0, s_t, NEG_INF)
        return s_t

    def finish(u, g, s_t):
        r0 = u * SWA_UNIT
        sink = sink_ref[g]
        mx = jnp.maximum(jnp.max(s_t, axis=0, keepdims=True), sink)
        p = jnp.exp2(s_t - mx)
        den = jnp.sum(p, axis=0, keepdims=True) + jnp.exp2(sink - mx)
        vt_g = vt_s[g * HEAD_DIM:(g + 1) * HEAD_DIM, r0:r0 + SWA_KEYS]
        o_t = (_dot(vt_g, p.astype(BF16)) / den).astype(BF16)
        for i in range(0, GROUP, 2):
            pair = jnp.concatenate([o_t[:, i * SWA_UNIT:(i + 1) * SWA_UNIT],
                                    o_t[:, (i + 1) * SWA_UNIT:(i + 2) * SWA_UNIT]], axis=0)
            c0 = (g * GROUP + i) * HEAD_DIM
            o_s[r0:r0 + SWA_UNIT, c0:c0 + 2 * HEAD_DIM] = pair.T

    todo = [(u, g) for u in range(tq // SWA_UNIT) for g in range(N_KV_HEADS)]
    for piece in range(per_sec):
        project(0, piece)
    s_next = scores(*todo[0])
    for idx, (u, g) in enumerate(todo):
        sec, slot = divmod(idx, per_sec)
        if sec + 1 < n_sec:
            project(sec + 1, slot)
        if sec >= 1:
            out_project(sec - 1, slot)
        s_cur = s_next
        if idx + 1 < len(todo):
            s_next = scores(*todo[idx + 1])
        finish(u, g, s_cur)
        if sec >= 1 and slot == per_sec - 1:
            normalise(sec - 1)
    for piece in range(per_sec):
        out_project(n_sec - 1, piece)
    normalise(n_sec - 1)


def _swa_prompt_layer(x2d, kv2d, vt, wqt, wo, bias, sinks, ln_g, ln_b, *, alpha, batch, seq, tq):
    n, d = x2d.shape
    section = SWA_SECTION
    assert n == batch * seq and seq % tq == 0 and tq % section == 0 and section % SWA_UNIT == 0 and WINDOW == LANES
    assert d % ((section // SWA_UNIT) * N_KV_HEADS) == 0
    n_j = seq // tq
    per = tq // WINDOW
    prev = lambda b, j: jnp.maximum((b * n_j + j) * per - 1, 0)
    return pl.pallas_call(
        functools.partial(_swa_prompt_kernel, alpha=alpha, tq=tq, section=section),
        grid=(batch, n_j),
        in_specs=[
            pl.BlockSpec((tq, d), lambda b, j: (b * n_j + j, 0)),
            pl.BlockSpec((tq, LANES), lambda b, j: (b * n_j + j, 0)),
            pl.BlockSpec((WINDOW, LANES), lambda b, j: (prev(b, j), 0)),
            pl.BlockSpec((LANES, tq), lambda b, j: (0, b * n_j + j)),
            pl.BlockSpec((LANES, WINDOW), lambda b, j: (0, prev(b, j))),
            _const_spec(wqt.shape),
            _const_spec(wo.shape),
            _const_spec(bias.shape),
            _const_spec(sinks.shape),
            _const_spec(ln_g.shape),
            _const_spec(ln_b.shape),
        ],
        out_specs=pl.BlockSpec((tq, d), lambda b, j: (b * n_j + j, 0)),
        out_shape=jax.ShapeDtypeStruct((n, d), F32),
        scratch_shapes=[
            pltpu.VMEM((tq, d), BF16),
            pltpu.VMEM((d, tq), BF16),
            pltpu.VMEM((tq, d), BF16),
            pltpu.VMEM((WINDOW + tq, LANES), BF16),
            pltpu.VMEM((LANES, WINDOW + tq), BF16),
        ],
        compiler_params=pltpu.CompilerParams(
            dimension_semantics=("arbitrary", "arbitrary"), vmem_limit_bytes=VMEM_LIMIT_BYTES),
        name="swa_prompt",
    )(x2d, kv2d, kv2d, vt, vt, wqt, wo, bias, sinks, ln_g, ln_b)


def _swa_prompt_tables(sinks):
    slopes = jnp.exp2(-8.0 * jnp.arange(1, N_Q_HEADS + 1, dtype=F32) / N_Q_HEADS)
    q = jnp.arange(SWA_UNIT)[None, :]
    kx = jnp.arange(SWA_KEYS)[:, None]
    dist = jnp.abs(q + WINDOW - kx).astype(F32)
    first = q < CHUNK
    visible = (first & (kx < WINDOW + CHUNK)) | (~first & (kx >= CHUNK))
    bias = jnp.where(visible[None], -(LOG2_E * slopes)[:, None, None] * dist[None], NEG_INF)
    sinks = sinks.astype(F32) * LOG2_E
    bias = bias.reshape(N_KV_HEADS, GROUP, SWA_KEYS, SWA_UNIT).transpose(0, 2, 1, 3)
    sink = jnp.broadcast_to(sinks.astype(F32).reshape(N_KV_HEADS, 1, GROUP, 1), (N_KV_HEADS, 1, GROUP, SWA_UNIT))
    return (bias.reshape(N_KV_HEADS, SWA_KEYS, GROUP * SWA_UNIT), sink.reshape(N_KV_HEADS, 1, GROUP * SWA_UNIT))


def _swa_sample_kernel(x_ref, kvn_ref, kc_ref, vc_ref, wq_ref, wo_ref, bias_ref, sink_ref, lg_ref, lb_ref,
                       y_ref, q_s, o_s, *, alpha, batch, seq):
    x = x_ref[...]
    xb = x.astype(BF16)
    q_s[...] = _dot(xb, wq_ref[...]).astype(BF16)
    per_pass = 2
    for b0 in range(0, batch, per_pass):
        units = []
        for b in range(b0, min(b0 + per_pass, batch)):
            r0 = b * seq
            kv_old = jnp.concatenate([kc_ref[b], vc_ref[b]], axis=1)
            groups = _split_kv(jnp.concatenate([kv_old, kvn_ref[r0:r0 + seq, :]], axis=0))
            for g in range(N_KV_HEADS):
                q_rows = jnp.concatenate(
                    [q_s[r0:r0 + seq, (g * PAIRS + pp) * LANES:(g * PAIRS + pp + 1) * LANES] for pp in range(PAIRS)],
                    axis=0)
                units.append((q_rows, groups[g], (bias_ref[g, 0], bias_ref[g, 1]), (sink_ref[g, 0], sink_ref[g, 1])))
        outs = _attn_units(units)
        for i, o in enumerate(outs):
            r0 = (b0 + i // N_KV_HEADS) * seq
            g = i % N_KV_HEADS
            for pp in range(PAIRS):
                o_s[r0:r0 + seq, (g * PAIRS + pp) * LANES:(g * PAIRS + pp + 1) * LANES] = (
                    o[pp * seq:(pp + 1) * seq, :].astype(BF16))
    mix = _dot(o_s[...], wo_ref[...])
    y_ref[...] = _layer_norm(alpha * x + mix, lg_ref[...], lb_ref[...])


def _swa_sample_layer(x2d, kv_new, k_cache, v_cache, wq, wo, bias, sink, ln_g, ln_b, *, alpha, batch, seq):
    n, d = x2d.shape
    assert n == batch * seq
    args = (x2d, kv_new, k_cache, v_cache, wq, wo, bias, sink, ln_g, ln_b)
    return pl.pallas_call(
        functools.partial(_swa_sample_kernel, alpha=alpha, batch=batch, seq=seq),
        grid=(1,),
        in_specs=[_const_spec(a.shape) for a in args],
        out_specs=pl.BlockSpec((n, d), lambda i: (0, 0)),
        out_shape=jax.ShapeDtypeStruct((n, d), F32),
        scratch_shapes=[pltpu.VMEM((n, d), BF16), pltpu.VMEM((n, d), BF16)],
        compiler_params=pltpu.CompilerParams(
            dimension_semantics=("arbitrary",), vmem_limit_bytes=VMEM_LIMIT_BYTES),
        name="swa_sample",
    )(*args)


def _attn_tables(sinks, ql, nk):
    slopes = jnp.exp2(-8.0 * jnp.arange(1, N_Q_HEADS + 1, dtype=F32) / N_Q_HEADS)
    dist = jnp.abs(jnp.arange(ql)[:, None] - jnp.arange(nk)[None, :] + (nk - ql)).astype(F32)
    head = (jnp.arange(N_KV_HEADS)[:, None, None] * GROUP + 2 * jnp.arange(PAIRS)[None, None, :]
            + jnp.arange(2)[None, :, None])
    bias = -slopes[head][..., None, None] * dist
    sink = jnp.broadcast_to(sinks.astype(F32)[head][..., None, None], head.shape + (ql, 1))
    return (bias.reshape(N_KV_HEADS, 2, PAIRS * ql, nk), sink.reshape(N_KV_HEADS, 2, PAIRS * ql, 1))


class _Tiles(NamedTuple):
    rows: int
    mixer: int
    sample_batch: int


def _tiles(batch, seq):
    n = batch * seq
    return _Tiles(rows=min(512, n), mixer=min(1024, seq), sample_batch=min(8, batch))


def _trunk(x, c0s, n0s, m0s, k_cache, v_cache, params, *, is_prompt, kv_rows):
    (w_in_a, b_gate_a, g_norm_a, w_out_a, w_kv, w_q_b, sinks_b, w_out_b, w_gu, w_down, ln_g, ln_b) = params
    batch, seq, d = x.shape
    depth = w_gu.shape[0]
    n_a = w_in_a.shape[0]
    alpha = (2 * depth) ** 0.25
    n = batch * seq
    x2d = x.reshape(n, d)
    tiles = _tiles(batch, seq)
    tm = tiles.rows
    L, bb_n, tt = seq, tiles.sample_batch, seq
    cs, ns, ms = [], [], []
    kv2d = None
    for layer in range(depth):
        row = lambda a: a.astype(F32)[None, :]
        if layer < n_a and is_prompt:
            x2d, c, nn, m = _mlstm_prompt_layer(
                x2d, w_in_a[layer], b_gate_a[layer], g_norm_a[layer], w_out_a[layer],
                row(ln_g[layer, 0]), row(ln_b[layer, 0]), alpha=alpha, batch=batch, seq=seq, tt=tiles.mixer)
            cs.append(c)
            ns.append(nn)
            ms.append(m)
        elif layer < n_a:
            w_all, bg, gn, wo = _prep_mlstm_weights(w_in_a[layer], b_gate_a[layer], g_norm_a[layer], w_out_a[layer])
            m0 = jnp.broadcast_to(m0s[layer].astype(F32)[:, :, None, None], (batch, M_HEADS, 8, LANES))
            x2d, c, nn, m = _mlstm_layer(
                x2d, c0s[layer].astype(F32), n0s[layer].astype(F32)[..., None], m0, w_all, bg, gn, wo,
                row(ln_g[layer, 0]), row(ln_b[layer, 0]), alpha=alpha, batch=batch, seq=seq, L=L, bb_n=bb_n, tt=tt)
            cs.append(c)
            ns.append(nn)
            ms.append(m)
        else:
            jb = layer - n_a
            if kv2d is None:
                kv2d, vt = _kv_proj(x2d, w_kv, tm=tm, with_vt=is_prompt)
            wq = (w_q_b[jb] * (HEAD_DIM ** -0.5)).astype(BF16)
            wo = w_out_b[jb].astype(BF16)
            if is_prompt:
                bias, sink = _swa_prompt_tables(sinks_b[jb])
                wqt = (w_q_b[jb] * (HEAD_DIM ** -0.5 * LOG2_E)).T.astype(BF16)
                x2d = _swa_prompt_layer(x2d, kv2d, vt, wqt, wo, bias, sink,
                                        row(ln_g[layer, 0]), row(ln_b[layer, 0]),
                                        alpha=alpha, batch=batch, seq=seq, tq=tiles.mixer)
            else:
                w_rows = k_cache.shape[1]
                bias, sink = _attn_tables(sinks_b[jb], seq, w_rows + seq)
                x2d = _swa_sample_layer(
                    x2d, kv2d, k_cache.astype(F32).reshape(batch, w_rows, N_KV_HEADS * HEAD_DIM),
                    v_cache.astype(F32).reshape(batch, w_rows, N_KV_HEADS * HEAD_DIM), wq, wo, bias, sink,
                    row(ln_g[layer, 0]), row(ln_b[layer, 0]), alpha=alpha, batch=batch, seq=seq)
        feeds_kv = layer + 1 == n_a and n_a < depth
        x2d, kv_new, vt_new = _ffn(
            x2d, w_gu, w_down, row(ln_g[layer, 1]), row(ln_b[layer, 1]), layer=layer, alpha=alpha,
            tm=min(2 * tm, n), sub=tm, w_kv=w_kv if feeds_kv else None, with_vt=feeds_kv and is_prompt)
        if feeds_kv:
            kv2d, vt = kv_new, vt_new
    kv = kv2d.reshape(batch, seq, kv2d.shape[1])[:, seq - kv_rows:].reshape(batch, kv_rows, 2, N_KV_HEADS, HEAD_DIM)
    return (x2d.reshape(batch, seq, d), jnp.stack(cs), jnp.stack(ns), jnp.stack(ms), kv[:, :, 0], kv[:, :, 1])


def kernel(x_prompt, x_sample, state_C, state_n, state_m, cache_k, cache_v, w_in_a, b_gate_a, g_norm_a,
           w_out_a, w_kv, w_q_b, sinks_b, w_out_b, w_gu, w_down, ln_g, ln_b):
    params = (w_in_a, b_gate_a, g_norm_a, w_out_a, w_kv, w_q_b, sinks_b, w_out_b,
              w_gu.astype(BF16), w_down.astype(BF16), ln_g, ln_b)
    y_p, p_c, p_n, p_m, p_k, p_v = _trunk(x_prompt, None, None, None, None, None, params, is_prompt=True,
                                          kv_rows=min(WINDOW, x_prompt.shape[1]))
    y_s, s_c, s_n, s_m, s_k, s_v = _trunk(x_sample, state_C, state_n, state_m, cache_k, cache_v, params,
                                          is_prompt=False, kv_rows=x_sample.shape[1])
    return (y_p, y_s, p_c, p_n, p_m, p_k, p_v, s_c, s_n, s_m, s_k, s_v)
```

```python
import functools
from typing import NamedTuple

import jax
import jax.numpy as jnp
from jax import lax
from jax.experimental import pallas as pl
from jax.experimental.pallas import tpu as pltpu

F32 = jnp.float32
BF16 = jnp.bfloat16

CHUNK = 64
WINDOW = 128
M_HEADS = 8
M_DK = 64
M_DV = 128
N_Q_HEADS = 16
N_KV_HEADS = 2
HEAD_DIM = 64
GROUP = N_Q_HEADS // N_KV_HEADS
LN_EPS = 1e-5
HEAD_NORM_EPS = 1e-6

LANES = 128
LOG2_E = 1.4426950408889634
VMEM_LIMIT_BYTES = 56 * 1024 * 1024

NEG_INF = float("-inf")
HIGHEST = lax.Precision.HIGHEST


def _const_spec(shape):
    nd = len(shape)
    return pl.BlockSpec(shape, lambda *_: (0,) * nd, pipeline_mode=pl.Buffered(1))


class _Row(NamedTuple):
    array: jax.Array
    index: int


def _spec_of(a):
    if isinstance(a, _Row):
        return pl.BlockSpec((None,) + tuple(a.array.shape[1:]), lambda *_: (a.index, 0, 0),
                            pipeline_mode=pl.Buffered(1))
    return _const_spec(a.shape)


def _arr(a):
    return a.array if isinstance(a, _Row) else a


def _layer_norm(y, g, b):
    mu = jnp.mean(y, axis=-1, keepdims=True)
    yc = y - mu
    var = jnp.mean(yc * yc, axis=-1, keepdims=True)
    return yc * lax.rsqrt(var + LN_EPS) * g + b


def _dot(a, b):
    return jnp.dot(a, b, preferred_element_type=F32)


def _dot_nt(a, b):
    return lax.dot_general(a, b, (((1,), (1,)), ((), ())), preferred_element_type=F32)


def _dot_tn(a, b):
    return lax.dot_general(a, b, (((0,), (0,)), ((), ())), preferred_element_type=F32)


FF_BLOCK = 256


def _ffn_kernel(*refs, alpha, d_ff, sub, with_kv, with_vt):
    x_ref, wgu_ref, wd_ref, g_ref, b_ref = refs[:5]
    rest = list(refs[5:])
    wkv_ref = rest.pop(0) if with_kv else None
    wvt_ref = rest.pop(0) if with_vt else None
    o_ref = rest.pop(0)
    kv_ref = rest.pop(0) if with_kv else None
    vt_ref = rest.pop(0) if with_vt else None
    acc_ref = rest.pop(0)
    h_ref = rest.pop(0)
    tm = x_ref.shape[0]

    n_fin = 4

    def finish(i, part):
        step = sub // n_fin
        rs = slice(i * sub + part * step, i * sub + (part + 1) * step)
        y = _layer_norm(alpha * x_ref[rs, :] + acc_ref[rs, :], g_ref[...], b_ref[...])
        o_ref[rs, :] = y
        if with_kv:
            yb = y.astype(BF16)
            kv_ref[rs, :] = _dot(yb, wkv_ref[...])
            if with_vt:
                vt_ref[:, rs] = _dot_nt(wvt_ref[...], yb).astype(BF16)

    for i in range(tm // sub):
        rs = slice(i * sub, (i + 1) * sub)
        xb = x_ref[rs, :].astype(BF16)
        for j in range(d_ff // FF_BLOCK):
            sl = slice(j * FF_BLOCK, (j + 1) * FF_BLOCK)
            su = slice(d_ff + j * FF_BLOCK, d_ff + (j + 1) * FF_BLOCK)
            g = _dot(xb, wgu_ref[:, sl])
            u = _dot(xb, wgu_ref[:, su])
            h_ref[rs, sl] = (g * jax.nn.sigmoid(g) * u).astype(BF16)
            if i > 0 and j % 2 == 0 and j // 2 < n_fin:
                finish(i - 1, j // 2)
        acc_ref[rs, :] = _dot(h_ref[rs, :], wd_ref[...])
    for part in range(n_fin):
        finish(tm // sub - 1, part)


def _layer_spec(shape, layer):
    nd = len(shape) - 1
    return pl.BlockSpec((None,) + tuple(shape[1:]), lambda *_: (layer,) + (0,) * nd, pipeline_mode=pl.Buffered(1))


def _ffn(x2d, w_gu, w_down, ln_g, ln_b, *, layer, alpha, tm, sub, w_kv=None, with_vt=False):
    n, d = x2d.shape
    d_ff = w_down.shape[1]
    assert n % tm == 0 and tm % sub == 0 and d_ff % FF_BLOCK == 0 and w_gu.shape[2] == 2 * d_ff
    with_kv = w_kv is not None
    assert with_kv or not with_vt
    args = [x2d, w_gu, w_down, _arr(ln_g), _arr(ln_b)]
    in_specs = [pl.BlockSpec((tm, d), lambda i: (i, 0)), _layer_spec(w_gu.shape, layer),
                _layer_spec(w_down.shape, layer), _spec_of(ln_g), _spec_of(ln_b)]
    out_specs = [pl.BlockSpec((tm, d), lambda i: (i, 0))]
    out_shape = [jax.ShapeDtypeStruct((n, d), F32)]
    if with_kv:
        nk = w_kv.shape[1]
        args.append(w_kv.astype(BF16))
        in_specs.append(_const_spec(w_kv.shape))
        out_specs.append(pl.BlockSpec((tm, nk), lambda i: (i, 0)))
        out_shape.append(jax.ShapeDtypeStruct((n, nk), F32))
        if with_vt:
            w_vt = w_kv[:, nk // 2:].T.astype(BF16)
            args.append(w_vt)
            in_specs.insert(len(in_specs), _const_spec(w_vt.shape))
            out_specs.append(pl.BlockSpec((nk // 2, tm), lambda i: (0, i)))
            out_shape.append(jax.ShapeDtypeStruct((nk // 2, n), BF16))
    outs = pl.pallas_call(
        functools.partial(_ffn_kernel, alpha=alpha, d_ff=d_ff, sub=sub, with_kv=with_kv, with_vt=with_vt),
        grid=(n // tm,),
        in_specs=in_specs,
        out_specs=out_specs,
        out_shape=out_shape,
        scratch_shapes=[pltpu.VMEM((tm, d), F32), pltpu.VMEM((tm, d_ff), BF16)],
        compiler_params=pltpu.CompilerParams(
            dimension_semantics=("arbitrary",), vmem_limit_bytes=VMEM_LIMIT_BYTES),
        name="ffn_kv" if with_kv else "ffn",
    )(*args)
    return outs[0], (outs[1] if with_kv else None), (outs[2] if with_vt else None)


def _kv_kernel(x_ref, w_ref, o_ref):
    o_ref[...] = _dot(x_ref[...].astype(BF16), w_ref[...])


def _kv_t_kernel(x_ref, w_ref, wvt_ref, o_ref, vt_ref):
    xb = x_ref[...].astype(BF16)
    o_ref[...] = _dot(xb, w_ref[...])
    vt_ref[...] = _dot_nt(wvt_ref[...], xb).astype(BF16)


def _kv_proj(x2d, w_kv, *, tm, with_vt):
    n, d = x2d.shape
    nk = w_kv.shape[1]
    assert n % tm == 0
    params = pltpu.CompilerParams(dimension_semantics=("arbitrary",), vmem_limit_bytes=VMEM_LIMIT_BYTES)
    x_spec = pl.BlockSpec((tm, d), lambda i: (i, 0))
    kv_spec = pl.BlockSpec((tm, nk), lambda i: (i, 0))
    w_kv_b = w_kv.astype(BF16)
    if not with_vt:
        kv = pl.pallas_call(
            _kv_kernel, grid=(n // tm,), in_specs=[x_spec, _const_spec(w_kv.shape)], out_specs=kv_spec,
            out_shape=jax.ShapeDtypeStruct((n, nk), F32), compiler_params=params, name="kv_proj",
        )(x2d, w_kv_b)
        return kv, None
    nv = nk // 2
    w_vt = w_kv[:, nv:].T.astype(BF16)
    return pl.pallas_call(
        _kv_t_kernel,
        grid=(n // tm,),
        in_specs=[x_spec, _const_spec(w_kv.shape), _const_spec(w_vt.shape)],
        out_specs=[kv_spec, pl.BlockSpec((nv, tm), lambda i: (0, i))],
        out_shape=[jax.ShapeDtypeStruct((n, nk), F32), jax.ShapeDtypeStruct((nv, n), BF16)],
        compiler_params=params,
        name="kv_proj_t",
    )(x2d, w_kv_b, w_vt)


A_QOFF = 0
A_KOFF = M_HEADS * LANES
A_VOFF = 2 * M_HEADS * LANES
A_OOFF = 3 * M_HEADS * LANES
A_GOFF = 4 * M_HEADS * LANES
A_COLS = A_GOFF + 2 * LANES
A_COL_BLOCK = 512


def _mlstm_kernel(x_ref, c0_ref, n0_ref, m0_ref, w_ref, bg_ref, gn_ref, wo_ref, lg_ref, lb_ref,
                  y_ref, cout_ref, mout_ref,
                  q_s, k_s, v_s, o_s, g_s, h_s, hb_s, c_s, m_s,
                  *, alpha, L, bb_n, tt, n_j):
    j = pl.program_id(1)
    rows = bb_n * tt
    units_per_b = tt // L
    hd = M_HEADS * LANES

    @pl.when(j == 0)
    def _():
        lane = lax.broadcasted_iota(jnp.int32, (1, 1, 1, LANES), 3)
        c_s[:, :, :, 0:LANES] = c0_ref[...]
        c_s[:, :, :, LANES:2 * LANES] = jnp.where(lane == 0, n0_ref[...], 0.0)
        m_s[...] = m0_ref[...]

    x = x_ref[...]
    xb = x.astype(BF16)
    for cb in range(0, hd, A_COL_BLOCK):
        q_s[:, cb:cb + A_COL_BLOCK] = _dot(xb, w_ref[:, A_QOFF + cb:A_QOFF + cb + A_COL_BLOCK]).astype(BF16)
        k_s[:, cb:cb + A_COL_BLOCK] = _dot(xb, w_ref[:, A_KOFF + cb:A_KOFF + cb + A_COL_BLOCK])
        v_s[:, cb:cb + A_COL_BLOCK] = _dot(xb, w_ref[:, A_VOFF + cb:A_VOFF + cb + A_COL_BLOCK]).astype(BF16)
        o_s[:, cb:cb + A_COL_BLOCK] = _dot(xb, w_ref[:, A_OOFF + cb:A_OOFF + cb + A_COL_BLOCK])
    g_s[...] = _dot(xb, w_ref[:, A_GOFF:A_GOFF + 2 * LANES]) + bg_ref[...]

    row_i = lax.broadcasted_iota(jnp.int32, (L, L), 0)
    col_i = lax.broadcasted_iota(jnp.int32, (L, L), 1)
    causal = row_i >= col_i
    tri = causal.astype(F32)
    ones_col = jnp.where(lax.broadcasted_iota(jnp.int32, (L, LANES), 1) == 0, 1.0, 0.0).astype(BF16)

    def unit(u, carry):
        r0 = pl.multiple_of(u * L, L)
        bb = u // units_per_b if bb_n > 1 else 0
        gi = g_s[pl.ds(r0, L), 0:LANES]
        gf = g_s[pl.ds(r0, L), LANES:2 * LANES]
        lf = jnp.minimum(gf, 0.0) - jnp.log1p(jnp.exp(-jnp.abs(gf)))
        bc = lax.dot_general(tri, lf, (((1,), (0,)), ((), ())), precision=HIGHEST,
                             preferred_element_type=F32)
        rs = gi - bc
        rs_t = rs.T
        b_last = bc[L - 1:L, :]
        heads = range(M_HEADS)
        hs_of = lambda h: slice(h * LANES, (h + 1) * LANES)
        qs = [q_s[pl.ds(r0, L), hs_of(h)] for h in heads]
        ks = [k_s[pl.ds(r0, L), hs_of(h)] for h in heads]
        vas = [jnp.concatenate([v_s[pl.ds(r0, L), hs_of(h)], ones_col], axis=1) for h in heads]
        m0s = [m_s[bb, h, 0:1, 0:1] for h in heads]
        cs = [c_s[bb, h] for h in heads]
        ss = [_dot_nt(qs[h], ks[h].astype(BF16)) for h in heads]
        qcs = [_dot(qs[h][:, 0:M_DK], cs[h].astype(BF16)) for h in heads]
        b_cols = [bc[:, h:h + 1] for h in heads]
        logds = [jnp.where(causal, b_cols[h] + rs_t[h:h + 1, :], NEG_INF) for h in heads]
        inters = [b_cols[h] + m0s[h] for h in heads]
        mxs = [jnp.maximum(inters[h], jnp.max(logds[h], axis=1, keepdims=True)) for h in heads]
        ps = [(ss[h] * jnp.exp(logds[h] - mxs[h])).astype(BF16) for h in heads]
        nds = [jnp.exp(inters[h] - mxs[h]) * qcs[h] + _dot(ps[h], vas[h]) for h in heads]
        for h in heads:
            num = nds[h][:, 0:M_DV]
            den = nds[h][:, M_DV:M_DV + 1]
            h_s[pl.ds(r0, L), hs_of(h)] = num / jnp.maximum(jnp.abs(den), jnp.exp(-mxs[h]))
        m_news = [mxs[h][L - 1:L, :] for h in heads]
        bls = [b_last[:, h:h + 1] for h in heads]
        wks = [(ks[h][:, 0:M_DK] * jnp.exp(rs[:, h:h + 1] + (bls[h] - m_news[h]))).astype(BF16) for h in heads]
        upds = [_dot_tn(wks[h], vas[h]) for h in heads]
        for h in heads:
            c_s[bb, h] = jnp.exp(bls[h] + m0s[h] - m_news[h]) * cs[h] + upds[h]
            m_s[bb, h] = jnp.broadcast_to(m_news[h], (8, LANES))
        return carry

    lax.fori_loop(0, rows // L, unit, 0)

    for h in range(M_HEADS):
        hs = slice(h * LANES, (h + 1) * LANES)
        hh = h_s[:, hs]
        hn = hh * lax.rsqrt(jnp.mean(hh * hh, axis=-1, keepdims=True) + HEAD_NORM_EPS)
        hb_s[:, hs] = (hn * gn_ref[:, hs] * jax.nn.sigmoid(o_s[:, hs])).astype(BF16)
    mix = _dot(hb_s[...], wo_ref[...])
    y_ref[...] = _layer_norm(alpha * x + mix, lg_ref[...], lb_ref[...])

    @pl.when(j == n_j - 1)
    def _():
        cout_ref[...] = c_s[...]
        mout_ref[...] = m_s[...]


def _mlstm_layer(x2d, c0, n0, m0, w_all, b_gate, g_norm, w_out, ln_g, ln_b, *, alpha, batch, seq, L, bb_n, tt):
    n, d = x2d.shape
    assert n == batch * seq and seq % tt == 0 and tt % L == 0 and batch % bb_n == 0
    assert bb_n == 1 or tt == seq
    n_j = seq // tt
    rows = bb_n * tt
    hd = M_HEADS * LANES
    kern = functools.partial(_mlstm_kernel, alpha=alpha, L=L, bb_n=bb_n, tt=tt, n_j=n_j)
    st4 = lambda bi, j: (bi, 0, 0, 0)
    y, c_out, m_out = pl.pallas_call(
        kern,
        grid=(batch // bb_n, n_j),
        in_specs=[
            pl.BlockSpec((rows, d), lambda bi, j: (bi * n_j + j, 0)),
            pl.BlockSpec((bb_n, M_HEADS, M_DK, M_DV), st4),
            pl.BlockSpec((bb_n, M_HEADS, M_DK, 1), st4),
            pl.BlockSpec((bb_n, M_HEADS, 8, LANES), st4),
            _const_spec(w_all.shape),
            _const_spec(b_gate.shape),
            _const_spec(g_norm.shape),
            _const_spec(w_out.shape),
            _spec_of(ln_g),
            _spec_of(ln_b),
        ],
        out_specs=[
            pl.BlockSpec((rows, d), lambda bi, j: (bi * n_j + j, 0)),
            pl.BlockSpec((bb_n, M_HEADS, M_DK, 2 * LANES), st4),
            pl.BlockSpec((bb_n, M_HEADS, 8, LANES), st4),
        ],
        out_shape=[
            jax.ShapeDtypeStruct((n, d), F32),
            jax.ShapeDtypeStruct((batch, M_HEADS, M_DK, 2 * LANES), F32),
            jax.ShapeDtypeStruct((batch, M_HEADS, 8, LANES), F32),
        ],
        scratch_shapes=[
            pltpu.VMEM((rows, hd), BF16),
            pltpu.VMEM((rows, hd), F32),
            pltpu.VMEM((rows, hd), BF16),
            pltpu.VMEM((rows, hd), F32),
            pltpu.VMEM((rows, 2 * LANES), F32),
            pltpu.VMEM((rows, hd), F32),
            pltpu.VMEM((rows, hd), BF16),
            pltpu.VMEM((bb_n, M_HEADS, M_DK, 2 * LANES), F32),
            pltpu.VMEM((bb_n, M_HEADS, 8, LANES), F32),
        ],
        compiler_params=pltpu.CompilerParams(
            dimension_semantics=("arbitrary", "arbitrary"), vmem_limit_bytes=VMEM_LIMIT_BYTES),
        name="mlstm_layer",
    )(x2d, c0, n0, m0, w_all, b_gate, g_norm, w_out, _arr(ln_g), _arr(ln_b))
    c_new = c_out[..., 0:M_DV]
    n_new = c_out[..., M_DV]
    m_new = m_out[:, :, 0, 0]
    return y, c_new, n_new, m_new


def _prep_mlstm_weights(w_in, b_gate, g_norm, w_out):
    d = w_in.shape[0]
    hk = M_HEADS * M_DK
    hv = M_HEADS * M_DV
    wq = w_in[:, 0:hk].reshape(d, M_HEADS, M_DK)
    wk = w_in[:, hk:2 * hk].reshape(d, M_HEADS, M_DK) * (M_DK ** -0.5)
    pad = ((0, 0), (0, 0), (0, LANES - M_DK))
    wq = jnp.pad(wq, pad).reshape(d, M_HEADS * LANES)
    wk = jnp.pad(wk, pad).reshape(d, M_HEADS * LANES)
    wv = w_in[:, 2 * hk:2 * hk + hv]
    wo = w_in[:, 2 * hk + hv:2 * hk + 2 * hv]
    wi = jnp.pad(w_in[:, 2 * hk + 2 * hv:2 * hk + 2 * hv + M_HEADS], ((0, 0), (0, LANES - M_HEADS)))
    wf = jnp.pad(w_in[:, 2 * hk + 2 * hv + M_HEADS:], ((0, 0), (0, LANES - M_HEADS)))
    w_all = jnp.concatenate([wq, wk, wv, wo, wi, wf], axis=1).astype(BF16)
    bg = jnp.concatenate([jnp.pad(b_gate[0:M_HEADS], (0, LANES - M_HEADS)),
                          jnp.pad(b_gate[M_HEADS:], (0, LANES - M_HEADS))]).astype(F32)[None, :]
    return w_all, bg, g_norm.astype(F32)[None, :], w_out.astype(BF16)


P_L = LANES
P_AUG = M_DV + 16
P_GATE_COPIES = 3
P_FGATE_LANE = 32


P_SECTION = 256


def _mlstm_pipe_kernel(x_ref, wqt_ref, wk_ref, wvt_ref, wot_ref, wg_ref, bg_ref, gn_ref, wout_ref,
                       lg_ref, lb_ref, y_ref, cout_ref, mout_ref,
                       xb_s, qt_s, k_s, vt_s, ot_s, g_s, ht_s, hn_s, c_s, m_s, *, alpha, tt, n_j):
    j = pl.program_id(1)
    L = P_L
    H = M_HEADS
    hv = H * M_DV
    d = x_ref.shape[1]
    section = P_SECTION
    n_sec = tt // section
    units = tt // L
    rows_of = lambda sec: slice(sec * section, (sec + 1) * section)

    @pl.when(j == 0)
    def _():
        c_s[...] = jnp.zeros_like(c_s)
        m_s[...] = jnp.zeros_like(m_s)

    row_i = lax.broadcasted_iota(jnp.int32, (L, LANES), 0)
    lane_i = lax.broadcasted_iota(jnp.int32, (L, LANES), 1)
    tri = (row_i >= lane_i).astype(F32)
    key8 = lax.broadcasted_iota(jnp.int32, (L, H * L), 0)
    qry8 = lax.broadcasted_iota(jnp.int32, (L, H * L), 1) & (L - 1)
    causal8 = key8 <= qry8
    ones_rows = jnp.ones((P_AUG - M_DV, L), F32)
    gn = gn_ref[...]

    zeros_dk = jnp.zeros((M_DK, L), BF16)

    def k_pair(h, ts):
        return k_s[ts, (h // 2) * LANES:(h // 2 + 1) * LANES]

    def pad_head(h, a):
        return jnp.concatenate([a, zeros_dk] if h % 2 == 0 else [zeros_dk, a], axis=0)

    def proj_pieces(sec):
        rs = rows_of(sec)

        def q_and_gates():
            xb_s[rs, :] = x_ref[rs, :].astype(BF16)
            qt_s[:, rs] = _dot_nt(wqt_ref[...], xb_s[rs, :]).astype(BF16)
            g_s[rs, :] = _dot(xb_s[rs, :], wg_ref[...]) + bg_ref[...]

        def keys():
            k_s[rs, :] = _dot(xb_s[rs, :], wk_ref[...]).astype(BF16)

        def vt_block(cb):
            vt_s[cb:cb + A_COL_BLOCK, rs] = _dot_nt(wvt_ref[cb:cb + A_COL_BLOCK, :], xb_s[rs, :])

        def ot_block(cb):
            ot_s[cb:cb + A_COL_BLOCK, rs] = _dot_nt(wot_ref[cb:cb + A_COL_BLOCK, :], xb_s[rs, :])

        blocks = range(0, hv, A_COL_BLOCK)
        return ([q_and_gates, keys],
                [functools.partial(f, cb) for f in (vt_block, ot_block) for cb in blocks])

    def post_pieces(sec):
        rs = rows_of(sec)

        def norm_gate(heads):
            for h in heads:
                hs = slice(h * M_DV, (h + 1) * M_DV)
                hh = ht_s[hs, rs]
                scale = lax.rsqrt(jnp.mean(hh * hh, axis=0, keepdims=True) + HEAD_NORM_EPS)
                gcol = jnp.concatenate([gn[hs, :]] * (section // LANES), axis=1)
                hn_s[hs, rs] = (hh * scale * gcol * jax.nn.sigmoid(ot_s[hs, rs])).astype(BF16)

        def out_block(i):
            blk = slice(i * (d // 4), (i + 1) * (d // 4))
            y_ref[rs, blk] = _dot_tn(hn_s[:, rs], wout_ref[:, blk])

        def deep_norm():
            y_ref[rs, :] = _layer_norm(alpha * x_ref[rs, :] + y_ref[rs, :], lg_ref[...], lb_ref[...])

        return ([functools.partial(norm_gate, range(0, H // 2)), functools.partial(norm_gate, range(H // 2, H)),
                 functools.partial(out_block, 0), functools.partial(out_block, 1)],
                [functools.partial(out_block, 2), functools.partial(out_block, 3), deep_norm])

    def pre(u):
        ts = slice(u * L, (u + 1) * L)
        gi = g_s[ts, :]
        gf = pltpu.roll(gi, LANES - P_FGATE_LANE, 1)
        lf = jnp.minimum(gf, 0.0) - jnp.log1p(jnp.exp(-jnp.abs(gf)))
        bc = lax.dot_general(tri, lf, (((1,), (0,)), ((), ())), precision=HIGHEST,
                             preferred_element_type=F32)
        rs = gi - bc
        cm = rs
        k = 1
        while k < L:
            cm = jnp.maximum(cm, jnp.where(row_i >= k, pltpu.roll(cm, k, 0), NEG_INF))
            k *= 2
        packed = jnp.where(lane_i < H, rs, jnp.where(lane_i < 2 * H, cm, bc))
        rows = packed.T[0:3 * H]
        rs_cat = jnp.concatenate([jnp.broadcast_to(rs[:, h:h + 1], (L, L)) for h in range(H)], axis=1)
        e_cat = jnp.where(causal8, rs_cat, NEG_INF)
        s_cat = jnp.concatenate(
            [_dot(k_pair(h, ts), jnp.concatenate([pad_head(h + i, qt_s[(h + i) * M_DK:(h + i + 1) * M_DK, ts])
                                                  for i in range(2)], axis=1))
             for h in range(0, H, 2)], axis=1)
        return rows, e_cat, s_cat

    def rec(u, pre_u, fillers):
        ts = slice(u * L, (u + 1) * L)
        rows, e_cat, s_cat = pre_u
        rs_t = rows[0:H]
        cm_t = rows[H:2 * H]
        b_t = rows[2 * H:3 * H]
        m0 = m_s[...]
        a_t = jnp.maximum(m0, cm_t)
        w_inter = jnp.exp(m0 - a_t)
        emx = jnp.exp(-(b_t + a_t))
        m_new = jnp.broadcast_to((b_t + a_t)[:, L - 1:L], (H, L))
        b_last = jnp.broadcast_to(b_t[:, L - 1:L], (H, L))
        w_end = jnp.exp(rs_t + (b_last - m_new))
        decay = jnp.exp(b_last + m0 - m_new)
        m_s[...] = m_new
        a_cat = jnp.concatenate([a_t[h:h + 1, :] for h in range(H)], axis=1)
        p_cat = (s_cat * jnp.exp(e_cat - a_cat)).astype(BF16)
        for h in range(H):
            if fillers:
                fillers.pop(0)()
            va = jnp.concatenate([vt_s[h * M_DV:(h + 1) * M_DV, ts], ones_rows], axis=0)
            ct = c_s[h]
            qw = (qt_s[h * M_DK:(h + 1) * M_DK, ts].astype(F32) * w_inter[h:h + 1, :]).astype(BF16)
            lhs = jnp.concatenate([va.astype(BF16), ct.astype(BF16)], axis=1)
            rhs = jnp.concatenate([p_cat[:, h * L:(h + 1) * L], pad_head(h, qw)], axis=0)
            nd = _dot(lhs, rhs)
            inv = 1.0 / jnp.maximum(jnp.abs(nd[M_DV:M_DV + 1, :]), emx[h:h + 1, :])
            ht_s[h * M_DV:(h + 1) * M_DV, ts] = nd[0:M_DV, :] * inv
            upd = _dot((va * w_end[h:h + 1, :]).astype(BF16), k_pair(h, ts))
            c_s[h] = decay[h:h + 1, :] * ct + upd
        while fillers:
            fillers.pop(0)()

    first, rest = proj_pieces(0)
    for piece in first + rest:
        piece()
    pre_next = pre(0)
    for u in range(units):
        sec, slot = divmod(u, section // L)
        fillers = []
        if sec + 1 < n_sec:
            fillers += proj_pieces(sec + 1)[slot]
        if sec >= 1:
            fillers += post_pieces(sec - 1)[slot]
        pre_cur = pre_next
        if u + 1 < units:
            pre_next = pre(u + 1)
        rec(u, pre_cur, fillers)
    for half in post_pieces(n_sec - 1):
        for piece in half:
            piece()

    @pl.when(j == n_j - 1)
    def _():
        cout_ref[0] = c_s[...]
        mout_ref[0] = m_s[...]


def _mlstm_prompt_layer(x2d, w_in, b_gate, g_norm, w_out, ln_g, ln_b, *, alpha, batch, seq, tt):
    n, d = x2d.shape
    assert n == batch * seq and seq % tt == 0 and tt % P_SECTION == 0 and P_SECTION == 2 * P_L
    assert P_GATE_COPIES * M_HEADS <= P_FGATE_LANE <= LANES - P_GATE_COPIES * M_HEADS and d % 4 == 0
    n_j = seq // tt
    hk = M_HEADS * M_DK
    hv = M_HEADS * M_DV
    wqt = w_in[:, 0:hk].T.astype(BF16)
    wk = (w_in[:, hk:2 * hk] * (M_DK ** -0.5)).astype(BF16)
    wvt = w_in[:, 2 * hk:2 * hk + hv].T.astype(BF16)
    wot = w_in[:, 2 * hk + hv:2 * hk + 2 * hv].T.astype(BF16)
    rep = lambda a, width: jnp.pad(jnp.tile(a, (1, P_GATE_COPIES)), ((0, 0), (0, width - P_GATE_COPIES * M_HEADS)))
    both = lambda i, f: jnp.concatenate([rep(i, P_FGATE_LANE), rep(f, LANES - P_FGATE_LANE)], axis=1)
    g0 = 2 * hk + 2 * hv
    wg = both(w_in[:, g0:g0 + M_HEADS], w_in[:, g0 + M_HEADS:]).astype(BF16)
    bg = both(b_gate[None, 0:M_HEADS], b_gate[None, M_HEADS:]).astype(F32)
    gn = jnp.broadcast_to(g_norm.astype(F32)[:, None], (hv, LANES))
    wout = w_out.astype(BF16)
    consts = (wqt, wk, wvt, wot, wg, bg, gn, wout, ln_g, ln_b)
    y, c_out, m_out = pl.pallas_call(
        functools.partial(_mlstm_pipe_kernel, alpha=alpha, tt=tt, n_j=n_j),
        grid=(batch, n_j),
        in_specs=[pl.BlockSpec((tt, d), lambda b, j: (b * n_j + j, 0))] + [_spec_of(a) for a in consts],
        out_specs=[
            pl.BlockSpec((tt, d), lambda b, j: (b * n_j + j, 0)),
            pl.BlockSpec((1, M_HEADS, P_AUG, 2 * M_DK), lambda b, j: (b, 0, 0, 0)),
            pl.BlockSpec((1, M_HEADS, LANES), lambda b, j: (b, 0, 0)),
        ],
        out_shape=[
            jax.ShapeDtypeStruct((n, d), F32),
            jax.ShapeDtypeStruct((batch, M_HEADS, P_AUG, 2 * M_DK), F32),
            jax.ShapeDtypeStruct((batch, M_HEADS, LANES), F32),
        ],
        scratch_shapes=[
            pltpu.VMEM((tt, d), BF16),
            pltpu.VMEM((hk, tt), BF16),
            pltpu.VMEM((tt, hk), BF16),
            pltpu.VMEM((hv, tt), F32),
            pltpu.VMEM((hv, tt), F32),
            pltpu.VMEM((tt, LANES), F32),
            pltpu.VMEM((hv, tt), F32),
            pltpu.VMEM((hv, tt), BF16),
            pltpu.VMEM((M_HEADS, P_AUG, 2 * M_DK), F32),
            pltpu.VMEM((M_HEADS, LANES), F32),
        ],
        compiler_params=pltpu.CompilerParams(
            dimension_semantics=("arbitrary", "arbitrary"), vmem_limit_bytes=VMEM_LIMIT_BYTES),
        name="mlstm_prompt",
    )(x2d, *[_arr(a) for a in consts])
    ct = jnp.stack([c_out[:, 0::2, :, 0:M_DK], c_out[:, 1::2, :, M_DK:]], axis=2)
    ct = ct.reshape(batch, M_HEADS, P_AUG, M_DK)
    c_new = jnp.swapaxes(ct[:, :, 0:M_DV, :], -1, -2)
    n_new = ct[:, :, M_DV, :]
    m_new = m_out[:, :, 0]
    return y, c_new, n_new, m_new


PAIRS = GROUP // 2


def _attn_units(units):
    halves = [(q, kv[i], kv[2 + i], bias[i], sink[i]) for q, kv, bias, sink in units for i in range(2)]
    ss = [_dot_nt(q, k_op) + bias for q, k_op, _, bias, _ in halves]
    mxs = [jnp.maximum(jnp.max(s, axis=-1, keepdims=True), hf[4]) for s, hf in zip(ss, halves)]
    ps = [jnp.exp(s - mx) for s, mx in zip(ss, mxs)]
    dens = [jnp.sum(p, axis=-1, keepdims=True) + jnp.exp(hf[4] - mx) for p, mx, hf in zip(ps, mxs, halves)]
    outs = [_dot(p.astype(BF16), hf[2]) / den for p, den, hf in zip(ps, dens, halves)]
    return [outs[2 * i] + outs[2 * i + 1] for i in range(len(units))]


def _split_kv(kv):
    lane = lax.broadcasted_iota(jnp.int32, (1, LANES), 1)
    low = lane < HEAD_DIM
    res = []
    kk = kv[:, 0:LANES]
    vv = kv[:, LANES:2 * LANES]
    kk_r = pltpu.roll(kk, HEAD_DIM, 1)
    vv_r = pltpu.roll(vv, HEAD_DIM, 1)
    z = jnp.zeros_like(kk)
    res.append((jnp.where(low, kk, z), jnp.where(low, z, kk_r), jnp.where(low, vv, z), jnp.where(low, z, vv_r)))
    res.append((jnp.where(low, kk_r, z), jnp.where(low, z, kk), jnp.where(low, vv_r, z), jnp.where(low, z, vv)))
    return [tuple(a.astype(BF16) for a in grp) for grp in res]


SWA_UNIT = 2 * CHUNK
SWA_KEYS = WINDOW + SWA_UNIT
SWA_SECTION = 256


def _swa_prompt_kernel(x_ref, kc_ref, kp_ref, vtc_ref, vtp_ref, wqt_ref, wo_ref, bias_ref, sink_ref,
                       lg_ref, lb_ref, y_ref, xb_s, qt_s, o_s, k_s, vt_s, *, alpha, tq, section):
    j = pl.program_id(1)
    k_s[0:WINDOW, :] = kp_ref[...].astype(BF16)
    k_s[WINDOW:WINDOW + tq, :] = kc_ref[...].astype(BF16)
    vt_s[:, 0:WINDOW] = vtp_ref[...]
    vt_s[:, WINDOW:WINDOW + tq] = vtc_ref[...]
    zeros = jnp.zeros((HEAD_DIM, GROUP * SWA_UNIT), BF16)
    key_i = lax.broadcasted_iota(jnp.int32, (SWA_KEYS, GROUP * SWA_UNIT), 0)
    d = x_ref.shape[1]
    n_sec = tq // section
    per_sec = (section // SWA_UNIT) * N_KV_HEADS
    rows_of = lambda sec: slice(sec * section, (sec + 1) * section)

    def project(sec, piece):
        rs = rows_of(sec)
        if piece == 0:
            xb_s[rs, :] = x_ref[rs, :].astype(BF16)
        blk = slice(piece * (d // per_sec), (piece + 1) * (d // per_sec))
        qt_s[blk, rs] = _dot_nt(wqt_ref[blk, :], xb_s[rs, :]).astype(BF16)

    def out_project(sec, piece):
        rs = rows_of(sec)
        blk = slice(piece * (d // per_sec), (piece + 1) * (d // per_sec))
        y_ref[rs, blk] = _dot(o_s[rs, :], wo_ref[:, blk])

    def normalise(sec):
        rs = rows_of(sec)
        y_ref[rs, :] = _layer_norm(alpha * x_ref[rs, :] + y_ref[rs, :], lg_ref[...], lb_ref[...])

    def scores(u, g):
        r0 = u * SWA_UNIT
        qt_g = jnp.concatenate(
            [qt_s[h * HEAD_DIM:(h + 1) * HEAD_DIM, r0:r0 + SWA_UNIT] for h in range(g * GROUP, (g + 1) * GROUP)],
            axis=1)
        qz = jnp.concatenate([qt_g, zeros] if g == 0 else [zeros, qt_g], axis=0)
        s_t = _dot(k_s[r0:r0 + SWA_KEYS, :], qz) + bias_ref[g]
        if r0 < WINDOW:
            s_t = jnp.where(j * tq + r0 - WINDOW + key_i >= 0, s_t, NEG_INF)
        return s_t

    def finish(u, g, s_t):
        r0 = u * SWA_UNIT
        sink = sink_ref[g]
        mx = jnp.maximum(jnp.max(s_t, axis=0, keepdims=True), sink)
        p = jnp.exp2(s_t - mx)
        den = jnp.sum(p, axis=0, keepdims=True) + jnp.exp2(sink - mx)
        vt_g = vt_s[g * HEAD_DIM:(g + 1) * HEAD_DIM, r0:r0 + SWA_KEYS]
        o_t = (_dot(vt_g, p.astype(BF16)) / den).astype(BF16)
        for i in range(0, GROUP, 2):
            pair = jnp.concatenate([o_t[:, i * SWA_UNIT:(i + 1) * SWA_UNIT],
                                    o_t[:, (i + 1) * SWA_UNIT:(i + 2) * SWA_UNIT]], axis=0)
            c0 = (g * GROUP + i) * HEAD_DIM
            o_s[r0:r0 + SWA_UNIT, c0:c0 + 2 * HEAD_DIM] = pair.T

    todo = [(u, g) for u in range(tq // SWA_UNIT) for g in range(N_KV_HEADS)]
    for piece in range(per_sec):
        project(0, piece)
    s_next = scores(*todo[0])
    for idx, (u, g) in enumerate(todo):
        sec, slot = divmod(idx, per_sec)
        if sec + 1 < n_sec:
            project(sec + 1, slot)
        if sec >= 1:
            out_project(sec - 1, slot)
        s_cur = s_next
        if idx + 1 < len(todo):
            s_next = scores(*todo[idx + 1])
        finish(u, g, s_cur)
        if sec >= 1 and slot == per_sec - 1:
            normalise(sec - 1)
    for piece in range(per_sec):
        out_project(n_sec - 1, piece)
    normalise(n_sec - 1)


def _swa_prompt_layer(x2d, kv2d, vt, wqt, wo, bias, sinks, ln_g, ln_b, *, alpha, batch, seq, tq):
    n, d = x2d.shape
    section = SWA_SECTION
    assert n == batch * seq and seq % tq == 0 and tq % section == 0 and section % SWA_UNIT == 0 and WINDOW == LANES
    assert d % ((section // SWA_UNIT) * N_KV_HEADS) == 0
    n_j = seq // tq
    per = tq // WINDOW
    prev = lambda b, j: jnp.maximum((b * n_j + j) * per - 1, 0)
    return pl.pallas_call(
        functools.partial(_swa_prompt_kernel, alpha=alpha, tq=tq, section=section),
        grid=(batch, n_j),
        in_specs=[
            pl.BlockSpec((tq, d), lambda b, j: (b * n_j + j, 0)),
            pl.BlockSpec((tq, LANES), lambda b, j: (b * n_j + j, 0)),
            pl.BlockSpec((WINDOW, LANES), lambda b, j: (prev(b, j), 0)),
            pl.BlockSpec((LANES, tq), lambda b, j: (0, b * n_j + j)),
            pl.BlockSpec((LANES, WINDOW), lambda b, j: (0, prev(b, j))),
            _const_spec(wqt.shape),
            _const_spec(wo.shape),
            _const_spec(bias.shape),
            _const_spec(sinks.shape),
            _spec_of(ln_g),
            _spec_of(ln_b),
        ],
        out_specs=pl.BlockSpec((tq, d), lambda b, j: (b * n_j + j, 0)),
        out_shape=jax.ShapeDtypeStruct((n, d), F32),
        scratch_shapes=[
            pltpu.VMEM((tq, d), BF16),
            pltpu.VMEM((d, tq), BF16),
            pltpu.VMEM((tq, d), BF16),
            pltpu.VMEM((WINDOW + tq, LANES), BF16),
            pltpu.VMEM((LANES, WINDOW + tq), BF16),
        ],
        compiler_params=pltpu.CompilerParams(
            dimension_semantics=("arbitrary", "arbitrary"), vmem_limit_bytes=VMEM_LIMIT_BYTES),
        name="swa_prompt",
    )(x2d, kv2d, kv2d, vt, vt, wqt, wo, bias, sinks, _arr(ln_g), _arr(ln_b))


def _swa_prompt_tables(sinks):
    slopes = jnp.exp2(-8.0 * jnp.arange(1, N_Q_HEADS + 1, dtype=F32) / N_Q_HEADS)
    q = jnp.arange(SWA_UNIT)[None, :]
    kx = jnp.arange(SWA_KEYS)[:, None]
    dist = jnp.abs(q + WINDOW - kx).astype(F32)
    first = q < CHUNK
    visible = (first & (kx < WINDOW + CHUNK)) | (~first & (kx >= CHUNK))
    bias = jnp.where(visible[None], -(LOG2_E * slopes)[:, None, None] * dist[None], NEG_INF)
    sinks = sinks.astype(F32) * LOG2_E
    bias = bias.reshape(N_KV_HEADS, GROUP, SWA_KEYS, SWA_UNIT).transpose(0, 2, 1, 3)
    sink = jnp.broadcast_to(sinks.astype(F32).reshape(N_KV_HEADS, 1, GROUP, 1), (N_KV_HEADS, 1, GROUP, SWA_UNIT))
    return (bias.reshape(N_KV_HEADS, SWA_KEYS, GROUP * SWA_UNIT), sink.reshape(N_KV_HEADS, 1, GROUP * SWA_UNIT))


def _swa_sample_kernel(x_ref, kvn_ref, kc_ref, vc_ref, wq_ref, wo_ref, bias_ref, sink_ref, lg_ref, lb_ref,
                       y_ref, q_s, o_s, *, alpha, batch, seq):
    x = x_ref[...]
    xb = x.astype(BF16)
    q_s[...] = _dot(xb, wq_ref[...]).astype(BF16)
    per_pass = 2
    for b0 in range(0, batch, per_pass):
        units = []
        for b in range(b0, min(b0 + per_pass, batch)):
            r0 = b * seq
            kv_old = jnp.concatenate([kc_ref[b], vc_ref[b]], axis=1)
            groups = _split_kv(jnp.concatenate([kv_old, kvn_ref[r0:r0 + seq, :]], axis=0))
            for g in range(N_KV_HEADS):
                q_rows = jnp.concatenate(
                    [q_s[r0:r0 + seq, (g * PAIRS + pp) * LANES:(g * PAIRS + pp + 1) * LANES] for pp in range(PAIRS)],
                    axis=0)
                units.append((q_rows, groups[g], (bias_ref[g, 0], bias_ref[g, 1]), (sink_ref[g, 0], sink_ref[g, 1])))
        outs = _attn_units(units)
        for i, o in enumerate(outs):
            r0 = (b0 + i // N_KV_HEADS) * seq
            g = i % N_KV_HEADS
            for pp in range(PAIRS):
                o_s[r0:r0 + seq, (g * PAIRS + pp) * LANES:(g * PAIRS + pp + 1) * LANES] = (
                    o[pp * seq:(pp + 1) * seq, :].astype(BF16))
    mix = _dot(o_s[...], wo_ref[...])
    y_ref[...] = _layer_norm(alpha * x + mix, lg_ref[...], lb_ref[...])


def _swa_sample_layer(x2d, kv_new, k_cache, v_cache, wq, wo, bias, sink, ln_g, ln_b, *, alpha, batch, seq):
    n, d = x2d.shape
    assert n == batch * seq
    args = (x2d, kv_new, k_cache, v_cache, wq, wo, bias, sink, ln_g, ln_b)
    return pl.pallas_call(
        functools.partial(_swa_sample_kernel, alpha=alpha, batch=batch, seq=seq),
        grid=(1,),
        in_specs=[_spec_of(a) for a in args],
        out_specs=pl.BlockSpec((n, d), lambda i: (0, 0)),
        out_shape=jax.ShapeDtypeStruct((n, d), F32),
        scratch_shapes=[pltpu.VMEM((n, d), BF16), pltpu.VMEM((n, d), BF16)],
        compiler_params=pltpu.CompilerParams(
            dimension_semantics=("arbitrary",), vmem_limit_bytes=VMEM_LIMIT_BYTES),
        name="swa_sample",
    )(*[_arr(a) for a in args])


def _attn_tables(sinks, ql, nk):
    slopes = jnp.exp2(-8.0 * jnp.arange(1, N_Q_HEADS + 1, dtype=F32) / N_Q_HEADS)
    dist = jnp.abs(jnp.arange(ql)[:, None] - jnp.arange(nk)[None, :] + (nk - ql)).astype(F32)
    head = (jnp.arange(N_KV_HEADS)[:, None, None] * GROUP + 2 * jnp.arange(PAIRS)[None, None, :]
            + jnp.arange(2)[None, :, None])
    bias = -slopes[head][..., None, None] * dist
    sink = jnp.broadcast_to(sinks.astype(F32)[head][..., None, None], head.shape + (ql, 1))
    return (bias.reshape(N_KV_HEADS, 2, PAIRS * ql, nk), sink.reshape(N_KV_HEADS, 2, PAIRS * ql, 1))


class _Tiles(NamedTuple):
    rows: int
    mixer: int
    sample_batch: int


def _tiles(batch, seq):
    n = batch * seq
    return _Tiles(rows=min(512, n), mixer=min(1024, seq), sample_batch=min(8, batch))


def _trunk(x, c0s, n0s, m0s, k_cache, v_cache, params, *, is_prompt, kv_rows):
    (w_in_a, b_gate_a, g_norm_a, w_out_a, w_kv, w_q_b, sinks_b, w_out_b, w_gu, w_down, ln_g, ln_b) = params
    batch, seq, d = x.shape
    depth = w_gu.shape[0]
    n_a = w_in_a.shape[0]
    alpha = (2 * depth) ** 0.25
    n = batch * seq
    x2d = x.reshape(n, d)
    tiles = _tiles(batch, seq)
    tm = tiles.rows
    L, bb_n, tt = seq, tiles.sample_batch, seq
    cs, ns, ms = [], [], []
    kv2d = None
    for layer in range(depth):
        if layer < n_a and is_prompt:
            x2d, c, nn, m = _mlstm_prompt_layer(
                x2d, w_in_a[layer], b_gate_a[layer], g_norm_a[layer], w_out_a[layer],
                _Row(ln_g, 2 * layer), _Row(ln_b, 2 * layer), alpha=alpha, batch=batch, seq=seq, tt=tiles.mixer)
            cs.append(c)
            ns.append(nn)
            ms.append(m)
        elif layer < n_a:
            w_all, bg, gn, wo = _prep_mlstm_weights(w_in_a[layer], b_gate_a[layer], g_norm_a[layer], w_out_a[layer])
            m0 = jnp.broadcast_to(m0s[layer].astype(F32)[:, :, None, None], (batch, M_HEADS, 8, LANES))
            x2d, c, nn, m = _mlstm_layer(
                x2d, c0s[layer].astype(F32), n0s[layer].astype(F32)[..., None], m0, w_all, bg, gn, wo,
                _Row(ln_g, 2 * layer), _Row(ln_b, 2 * layer), alpha=alpha, batch=batch, seq=seq, L=L, bb_n=bb_n, tt=tt)
            cs.append(c)
            ns.append(nn)
            ms.append(m)
        else:
            jb = layer - n_a
            if kv2d is None:
                kv2d, vt = _kv_proj(x2d, w_kv, tm=tm, with_vt=is_prompt)
            wq = (w_q_b[jb] * (HEAD_DIM ** -0.5)).astype(BF16)
            wo = w_out_b[jb].astype(BF16)
            if is_prompt:
                bias, sink = _swa_prompt_tables(sinks_b[jb])
                wqt = (w_q_b[jb] * (HEAD_DIM ** -0.5 * LOG2_E)).T.astype(BF16)
                x2d = _swa_prompt_layer(x2d, kv2d, vt, wqt, wo, bias, sink,
                                        _Row(ln_g, 2 * layer), _Row(ln_b, 2 * layer),
                                        alpha=alpha, batch=batch, seq=seq, tq=tiles.mixer)
            else:
                w_rows = k_cache.shape[1]
                bias, sink = _attn_tables(sinks_b[jb], seq, w_rows + seq)
                x2d = _swa_sample_layer(
                    x2d, kv2d, k_cache.astype(F32).reshape(batch, w_rows, N_KV_HEADS * HEAD_DIM),
                    v_cache.astype(F32).reshape(batch, w_rows, N_KV_HEADS * HEAD_DIM), wq, wo, bias, sink,
                    _Row(ln_g, 2 * layer), _Row(ln_b, 2 * layer), alpha=alpha, batch=batch, seq=seq)
        feeds_kv = layer + 1 == n_a and n_a < depth
        x2d, kv_new, vt_new = _ffn(
            x2d, w_gu, w_down, _Row(ln_g, 2 * layer + 1), _Row(ln_b, 2 * layer + 1), layer=layer, alpha=alpha,
            tm=min(2 * tm, n), sub=tm, w_kv=w_kv if feeds_kv else None, with_vt=feeds_kv and is_prompt)
        if feeds_kv:
            kv2d, vt = kv_new, vt_new
    kv = kv2d.reshape(batch, seq, kv2d.shape[1])[:, seq - kv_rows:].reshape(batch, kv_rows, 2, N_KV_HEADS, HEAD_DIM)
    return (x2d.reshape(batch, seq, d), jnp.stack(cs), jnp.stack(ns), jnp.stack(ms), kv[:, :, 0], kv[:, :, 1])


def kernel(x_prompt, x_sample, state_C, state_n, state_m, cache_k, cache_v, w_in_a, b_gate_a, g_norm_a,
           w_out_a, w_kv, w_q_b, sinks_b, w_out_b, w_gu, w_down, ln_g, ln_b):
    params = (w_in_a, b_gate_a, g_norm_a, w_out_a, w_kv, w_q_b, sinks_b, w_out_b,
              w_gu.astype(BF16), w_down.astype(BF16),
              ln_g.astype(F32).reshape(-1, 1, ln_g.shape[-1]), ln_b.astype(F32).reshape(-1, 1, ln_b.shape[-1]))
    y_p, p_c, p_n, p_m, p_k, p_v = _trunk(x_prompt, None, None, None, None, None, params, is_prompt=True,
                                          kv_rows=min(WINDOW, x_prompt.shape[1]))
    y_s, s_c, s_n, s_m, s_k, s_v = _trunk(x_sample, state_C, state_n, state_m, cache_k, cache_v, params,
                                          is_prompt=False, kv_rows=x_sample.shape[1])
    return (y_p, y_s, p_c, p_n, p_m, p_k, p_v, s_c, s_n, s_m, s_k, s_v)
```

```python
import functools
from typing import NamedTuple

import jax
import jax.numpy as jnp
from jax import lax
from jax.experimental import pallas as pl
from jax.experimental.pallas import tpu as pltpu

F32 = jnp.float32
BF16 = jnp.bfloat16

CHUNK = 64
WINDOW = 128
M_HEADS = 8
M_DK = 64
M_DV = 128
N_Q_HEADS = 16
N_KV_HEADS = 2
HEAD_DIM = 64
GROUP = N_Q_HEADS // N_KV_HEADS
LN_EPS = 1e-5
HEAD_NORM_EPS = 1e-6

LANES = 128
LOG2_E = 1.4426950408889634
VMEM_LIMIT_BYTES = 56 * 1024 * 1024

NEG_INF = float("-inf")
HIGHEST = lax.Precision.HIGHEST


def _const_spec(shape):
    nd = len(shape)
    return pl.BlockSpec(shape, lambda *_: (0,) * nd, pipeline_mode=pl.Buffered(1))


class _Row(NamedTuple):
    array: jax.Array
    index: int


def _spec_of(a):
    if isinstance(a, _Row):
        return pl.BlockSpec((None,) + tuple(a.array.shape[1:]), lambda *_: (a.index, 0, 0),
                            pipeline_mode=pl.Buffered(1))
    return _const_spec(a.shape)


def _arr(a):
    return a.array if isinstance(a, _Row) else a


def _layer_norm(y, g, b):
    mu = jnp.mean(y, axis=-1, keepdims=True)
    yc = y - mu
    var = jnp.mean(yc * yc, axis=-1, keepdims=True)
    return yc * lax.rsqrt(var + LN_EPS) * g + b


def _dot(a, b):
    return jnp.dot(a, b, preferred_element_type=F32)


def _dot_nt(a, b):
    return lax.dot_general(a, b, (((1,), (1,)), ((), ())), preferred_element_type=F32)


def _dot_tn(a, b):
    return lax.dot_general(a, b, (((0,), (0,)), ((), ())), preferred_element_type=F32)


FF_BLOCK = 256


def _ffn_kernel(*refs, alpha, d_ff, sub, with_kv, with_vt):
    x_ref, wgu_ref, wd_ref, g_ref, b_ref = refs[:5]
    rest = list(refs[5:])
    wkv_ref = rest.pop(0) if with_kv else None
    wvt_ref = rest.pop(0) if with_vt else None
    o_ref = rest.pop(0)
    kv_ref = rest.pop(0) if with_kv else None
    vt_ref = rest.pop(0) if with_vt else None
    acc_ref = rest.pop(0)
    h_ref = rest.pop(0)
    tm = x_ref.shape[0]

    n_fin = 4

    def finish(i, part):
        step = sub // n_fin
        rs = slice(i * sub + part * step, i * sub + (part + 1) * step)
        y = _layer_norm(alpha * x_ref[rs, :] + acc_ref[rs, :], g_ref[...], b_ref[...])
        o_ref[rs, :] = y
        if with_kv:
            yb = y.astype(BF16)
            kv_ref[rs, :] = _dot(yb, wkv_ref[...])
            if with_vt:
                vt_ref[:, rs] = _dot_nt(wvt_ref[...], yb).astype(BF16)

    for i in range(tm // sub):
        rs = slice(i * sub, (i + 1) * sub)
        xb = x_ref[rs, :].astype(BF16)
        for j in range(d_ff // FF_BLOCK):
            sl = slice(j * FF_BLOCK, (j + 1) * FF_BLOCK)
            su = slice(d_ff + j * FF_BLOCK, d_ff + (j + 1) * FF_BLOCK)
            g = _dot(xb, wgu_ref[:, sl])
            u = _dot(xb, wgu_ref[:, su])
            h_ref[rs, sl] = (g * jax.nn.sigmoid(g) * u).astype(BF16)
            if i > 0 and j % 2 == 0 and j // 2 < n_fin:
                finish(i - 1, j // 2)
        acc_ref[rs, :] = _dot(h_ref[rs, :], wd_ref[...])
    for part in range(n_fin):
        finish(tm // sub - 1, part)


def _layer_spec(shape, layer):
    nd = len(shape) - 1
    return pl.BlockSpec((None,) + tuple(shape[1:]), lambda *_: (layer,) + (0,) * nd, pipeline_mode=pl.Buffered(1))


def _ffn(x2d, w_gu, w_down, ln_g, ln_b, *, layer, alpha, tm, sub, w_kv=None, with_vt=False):
    n, d = x2d.shape
    d_ff = w_down.shape[1]
    assert n % tm == 0 and tm % sub == 0 and d_ff % FF_BLOCK == 0 and w_gu.shape[2] == 2 * d_ff
    with_kv = w_kv is not None
    assert with_kv or not with_vt
    args = [x2d, w_gu, w_down, _arr(ln_g), _arr(ln_b)]
    in_specs = [pl.BlockSpec((tm, d), lambda i: (i, 0)), _layer_spec(w_gu.shape, layer),
                _layer_spec(w_down.shape, layer), _spec_of(ln_g), _spec_of(ln_b)]
    out_specs = [pl.BlockSpec((tm, d), lambda i: (i, 0))]
    out_shape = [jax.ShapeDtypeStruct((n, d), F32)]
    if with_kv:
        nk = w_kv.shape[1]
        args.append(w_kv.astype(BF16))
        in_specs.append(_const_spec(w_kv.shape))
        out_specs.append(pl.BlockSpec((tm, nk), lambda i: (i, 0)))
        out_shape.append(jax.ShapeDtypeStruct((n, nk), F32))
        if with_vt:
            w_vt = w_kv[:, nk // 2:].T.astype(BF16)
            args.append(w_vt)
            in_specs.insert(len(in_specs), _const_spec(w_vt.shape))
            out_specs.append(pl.BlockSpec((nk // 2, tm), lambda i: (0, i)))
            out_shape.append(jax.ShapeDtypeStruct((nk // 2, n), BF16))
    outs = pl.pallas_call(
        functools.partial(_ffn_kernel, alpha=alpha, d_ff=d_ff, sub=sub, with_kv=with_kv, with_vt=with_vt),
        grid=(n // tm,),
        in_specs=in_specs,
        out_specs=out_specs,
        out_shape=out_shape,
        scratch_shapes=[pltpu.VMEM((tm, d), F32), pltpu.VMEM((tm, d_ff), BF16)],
        compiler_params=pltpu.CompilerParams(
            dimension_semantics=("arbitrary",), vmem_limit_bytes=VMEM_LIMIT_BYTES),
        name="ffn_kv" if with_kv else "ffn",
    )(*args)
    return outs[0], (outs[1] if with_kv else None), (outs[2] if with_vt else None)


def _kv_kernel(x_ref, w_ref, o_ref):
    o_ref[...] = _dot(x_ref[...].astype(BF16), w_ref[...])


def _kv_t_kernel(x_ref, w_ref, wvt_ref, o_ref, vt_ref):
    xb = x_ref[...].astype(BF16)
    o_ref[...] = _dot(xb, w_ref[...])
    vt_ref[...] = _dot_nt(wvt_ref[...], xb).astype(BF16)


def _kv_proj(x2d, w_kv, *, tm, with_vt):
    n, d = x2d.shape
    nk = w_kv.shape[1]
    assert n % tm == 0
    params = pltpu.CompilerParams(dimension_semantics=("arbitrary",), vmem_limit_bytes=VMEM_LIMIT_BYTES)
    x_spec = pl.BlockSpec((tm, d), lambda i: (i, 0))
    kv_spec = pl.BlockSpec((tm, nk), lambda i: (i, 0))
    w_kv_b = w_kv.astype(BF16)
    if not with_vt:
        kv = pl.pallas_call(
            _kv_kernel, grid=(n // tm,), in_specs=[x_spec, _const_spec(w_kv.shape)], out_specs=kv_spec,
            out_shape=jax.ShapeDtypeStruct((n, nk), F32), compiler_params=params, name="kv_proj",
        )(x2d, w_kv_b)
        return kv, None
    nv = nk // 2
    w_vt = w_kv[:, nv:].T.astype(BF16)
    return pl.pallas_call(
        _kv_t_kernel,
        grid=(n // tm,),
        in_specs=[x_spec, _const_spec(w_kv.shape), _const_spec(w_vt.shape)],
        out_specs=[kv_spec, pl.BlockSpec((nv, tm), lambda i: (0, i))],
        out_shape=[jax.ShapeDtypeStruct((n, nk), F32), jax.ShapeDtypeStruct((nv, n), BF16)],
        compiler_params=params,
        name="kv_proj_t",
    )(x2d, w_kv_b, w_vt)


A_QOFF = 0
A_KOFF = M_HEADS * LANES
A_VOFF = 2 * M_HEADS * LANES
A_OOFF = 3 * M_HEADS * LANES
A_GOFF = 4 * M_HEADS * LANES
A_COLS = A_GOFF + 2 * LANES
A_COL_BLOCK = 512


def _mlstm_kernel(x_ref, c0_ref, n0_ref, m0_ref, w_ref, bg_ref, gn_ref, wo_ref, lg_ref, lb_ref,
                  y_ref, cout_ref, mout_ref,
                  q_s, k_s, v_s, o_s, g_s, h_s, hb_s, c_s, m_s,
                  *, alpha, L, bb_n, tt, n_j):
    j = pl.program_id(1)
    rows = bb_n * tt
    units_per_b = tt // L
    hd = M_HEADS * LANES

    @pl.when(j == 0)
    def _():
        lane = lax.broadcasted_iota(jnp.int32, (1, 1, 1, LANES), 3)
        c_s[:, :, :, 0:LANES] = c0_ref[...]
        c_s[:, :, :, LANES:2 * LANES] = jnp.where(lane == 0, n0_ref[...], 0.0)
        m_s[...] = m0_ref[...]

    x = x_ref[...]
    xb = x.astype(BF16)
    for cb in range(0, hd, A_COL_BLOCK):
        q_s[:, cb:cb + A_COL_BLOCK] = _dot(xb, w_ref[:, A_QOFF + cb:A_QOFF + cb + A_COL_BLOCK]).astype(BF16)
        k_s[:, cb:cb + A_COL_BLOCK] = _dot(xb, w_ref[:, A_KOFF + cb:A_KOFF + cb + A_COL_BLOCK])
        v_s[:, cb:cb + A_COL_BLOCK] = _dot(xb, w_ref[:, A_VOFF + cb:A_VOFF + cb + A_COL_BLOCK]).astype(BF16)
        o_s[:, cb:cb + A_COL_BLOCK] = _dot(xb, w_ref[:, A_OOFF + cb:A_OOFF + cb + A_COL_BLOCK])
    g_s[...] = _dot(xb, w_ref[:, A_GOFF:A_GOFF + 2 * LANES]) + bg_ref[...]

    row_i = lax.broadcasted_iota(jnp.int32, (L, L), 0)
    col_i = lax.broadcasted_iota(jnp.int32, (L, L), 1)
    causal = row_i >= col_i
    tri = causal.astype(F32)
    ones_col = jnp.where(lax.broadcasted_iota(jnp.int32, (L, LANES), 1) == 0, 1.0, 0.0).astype(BF16)

    def unit(u, carry):
        r0 = pl.multiple_of(u * L, L)
        bb = u // units_per_b if bb_n > 1 else 0
        gi = g_s[pl.ds(r0, L), 0:LANES]
        gf = g_s[pl.ds(r0, L), LANES:2 * LANES]
        lf = jnp.minimum(gf, 0.0) - jnp.log1p(jnp.exp(-jnp.abs(gf)))
        bc = lax.dot_general(tri, lf, (((1,), (0,)), ((), ())), precision=HIGHEST,
                             preferred_element_type=F32)
        rs = gi - bc
        rs_t = rs.T
        b_last = bc[L - 1:L, :]
        heads = range(M_HEADS)
        hs_of = lambda h: slice(h * LANES, (h + 1) * LANES)
        qs = [q_s[pl.ds(r0, L), hs_of(h)] for h in heads]
        ks = [k_s[pl.ds(r0, L), hs_of(h)] for h in heads]
        vas = [jnp.concatenate([v_s[pl.ds(r0, L), hs_of(h)], ones_col], axis=1) for h in heads]
        m0s = [m_s[bb, h, 0:1, 0:1] for h in heads]
        cs = [c_s[bb, h] for h in heads]
        ss = [_dot_nt(qs[h], ks[h].astype(BF16)) for h in heads]
        qcs = [_dot(qs[h][:, 0:M_DK], cs[h].astype(BF16)) for h in heads]
        b_cols = [bc[:, h:h + 1] for h in heads]
        logds = [jnp.where(causal, b_cols[h] + rs_t[h:h + 1, :], NEG_INF) for h in heads]
        inters = [b_cols[h] + m0s[h] for h in heads]
        mxs = [jnp.maximum(inters[h], jnp.max(logds[h], axis=1, keepdims=True)) for h in heads]
        ps = [(ss[h] * jnp.exp(logds[h] - mxs[h])).astype(BF16) for h in heads]
        nds = [jnp.exp(inters[h] - mxs[h]) * qcs[h] + _dot(ps[h], vas[h]) for h in heads]
        for h in heads:
            num = nds[h][:, 0:M_DV]
            den = nds[h][:, M_DV:M_DV + 1]
            h_s[pl.ds(r0, L), hs_of(h)] = num / jnp.maximum(jnp.abs(den), jnp.exp(-mxs[h]))
        m_news = [mxs[h][L - 1:L, :] for h in heads]
        bls = [b_last[:, h:h + 1] for h in heads]
        wks = [(ks[h][:, 0:M_DK] * jnp.exp(rs[:, h:h + 1] + (bls[h] - m_news[h]))).astype(BF16) for h in heads]
        upds = [_dot_tn(wks[h], vas[h]) for h in heads]
        for h in heads:
            c_s[bb, h] = jnp.exp(bls[h] + m0s[h] - m_news[h]) * cs[h] + upds[h]
            m_s[bb, h] = jnp.broadcast_to(m_news[h], (8, LANES))
        return carry

    lax.fori_loop(0, rows // L, unit, 0)

    for h in range(M_HEADS):
        hs = slice(h * LANES, (h + 1) * LANES)
        hh = h_s[:, hs]
        hn = hh * lax.rsqrt(jnp.mean(hh * hh, axis=-1, keepdims=True) + HEAD_NORM_EPS)
        hb_s[:, hs] = (hn * gn_ref[:, hs] * jax.nn.sigmoid(o_s[:, hs])).astype(BF16)
    mix = _dot(hb_s[...], wo_ref[...])
    y_ref[...] = _layer_norm(alpha * x + mix, lg_ref[...], lb_ref[...])

    @pl.when(j == n_j - 1)
    def _():
        cout_ref[...] = c_s[...]
        mout_ref[...] = m_s[...]


def _mlstm_layer(x2d, c0, n0, m0, w_all, b_gate, g_norm, w_out, ln_g, ln_b, *, alpha, batch, seq, L, bb_n, tt):
    n, d = x2d.shape
    assert n == batch * seq and seq % tt == 0 and tt % L == 0 and batch % bb_n == 0
    assert bb_n == 1 or tt == seq
    n_j = seq // tt
    rows = bb_n * tt
    hd = M_HEADS * LANES
    kern = functools.partial(_mlstm_kernel, alpha=alpha, L=L, bb_n=bb_n, tt=tt, n_j=n_j)
    st4 = lambda bi, j: (bi, 0, 0, 0)
    y, c_out, m_out = pl.pallas_call(
        kern,
        grid=(batch // bb_n, n_j),
        in_specs=[
            pl.BlockSpec((rows, d), lambda bi, j: (bi * n_j + j, 0)),
            pl.BlockSpec((bb_n, M_HEADS, M_DK, M_DV), st4),
            pl.BlockSpec((bb_n, M_HEADS, M_DK, 1), st4),
            pl.BlockSpec((bb_n, M_HEADS, 8, LANES), st4),
            _const_spec(w_all.shape),
            _const_spec(b_gate.shape),
            _const_spec(g_norm.shape),
            _const_spec(w_out.shape),
            _spec_of(ln_g),
            _spec_of(ln_b),
        ],
        out_specs=[
            pl.BlockSpec((rows, d), lambda bi, j: (bi * n_j + j, 0)),
            pl.BlockSpec((bb_n, M_HEADS, M_DK, 2 * LANES), st4),
            pl.BlockSpec((bb_n, M_HEADS, 8, LANES), st4),
        ],
        out_shape=[
            jax.ShapeDtypeStruct((n, d), F32),
            jax.ShapeDtypeStruct((batch, M_HEADS, M_DK, 2 * LANES), F32),
            jax.ShapeDtypeStruct((batch, M_HEADS, 8, LANES), F32),
        ],
        scratch_shapes=[
            pltpu.VMEM((rows, hd), BF16),
            pltpu.VMEM((rows, hd), F32),
            pltpu.VMEM((rows, hd), BF16),
            pltpu.VMEM((rows, hd), F32),
            pltpu.VMEM((rows, 2 * LANES), F32),
            pltpu.VMEM((rows, hd), F32),
            pltpu.VMEM((rows, hd), BF16),
            pltpu.VMEM((bb_n, M_HEADS, M_DK, 2 * LANES), F32),
            pltpu.VMEM((bb_n, M_HEADS, 8, LANES), F32),
        ],
        compiler_params=pltpu.CompilerParams(
            dimension_semantics=("arbitrary", "arbitrary"), vmem_limit_bytes=VMEM_LIMIT_BYTES),
        name="mlstm_layer",
    )(x2d, c0, n0, m0, w_all, b_gate, g_norm, w_out, _arr(ln_g), _arr(ln_b))
    c_new = c_out[..., 0:M_DV]
    n_new = c_out[..., M_DV]
    m_new = m_out[:, :, 0, 0]
    return y, c_new, n_new, m_new


def _prep_mlstm_weights(w_in, b_gate, g_norm, w_out):
    d = w_in.shape[0]
    hk = M_HEADS * M_DK
    hv = M_HEADS * M_DV
    wq = w_in[:, 0:hk].reshape(d, M_HEADS, M_DK)
    wk = w_in[:, hk:2 * hk].reshape(d, M_HEADS, M_DK) * (M_DK ** -0.5)
    pad = ((0, 0), (0, 0), (0, LANES - M_DK))
    wq = jnp.pad(wq, pad).reshape(d, M_HEADS * LANES)
    wk = jnp.pad(wk, pad).reshape(d, M_HEADS * LANES)
    wv = w_in[:, 2 * hk:2 * hk + hv]
    wo = w_in[:, 2 * hk + hv:2 * hk + 2 * hv]
    wi = jnp.pad(w_in[:, 2 * hk + 2 * hv:2 * hk + 2 * hv + M_HEADS], ((0, 0), (0, LANES - M_HEADS)))
    wf = jnp.pad(w_in[:, 2 * hk + 2 * hv + M_HEADS:], ((0, 0), (0, LANES - M_HEADS)))
    w_all = jnp.concatenate([wq, wk, wv, wo, wi, wf], axis=1).astype(BF16)
    bg = jnp.concatenate([jnp.pad(b_gate[0:M_HEADS], (0, LANES - M_HEADS)),
                          jnp.pad(b_gate[M_HEADS:], (0, LANES - M_HEADS))]).astype(F32)[None, :]
    return w_all, bg, g_norm.astype(F32)[None, :], w_out.astype(BF16)


P_L = LANES
P_AUG = M_DV + 16
P_GATE_COPIES = 3
P_FGATE_LANE = 32


P_SECTION = 256


def _mlstm_pipe_kernel(x_ref, wqt_ref, wk_ref, wvt_ref, wot_ref, wg_ref, bg_ref, gn_ref, wout_ref,
                       lg_ref, lb_ref, y_ref, cout_ref, mout_ref,
                       xb_s, qt_s, k_s, vt_s, ot_s, g_s, ht_s, hn_s, c_s, m_s, *, alpha, tt, n_j):
    j = pl.program_id(1)
    L = P_L
    H = M_HEADS
    hv = H * M_DV
    d = x_ref.shape[1]
    section = P_SECTION
    n_sec = tt // section
    units = tt // L
    rows_of = lambda sec: slice(sec * section, (sec + 1) * section)

    @pl.when(j == 0)
    def _():
        c_s[...] = jnp.zeros_like(c_s)
        m_s[...] = jnp.zeros_like(m_s)

    row_i = lax.broadcasted_iota(jnp.int32, (L, LANES), 0)
    lane_i = lax.broadcasted_iota(jnp.int32, (L, LANES), 1)
    tri = (row_i >= lane_i).astype(F32)
    key8 = lax.broadcasted_iota(jnp.int32, (L, H * L), 0)
    qry8 = lax.broadcasted_iota(jnp.int32, (L, H * L), 1) & (L - 1)
    causal8 = key8 <= qry8
    ones_rows = jnp.ones((P_AUG - M_DV, L), F32)
    gn = gn_ref[...]

    zeros_dk = jnp.zeros((M_DK, L), BF16)

    def k_pair(h, ts):
        return k_s[ts, (h // 2) * LANES:(h // 2 + 1) * LANES]

    def pad_head(h, a):
        return jnp.concatenate([a, zeros_dk] if h % 2 == 0 else [zeros_dk, a], axis=0)

    def proj_pieces(sec):
        rs = rows_of(sec)

        def q_and_gates():
            xb_s[rs, :] = x_ref[rs, :].astype(BF16)
            qt_s[:, rs] = _dot_nt(wqt_ref[...], xb_s[rs, :]).astype(BF16)
            g_s[rs, :] = _dot(xb_s[rs, :], wg_ref[...]) + bg_ref[...]

        def keys():
            k_s[rs, :] = _dot(xb_s[rs, :], wk_ref[...]).astype(BF16)

        def vt_block(cb):
            vt_s[cb:cb + A_COL_BLOCK, rs] = _dot_nt(wvt_ref[cb:cb + A_COL_BLOCK, :], xb_s[rs, :])

        def ot_block(cb):
            ot_s[cb:cb + A_COL_BLOCK, rs] = _dot_nt(wot_ref[cb:cb + A_COL_BLOCK, :], xb_s[rs, :])

        blocks = range(0, hv, A_COL_BLOCK)
        return ([q_and_gates, keys],
                [functools.partial(f, cb) for f in (vt_block, ot_block) for cb in blocks])

    def post_pieces(sec):
        rs = rows_of(sec)

        def norm_gate(heads):
            for h in heads:
                hs = slice(h * M_DV, (h + 1) * M_DV)
                hh = ht_s[hs, rs]
                scale = lax.rsqrt(jnp.mean(hh * hh, axis=0, keepdims=True) + HEAD_NORM_EPS)
                gcol = jnp.concatenate([gn[hs, :]] * (section // LANES), axis=1)
                hn_s[rs, hs] = (hh * scale * gcol * jax.nn.sigmoid(ot_s[hs, rs])).astype(BF16).T

        def out_block(i):
            blk = slice(i * (d // 4), (i + 1) * (d // 4))
            y_ref[rs, blk] = _dot(hn_s[rs, :], wout_ref[:, blk])

        def deep_norm():
            y_ref[rs, :] = _layer_norm(alpha * x_ref[rs, :] + y_ref[rs, :], lg_ref[...], lb_ref[...])

        return ([functools.partial(norm_gate, range(0, H // 2)), functools.partial(norm_gate, range(H // 2, H)),
                 functools.partial(out_block, 0), functools.partial(out_block, 1)],
                [functools.partial(out_block, 2), functools.partial(out_block, 3), deep_norm])

    def pre(u):
        ts = slice(u * L, (u + 1) * L)
        gi = g_s[ts, :]
        gf = pltpu.roll(gi, LANES - P_FGATE_LANE, 1)
        lf = jnp.minimum(gf, 0.0) - jnp.log1p(jnp.exp(-jnp.abs(gf)))
        bc = lax.dot_general(tri, lf, (((1,), (0,)), ((), ())), precision=HIGHEST,
                             preferred_element_type=F32)
        rs = gi - bc
        cm = rs
        k = 1
        while k < L:
            cm = jnp.maximum(cm, jnp.where(row_i >= k, pltpu.roll(cm, k, 0), NEG_INF))
            k *= 2
        packed = jnp.where(lane_i < H, rs, jnp.where(lane_i < 2 * H, cm, bc))
        rows = packed.T[0:3 * H]
        rs_cat = jnp.concatenate([jnp.broadcast_to(rs[:, h:h + 1], (L, L)) for h in range(H)], axis=1)
        e_cat = jnp.where(causal8, rs_cat, NEG_INF)
        s_cat = jnp.concatenate(
            [_dot(k_pair(h, ts), jnp.concatenate([pad_head(h + i, qt_s[(h + i) * M_DK:(h + i + 1) * M_DK, ts])
                                                  for i in range(2)], axis=1))
             for h in range(0, H, 2)], axis=1)
        return rows, e_cat, s_cat

    def rec(u, pre_u, fillers):
        ts = slice(u * L, (u + 1) * L)
        rows, e_cat, s_cat = pre_u
        rs_t = rows[0:H]
        cm_t = rows[H:2 * H]
        b_t = rows[2 * H:3 * H]
        m0 = m_s[...]
        a_t = jnp.maximum(m0, cm_t)
        w_inter = jnp.exp(m0 - a_t)
        emx = jnp.exp(-(b_t + a_t))
        m_new = jnp.broadcast_to((b_t + a_t)[:, L - 1:L], (H, L))
        b_last = jnp.broadcast_to(b_t[:, L - 1:L], (H, L))
        w_end = jnp.exp(rs_t + (b_last - m_new))
        decay = jnp.exp(b_last + m0 - m_new)
        m_s[...] = m_new
        a_cat = jnp.concatenate([a_t[h:h + 1, :] for h in range(H)], axis=1)
        p_cat = (s_cat * jnp.exp(e_cat - a_cat)).astype(BF16)
        for h in range(H):
            if fillers:
                fillers.pop(0)()
            va = jnp.concatenate([vt_s[h * M_DV:(h + 1) * M_DV, ts], ones_rows], axis=0)
            ct = c_s[h]
            qw = (qt_s[h * M_DK:(h + 1) * M_DK, ts].astype(F32) * w_inter[h:h + 1, :]).astype(BF16)
            lhs = jnp.concatenate([va.astype(BF16), ct.astype(BF16)], axis=1)
            rhs = jnp.concatenate([p_cat[:, h * L:(h + 1) * L], pad_head(h, qw)], axis=0)
            nd = _dot(lhs, rhs)
            inv = 1.0 / jnp.maximum(jnp.abs(nd[M_DV:M_DV + 1, :]), emx[h:h + 1, :])
            ht_s[h * M_DV:(h + 1) * M_DV, ts] = nd[0:M_DV, :] * inv
            upd = _dot((va * w_end[h:h + 1, :]).astype(BF16), k_pair(h, ts))
            c_s[h] = decay[h:h + 1, :] * ct + upd
        while fillers:
            fillers.pop(0)()

    first, rest = proj_pieces(0)
    for piece in first + rest:
        piece()
    pre_next = pre(0)
    for u in range(units):
        sec, slot = divmod(u, section // L)
        fillers = []
        if sec + 1 < n_sec:
            fillers += proj_pieces(sec + 1)[slot]
        if sec >= 1:
            fillers += post_pieces(sec - 1)[slot]
        pre_cur = pre_next
        if u + 1 < units:
            pre_next = pre(u + 1)
        rec(u, pre_cur, fillers)
    for half in post_pieces(n_sec - 1):
        for piece in half:
            piece()

    @pl.when(j == n_j - 1)
    def _():
        cout_ref[0] = c_s[...]
        mout_ref[0] = m_s[...]


def _mlstm_prompt_layer(x2d, w_in, b_gate, g_norm, w_out, ln_g, ln_b, *, alpha, batch, seq, tt):
    n, d = x2d.shape
    assert n == batch * seq and seq % tt == 0 and tt % P_SECTION == 0 and P_SECTION == 2 * P_L
    assert P_GATE_COPIES * M_HEADS <= P_FGATE_LANE <= LANES - P_GATE_COPIES * M_HEADS and d % 4 == 0
    n_j = seq // tt
    hk = M_HEADS * M_DK
    hv = M_HEADS * M_DV
    wqt = w_in[:, 0:hk].T.astype(BF16)
    wk = (w_in[:, hk:2 * hk] * (M_DK ** -0.5)).astype(BF16)
    wvt = w_in[:, 2 * hk:2 * hk + hv].T.astype(BF16)
    wot = w_in[:, 2 * hk + hv:2 * hk + 2 * hv].T.astype(BF16)
    rep = lambda a, width: jnp.pad(jnp.tile(a, (1, P_GATE_COPIES)), ((0, 0), (0, width - P_GATE_COPIES * M_HEADS)))
    both = lambda i, f: jnp.concatenate([rep(i, P_FGATE_LANE), rep(f, LANES - P_FGATE_LANE)], axis=1)
    g0 = 2 * hk + 2 * hv
    wg = both(w_in[:, g0:g0 + M_HEADS], w_in[:, g0 + M_HEADS:]).astype(BF16)
    bg = both(b_gate[None, 0:M_HEADS], b_gate[None, M_HEADS:]).astype(F32)
    gn = jnp.broadcast_to(g_norm.astype(F32)[:, None], (hv, LANES))
    wout = w_out.astype(BF16)
    consts = (wqt, wk, wvt, wot, wg, bg, gn, wout, ln_g, ln_b)
    y, c_out, m_out = pl.pallas_call(
        functools.partial(_mlstm_pipe_kernel, alpha=alpha, tt=tt, n_j=n_j),
        grid=(batch, n_j),
        in_specs=[pl.BlockSpec((tt, d), lambda b, j: (b * n_j + j, 0))] + [_spec_of(a) for a in consts],
        out_specs=[
            pl.BlockSpec((tt, d), lambda b, j: (b * n_j + j, 0)),
            pl.BlockSpec((1, M_HEADS, P_AUG, 2 * M_DK), lambda b, j: (b, 0, 0, 0)),
            pl.BlockSpec((1, M_HEADS, LANES), lambda b, j: (b, 0, 0)),
        ],
        out_shape=[
            jax.ShapeDtypeStruct((n, d), F32),
            jax.ShapeDtypeStruct((batch, M_HEADS, P_AUG, 2 * M_DK), F32),
            jax.ShapeDtypeStruct((batch, M_HEADS, LANES), F32),
        ],
        scratch_shapes=[
            pltpu.VMEM((tt, d), BF16),
            pltpu.VMEM((hk, tt), BF16),
            pltpu.VMEM((tt, hk), BF16),
            pltpu.VMEM((hv, tt), F32),
            pltpu.VMEM((hv, tt), F32),
            pltpu.VMEM((tt, LANES), F32),
            pltpu.VMEM((hv, tt), F32),
            pltpu.VMEM((tt, hv), BF16),
            pltpu.VMEM((M_HEADS, P_AUG, 2 * M_DK), F32),
            pltpu.VMEM((M_HEADS, LANES), F32),
        ],
        compiler_params=pltpu.CompilerParams(
            dimension_semantics=("arbitrary", "arbitrary"), vmem_limit_bytes=VMEM_LIMIT_BYTES),
        name="mlstm_prompt",
    )(x2d, *[_arr(a) for a in consts])
    ct = jnp.stack([c_out[:, 0::2, :, 0:M_DK], c_out[:, 1::2, :, M_DK:]], axis=2)
    ct = ct.reshape(batch, M_HEADS, P_AUG, M_DK)
    c_new = jnp.swapaxes(ct[:, :, 0:M_DV, :], -1, -2)
    n_new = ct[:, :, M_DV, :]
    m_new = m_out[:, :, 0]
    return y, c_new, n_new, m_new


PAIRS = GROUP // 2


def _attn_units(units):
    halves = [(q, kv[i], kv[2 + i], bias[i], sink[i]) for q, kv, bias, sink in units for i in range(2)]
    ss = [_dot_nt(q, k_op) + bias for q, k_op, _, bias, _ in halves]
    mxs = [jnp.maximum(jnp.max(s, axis=-1, keepdims=True), hf[4]) for s, hf in zip(ss, halves)]
    ps = [jnp.exp(s - mx) for s, mx in zip(ss, mxs)]
    dens = [jnp.sum(p, axis=-1, keepdims=True) + jnp.exp(hf[4] - mx) for p, mx, hf in zip(ps, mxs, halves)]
    outs = [_dot(p.astype(BF16), hf[2]) / den for p, den, hf in zip(ps, dens, halves)]
    return [outs[2 * i] + outs[2 * i + 1] for i in range(len(units))]


def _split_kv(kv):
    lane = lax.broadcasted_iota(jnp.int32, (1, LANES), 1)
    low = lane < HEAD_DIM
    res = []
    kk = kv[:, 0:LANES]
    vv = kv[:, LANES:2 * LANES]
    kk_r = pltpu.roll(kk, HEAD_DIM, 1)
    vv_r = pltpu.roll(vv, HEAD_DIM, 1)
    z = jnp.zeros_like(kk)
    res.append((jnp.where(low, kk, z), jnp.where(low, z, kk_r), jnp.where(low, vv, z), jnp.where(low, z, vv_r)))
    res.append((jnp.where(low, kk_r, z), jnp.where(low, z, kk), jnp.where(low, vv_r, z), jnp.where(low, z, vv)))
    return [tuple(a.astype(BF16) for a in grp) for grp in res]


SWA_UNIT = 2 * CHUNK
SWA_KEYS = WINDOW + SWA_UNIT
SWA_SECTION = 256


def _swa_prompt_kernel(x_ref, kc_ref, kp_ref, vtc_ref, vtp_ref, wqt_ref, wo_ref, bias_ref, sink_ref,
                       lg_ref, lb_ref, y_ref, xb_s, qt_s, o_s, k_s, vt_s, *, alpha, tq, section):
    j = pl.program_id(1)
    k_s[0:WINDOW, :] = kp_ref[...].astype(BF16)
    k_s[WINDOW:WINDOW + tq, :] = kc_ref[...].astype(BF16)
    vt_s[:, 0:WINDOW] = vtp_ref[...]
    vt_s[:, WINDOW:WINDOW + tq] = vtc_ref[...]
    zeros = jnp.zeros((HEAD_DIM, GROUP * SWA_UNIT), BF16)
    key_i = lax.broadcasted_iota(jnp.int32, (SWA_KEYS, GROUP * SWA_UNIT), 0)
    d = x_ref.shape[1]
    n_sec = tq // section
    per_sec = (section // SWA_UNIT) * N_KV_HEADS
    rows_of = lambda sec: slice(sec * section, (sec + 1) * section)

    def project(sec, piece):
        rs = rows_of(sec)
        if piece == 0:
            xb_s[rs, :] = x_ref[rs, :].astype(BF16)
        blk = slice(piece * (d // per_sec), (piece + 1) * (d // per_sec))
        qt_s[blk, rs] = _dot_nt(wqt_ref[blk, :], xb_s[rs, :]).astype(BF16)

    def out_project(sec, piece):
        rs = rows_of(sec)
        blk = slice(piece * (d // per_sec), (piece + 1) * (d // per_sec))
        y_ref[rs, blk] = _dot(o_s[rs, :], wo_ref[:, blk])

    def normalise(sec):
        rs = rows_of(sec)
        y_ref[rs, :] = _layer_norm(alpha * x_ref[rs, :] + y_ref[rs, :], lg_ref[...], lb_ref[...])

    def scores(u, g):
        r0 = u * SWA_UNIT
        qt_g = jnp.concatenate(
            [qt_s[h * HEAD_DIM:(h + 1) * HEAD_DIM, r0:r0 + SWA_UNIT] for h in range(g * GROUP, (g + 1) * GROUP)],
            axis=1)
        qz = jnp.concatenate([qt_g, zeros] if g == 0 else [zeros, qt_g], axis=0)
        s_t = _dot(k_s[r0:r0 + SWA_KEYS, :], qz) + bias_ref[g]
        if r0 < WINDOW:
            s_t = jnp.where(j * tq + r0 - WINDOW + key_i >= 0, s_t, NEG_INF)
        return s_t

    def finish(u, g, s_t):
        r0 = u * SWA_UNIT
        sink = sink_ref[g]
        mx = jnp.maximum(jnp.max(s_t, axis=0, keepdims=True), sink)
        p = jnp.exp2(s_t - mx)
        den = jnp.sum(p, axis=0, keepdims=True) + jnp.exp2(sink - mx)
        vt_g = vt_s[g * HEAD_DIM:(g + 1) * HEAD_DIM, r0:r0 + SWA_KEYS]
        o_t = (_dot(vt_g, p.astype(BF16)) / den).astype(BF16)
        for i in range(0, GROUP, 2):
            pair = jnp.concatenate([o_t[:, i * SWA_UNIT:(i + 1) * SWA_UNIT],
                                    o_t[:, (i + 1) * SWA_UNIT:(i + 2) * SWA_UNIT]], axis=0)
            c0 = (g * GROUP + i) * HEAD_DIM
            o_s[r0:r0 + SWA_UNIT, c0:c0 + 2 * HEAD_DIM] = pair.T

    todo = [(u, g) for u in range(tq // SWA_UNIT) for g in range(N_KV_HEADS)]
    for piece in range(per_sec):
        project(0, piece)
    s_next = scores(*todo[0])
    for idx, (u, g) in enumerate(todo):
        sec, slot = divmod(idx, per_sec)
        if sec + 1 < n_sec:
            project(sec + 1, slot)
        if sec >= 1:
            out_project(sec - 1, slot)
        s_cur = s_next
        if idx + 1 < len(todo):
            s_next = scores(*todo[idx + 1])
        finish(u, g, s_cur)
        if sec >= 1 and slot == per_sec - 1:
            normalise(sec - 1)
    for piece in range(per_sec):
        out_project(n_sec - 1, piece)
    normalise(n_sec - 1)


def _swa_prompt_layer(x2d, kv2d, vt, wqt, wo, bias, sinks, ln_g, ln_b, *, alpha, batch, seq, tq):
    n, d = x2d.shape
    section = SWA_SECTION
    assert n == batch * seq and seq % tq == 0 and tq % section == 0 and section % SWA_UNIT == 0 and WINDOW == LANES
    assert d % ((section // SWA_UNIT) * N_KV_HEADS) == 0
    n_j = seq // tq
    per = tq // WINDOW
    prev = lambda b, j: jnp.maximum((b * n_j + j) * per - 1, 0)
    return pl.pallas_call(
        functools.partial(_swa_prompt_kernel, alpha=alpha, tq=tq, section=section),
        grid=(batch, n_j),
        in_specs=[
            pl.BlockSpec((tq, d), lambda b, j: (b * n_j + j, 0)),
            pl.BlockSpec((tq, LANES), lambda b, j: (b * n_j + j, 0)),
            pl.BlockSpec((WINDOW, LANES), lambda b, j: (prev(b, j), 0)),
            pl.BlockSpec((LANES, tq), lambda b, j: (0, b * n_j + j)),
            pl.BlockSpec((LANES, WINDOW), lambda b, j: (0, prev(b, j))),
            _const_spec(wqt.shape),
            _const_spec(wo.shape),
            _const_spec(bias.shape),
            _const_spec(sinks.shape),
            _spec_of(ln_g),
            _spec_of(ln_b),
        ],
        out_specs=pl.BlockSpec((tq, d), lambda b, j: (b * n_j + j, 0)),
        out_shape=jax.ShapeDtypeStruct((n, d), F32),
        scratch_shapes=[
            pltpu.VMEM((tq, d), BF16),
            pltpu.VMEM((d, tq), BF16),
            pltpu.VMEM((tq, d), BF16),
            pltpu.VMEM((WINDOW + tq, LANES), BF16),
            pltpu.VMEM((LANES, WINDOW + tq), BF16),
        ],
        compiler_params=pltpu.CompilerParams(
            dimension_semantics=("arbitrary", "arbitrary"), vmem_limit_bytes=VMEM_LIMIT_BYTES),
        name="swa_prompt",
    )(x2d, kv2d, kv2d, vt, vt, wqt, wo, bias, sinks, _arr(ln_g), _arr(ln_b))


def _swa_prompt_tables(sinks):
    slopes = jnp.exp2(-8.0 * jnp.arange(1, N_Q_HEADS + 1, dtype=F32) / N_Q_HEADS)
    q = jnp.arange(SWA_UNIT)[None, :]
    kx = jnp.arange(SWA_KEYS)[:, None]
    dist = jnp.abs(q + WINDOW - kx).astype(F32)
    first = q < CHUNK
    visible = (first & (kx < WINDOW + CHUNK)) | (~first & (kx >= CHUNK))
    bias = jnp.where(visible[None], -(LOG2_E * slopes)[:, None, None] * dist[None], NEG_INF)
    sinks = sinks.astype(F32) * LOG2_E
    bias = bias.reshape(N_KV_HEADS, GROUP, SWA_KEYS, SWA_UNIT).transpose(0, 2, 1, 3)
    sink = jnp.broadcast_to(sinks.astype(F32).reshape(N_KV_HEADS, 1, GROUP, 1), (N_KV_HEADS, 1, GROUP, SWA_UNIT))
    return (bias.reshape(N_KV_HEADS, SWA_KEYS, GROUP * SWA_UNIT), sink.reshape(N_KV_HEADS, 1, GROUP * SWA_UNIT))


def _swa_sample_kernel(x_ref, kvn_ref, kc_ref, vc_ref, wq_ref, wo_ref, bias_ref, sink_ref, lg_ref, lb_ref,
                       y_ref, q_s, o_s, *, alpha, batch, seq):
    x = x_ref[...]
    xb = x.astype(BF16)
    q_s[...] = _dot(xb, wq_ref[...]).astype(BF16)
    per_pass = 2
    for b0 in range(0, batch, per_pass):
        units = []
        for b in range(b0, min(b0 + per_pass, batch)):
            r0 = b * seq
            kv_old = jnp.concatenate([kc_ref[b], vc_ref[b]], axis=1)
            groups = _split_kv(jnp.concatenate([kv_old, kvn_ref[r0:r0 + seq, :]], axis=0))
            for g in range(N_KV_HEADS):
                q_rows = jnp.concatenate(
                    [q_s[r0:r0 + seq, (g * PAIRS + pp) * LANES:(g * PAIRS + pp + 1) * LANES] for pp in range(PAIRS)],
                    axis=0)
                units.append((q_rows, groups[g], (bias_ref[g, 0], bias_ref[g, 1]), (sink_ref[g, 0], sink_ref[g, 1])))
        outs = _attn_units(units)
        for i, o in enumerate(outs):
            r0 = (b0 + i // N_KV_HEADS) * seq
            g = i % N_KV_HEADS
            for pp in range(PAIRS):
                o_s[r0:r0 + seq, (g * PAIRS + pp) * LANES:(g * PAIRS + pp + 1) * LANES] = (
                    o[pp * seq:(pp + 1) * seq, :].astype(BF16))
    mix = _dot(o_s[...], wo_ref[...])
    y_ref[...] = _layer_norm(alpha * x + mix, lg_ref[...], lb_ref[...])


def _swa_sample_layer(x2d, kv_new, k_cache, v_cache, wq, wo, bias, sink, ln_g, ln_b, *, alpha, batch, seq):
    n, d = x2d.shape
    assert n == batch * seq
    args = (x2d, kv_new, k_cache, v_cache, wq, wo, bias, sink, ln_g, ln_b)
    return pl.pallas_call(
        functools.partial(_swa_sample_kernel, alpha=alpha, batch=batch, seq=seq),
        grid=(1,),
        in_specs=[_spec_of(a) for a in args],
        out_specs=pl.BlockSpec((n, d), lambda i: (0, 0)),
        out_shape=jax.ShapeDtypeStruct((n, d), F32),
        scratch_shapes=[pltpu.VMEM((n, d), BF16), pltpu.VMEM((n, d), BF16)],
        compiler_params=pltpu.CompilerParams(
            dimension_semantics=("arbitrary",), vmem_limit_bytes=VMEM_LIMIT_BYTES),
        name="swa_sample",
    )(*[_arr(a) for a in args])


def _attn_tables(sinks, ql, nk):
    slopes = jnp.exp2(-8.0 * jnp.arange(1, N_Q_HEADS + 1, dtype=F32) / N_Q_HEADS)
    dist = jnp.abs(jnp.arange(ql)[:, None] - jnp.arange(nk)[None, :] + (nk - ql)).astype(F32)
    head = (jnp.arange(N_KV_HEADS)[:, None, None] * GROUP + 2 * jnp.arange(PAIRS)[None, None, :]
            + jnp.arange(2)[None, :, None])
    bias = -slopes[head][..., None, None] * dist
    sink = jnp.broadcast_to(sinks.astype(F32)[head][..., None, None], head.shape + (ql, 1))
    return (bias.reshape(N_KV_HEADS, 2, PAIRS * ql, nk), sink.reshape(N_KV_HEADS, 2, PAIRS * ql, 1))


class _Tiles(NamedTuple):
    rows: int
    mixer: int
    sample_batch: int


def _tiles(batch, seq):
    n = batch * seq
    return _Tiles(rows=min(512, n), mixer=min(1024, seq), sample_batch=min(8, batch))


def _trunk(x, c0s, n0s, m0s, k_cache, v_cache, params, *, is_prompt, kv_rows):
    (w_in_a, b_gate_a, g_norm_a, w_out_a, w_kv, w_q_b, sinks_b, w_out_b, w_gu, w_down, ln_g, ln_b) = params
    batch, seq, d = x.shape
    depth = w_gu.shape[0]
    n_a = w_in_a.shape[0]
    alpha = (2 * depth) ** 0.25
    n = batch * seq
    x2d = x.reshape(n, d)
    tiles = _tiles(batch, seq)
    tm = tiles.rows
    L, bb_n, tt = seq, tiles.sample_batch, seq
    cs, ns, ms = [], [], []
    kv2d = None
    for layer in range(depth):
        if layer < n_a and is_prompt:
            x2d, c, nn, m = _mlstm_prompt_layer(
                x2d, w_in_a[layer], b_gate_a[layer], g_norm_a[layer], w_out_a[layer],
                _Row(ln_g, 2 * layer), _Row(ln_b, 2 * layer), alpha=alpha, batch=batch, seq=seq, tt=tiles.mixer)
            cs.append(c)
            ns.append(nn)
            ms.append(m)
        elif layer < n_a:
            w_all, bg, gn, wo = _prep_mlstm_weights(w_in_a[layer], b_gate_a[layer], g_norm_a[layer], w_out_a[layer])
            m0 = jnp.broadcast_to(m0s[layer].astype(F32)[:, :, None, None], (batch, M_HEADS, 8, LANES))
            x2d, c, nn, m = _mlstm_layer(
                x2d, c0s[layer].astype(F32), n0s[layer].astype(F32)[..., None], m0, w_all, bg, gn, wo,
                _Row(ln_g, 2 * layer), _Row(ln_b, 2 * layer), alpha=alpha, batch=batch, seq=seq, L=L, bb_n=bb_n, tt=tt)
            cs.append(c)
            ns.append(nn)
            ms.append(m)
        else:
            jb = layer - n_a
            if kv2d is None:
                kv2d, vt = _kv_proj(x2d, w_kv, tm=tm, with_vt=is_prompt)
            wq = (w_q_b[jb] * (HEAD_DIM ** -0.5)).astype(BF16)
            wo = w_out_b[jb].astype(BF16)
            if is_prompt:
                bias, sink = _swa_prompt_tables(sinks_b[jb])
                wqt = (w_q_b[jb] * (HEAD_DIM ** -0.5 * LOG2_E)).T.astype(BF16)
                x2d = _swa_prompt_layer(x2d, kv2d, vt, wqt, wo, bias, sink,
                                        _Row(ln_g, 2 * layer), _Row(ln_b, 2 * layer),
                                        alpha=alpha, batch=batch, seq=seq, tq=tiles.mixer)
            else:
                w_rows = k_cache.shape[1]
                bias, sink = _attn_tables(sinks_b[jb], seq, w_rows + seq)
                x2d = _swa_sample_layer(
                    x2d, kv2d, k_cache.astype(F32).reshape(batch, w_rows, N_KV_HEADS * HEAD_DIM),
                    v_cache.astype(F32).reshape(batch, w_rows, N_KV_HEADS * HEAD_DIM), wq, wo, bias, sink,
                    _Row(ln_g, 2 * layer), _Row(ln_b, 2 * layer), alpha=alpha, batch=batch, seq=seq)
        feeds_kv = layer + 1 == n_a and n_a < depth
        x2d, kv_new, vt_new = _ffn(
            x2d, w_gu, w_down, _Row(ln_g, 2 * layer + 1), _Row(ln_b, 2 * layer + 1), layer=layer, alpha=alpha,
            tm=min(2 * tm, n), sub=tm, w_kv=w_kv if feeds_kv else None, with_vt=feeds_kv and is_prompt)
        if feeds_kv:
            kv2d, vt = kv_new, vt_new
    kv = kv2d.reshape(batch, seq, kv2d.shape[1])[:, seq - kv_rows:].reshape(batch, kv_rows, 2, N_KV_HEADS, HEAD_DIM)
    return (x2d.reshape(batch, seq, d), jnp.stack(cs), jnp.stack(ns), jnp.stack(ms), kv[:, :, 0], kv[:, :, 1])


def kernel(x_prompt, x_sample, state_C, state_n, state_m, cache_k, cache_v, w_in_a, b_gate_a, g_norm_a,
           w_out_a, w_kv, w_q_b, sinks_b, w_out_b, w_gu, w_down, ln_g, ln_b):
    params = (w_in_a, b_gate_a, g_norm_a, w_out_a, w_kv, w_q_b, sinks_b, w_out_b,
              w_gu.astype(BF16), w_down.astype(BF16),
              ln_g.astype(F32).reshape(-1, 1, ln_g.shape[-1]), ln_b.astype(F32).reshape(-1, 1, ln_b.shape[-1]))
    y_p, p_c, p_n, p_m, p_k, p_v = _trunk(x_prompt, None, None, None, None, None, params, is_prompt=True,
                                          kv_rows=min(WINDOW, x_prompt.shape[1]))
    y_s, s_c, s_n, s_m, s_k, s_v = _trunk(x_sample, state_C, state_n, state_m, cache_k, cache_v, params,
                                          is_prompt=False, kv_rows=x_sample.shape[1])
    return (y_p, y_s, p_c, p_n, p_m, p_k, p_v, s_c, s_n, s_m, s_k, s_v)
```

```python
import functools

import jax
import jax.numpy as jnp
from jax import lax
from jax.experimental import pallas as pl
from jax.experimental.pallas import tpu as pltpu

F32 = jnp.float32
BF16 = jnp.bfloat16

CHUNK = 64
WINDOW = 128
M_HEADS = 8
M_DK = 64
M_DV = 128
N_Q_HEADS = 16
N_KV_HEADS = 2
HEAD_DIM = 64
GROUP = N_Q_HEADS // N_KV_HEADS
LN_EPS = 1e-5
HEAD_NORM_EPS = 1e-6

LANES = 128
LOG2_E = 1.4426950408889634
VMEM_LIMIT_BYTES = 56 * 1024 * 1024

NEG_INF = float("-inf")
HIGHEST = lax.Precision.HIGHEST


def _const_spec(shape):
    nd = len(shape)
    return pl.BlockSpec(shape, lambda *_: (0,) * nd, pipeline_mode=pl.Buffered(1))


def _layer_norm(y, g, b):
    mu = jnp.mean(y, axis=-1, keepdims=True)
    yc = y - mu
    var = jnp.mean(yc * yc, axis=-1, keepdims=True)
    return yc * lax.rsqrt(var + LN_EPS) * g + b


def _dot(a, b):
    return jnp.dot(a, b, preferred_element_type=F32)


def _dot_nt(a, b):
    return lax.dot_general(a, b, (((1,), (1,)), ((), ())), preferred_element_type=F32)


def _dot_tn(a, b):
    return lax.dot_general(a, b, (((0,), (0,)), ((), ())), preferred_element_type=F32)


FF_BLOCK = 256


def _ffn_kernel(*refs, alpha, d_ff, sub, with_kv, with_vt):
    x_ref, wgu_ref, wd_ref, g_ref, b_ref = refs[:5]
    rest = list(refs[5:])
    wkv_ref = rest.pop(0) if with_kv else None
    wvt_ref = rest.pop(0) if with_vt else None
    o_ref = rest.pop(0)
    kv_ref = rest.pop(0) if with_kv else None
    vt_ref = rest.pop(0) if with_vt else None
    acc_ref = rest.pop(0)
    h_ref = rest.pop(0)
    tm = x_ref.shape[0]

    n_fin = 4

    def finish(i, part):
        step = sub // n_fin
        rs = slice(i * sub + part * step, i * sub + (part + 1) * step)
        y = _layer_norm(alpha * x_ref[rs, :] + acc_ref[rs, :], g_ref[...], b_ref[...])
        o_ref[rs, :] = y
        if with_kv:
            yb = y.astype(BF16)
            kv_ref[rs, :] = _dot(yb, wkv_ref[...])
            if with_vt:
                vt_ref[:, rs] = _dot_nt(wvt_ref[...], yb).astype(BF16)

    for i in range(tm // sub):
        rs = slice(i * sub, (i + 1) * sub)
        xb = x_ref[rs, :].astype(BF16)
        for j in range(d_ff // FF_BLOCK):
            sl = slice(j * FF_BLOCK, (j + 1) * FF_BLOCK)
            su = slice(d_ff + j * FF_BLOCK, d_ff + (j + 1) * FF_BLOCK)
            g = _dot(xb, wgu_ref[:, sl])
            u = _dot(xb, wgu_ref[:, su])
            h_ref[rs, sl] = (g * jax.nn.sigmoid(g) * u).astype(BF16)
            if i > 0 and j % 2 == 0 and j // 2 < n_fin:
                finish(i - 1, j // 2)
        acc_ref[rs, :] = _dot(h_ref[rs, :], wd_ref[...])
    for part in range(n_fin):
        finish(tm // sub - 1, part)


def _layer_spec(shape, layer):
    nd = len(shape) - 1
    return pl.BlockSpec((None,) + tuple(shape[1:]), lambda *_: (layer,) + (0,) * nd, pipeline_mode=pl.Buffered(1))


def _ffn(x2d, w_gu, w_down, ln_g, ln_b, *, layer, alpha, tm, sub, w_kv=None, with_vt=False):
    n, d = x2d.shape
    d_ff = w_down.shape[1]
    assert n % tm == 0 and tm % sub == 0 and d_ff % FF_BLOCK == 0 and w_gu.shape[2] == 2 * d_ff
    with_kv = w_kv is not None
    assert with_kv or not with_vt
    args = [x2d, w_gu, w_down, ln_g, ln_b]
    in_specs = [pl.BlockSpec((tm, d), lambda i: (i, 0)), _layer_spec(w_gu.shape, layer),
                _layer_spec(w_down.shape, layer), _const_spec(ln_g.shape), _const_spec(ln_b.shape)]
    out_specs = [pl.BlockSpec((tm, d), lambda i: (i, 0))]
    out_shape = [jax.ShapeDtypeStruct((n, d), F32)]
    if with_kv:
        nk = w_kv.shape[1]
        args.append(w_kv.astype(BF16))
        in_specs.append(_const_spec(w_kv.shape))
        out_specs.append(pl.BlockSpec((tm, nk), lambda i: (i, 0)))
        out_shape.append(jax.ShapeDtypeStruct((n, nk), F32))
        if with_vt:
            w_vt = w_kv[:, nk // 2:].T.astype(BF16)
            args.append(w_vt)
            in_specs.insert(len(in_specs), _const_spec(w_vt.shape))
            out_specs.append(pl.BlockSpec((nk // 2, tm), lambda i: (0, i)))
            out_shape.append(jax.ShapeDtypeStruct((nk // 2, n), BF16))
    outs = pl.pallas_call(
        functools.partial(_ffn_kernel, alpha=alpha, d_ff=d_ff, sub=sub, with_kv=with_kv, with_vt=with_vt),
        grid=(n // tm,),
        in_specs=in_specs,
        out_specs=out_specs,
        out_shape=out_shape,
        scratch_shapes=[pltpu.VMEM((tm, d), F32), pltpu.VMEM((tm, d_ff), BF16)],
        compiler_params=pltpu.CompilerParams(
            dimension_semantics=("arbitrary",), vmem_limit_bytes=VMEM_LIMIT_BYTES),
        name="ffn_kv" if with_kv else "ffn",
    )(*args)
    return outs[0], (outs[1] if with_kv else None), (outs[2] if with_vt else None)


def _kv_kernel(x_ref, w_ref, o_ref):
    o_ref[...] = _dot(x_ref[...].astype(BF16), w_ref[...])


def _kv_t_kernel(x_ref, w_ref, wvt_ref, o_ref, vt_ref):
    xb = x_ref[...].astype(BF16)
    o_ref[...] = _dot(xb, w_ref[...])
    vt_ref[...] = _dot_nt(wvt_ref[...], xb).astype(BF16)


def _kv_proj(x2d, w_kv, *, tm, with_vt):
    n, d = x2d.shape
    nk = w_kv.shape[1]
    assert n % tm == 0
    params = pltpu.CompilerParams(dimension_semantics=("arbitrary",), vmem_limit_bytes=VMEM_LIMIT_BYTES)
    x_spec = pl.BlockSpec((tm, d), lambda i: (i, 0))
    kv_spec = pl.BlockSpec((tm, nk), lambda i: (i, 0))
    w_kv_b = w_kv.astype(BF16)
    if not with_vt:
        kv = pl.pallas_call(
            _kv_kernel, grid=(n // tm,), in_specs=[x_spec, _const_spec(w_kv.shape)], out_specs=kv_spec,
            out_shape=jax.ShapeDtypeStruct((n, nk), F32), compiler_params=params, name="kv_proj",
        )(x2d, w_kv_b)
        return kv, None
    nv = nk // 2
    w_vt = w_kv[:, nv:].T.astype(BF16)
    return pl.pallas_call(
        _kv_t_kernel,
        grid=(n // tm,),
        in_specs=[x_spec, _const_spec(w_kv.shape), _const_spec(w_vt.shape)],
        out_specs=[kv_spec, pl.BlockSpec((nv, tm), lambda i: (0, i))],
        out_shape=[jax.ShapeDtypeStruct((n, nk), F32), jax.ShapeDtypeStruct((nv, n), BF16)],
        compiler_params=params,
        name="kv_proj_t",
    )(x2d, w_kv_b, w_vt)


A_QOFF = 0
A_KOFF = M_HEADS * LANES
A_VOFF = 2 * M_HEADS * LANES
A_OOFF = 3 * M_HEADS * LANES
A_GOFF = 4 * M_HEADS * LANES
A_COLS = A_GOFF + 2 * LANES
A_COL_BLOCK = 512


def _mlstm_kernel(x_ref, c0_ref, n0_ref, m0_ref, w_ref, bg_ref, gn_ref, wo_ref, lg_ref, lb_ref,
                  y_ref, cout_ref, mout_ref,
                  q_s, k_s, v_s, o_s, g_s, h_s, hb_s, c_s, m_s,
                  *, alpha, L, bb_n, tt, n_j):
    j = pl.program_id(1)
    rows = bb_n * tt
    units_per_b = tt // L
    hd = M_HEADS * LANES

    @pl.when(j == 0)
    def _():
        lane = lax.broadcasted_iota(jnp.int32, (1, 1, 1, LANES), 3)
        c_s[:, :, :, 0:LANES] = c0_ref[...]
        c_s[:, :, :, LANES:2 * LANES] = jnp.where(lane == 0, n0_ref[...], 0.0)
        m_s[...] = m0_ref[...]

    x = x_ref[...]
    xb = x.astype(BF16)
    for cb in range(0, hd, A_COL_BLOCK):
        q_s[:, cb:cb + A_COL_BLOCK] = _dot(xb, w_ref[:, A_QOFF + cb:A_QOFF + cb + A_COL_BLOCK]).astype(BF16)
        k_s[:, cb:cb + A_COL_BLOCK] = _dot(xb, w_ref[:, A_KOFF + cb:A_KOFF + cb + A_COL_BLOCK])
        v_s[:, cb:cb + A_COL_BLOCK] = _dot(xb, w_ref[:, A_VOFF + cb:A_VOFF + cb + A_COL_BLOCK]).astype(BF16)
        o_s[:, cb:cb + A_COL_BLOCK] = _dot(xb, w_ref[:, A_OOFF + cb:A_OOFF + cb + A_COL_BLOCK])
    g_s[...] = _dot(xb, w_ref[:, A_GOFF:A_GOFF + 2 * LANES]) + bg_ref[...]

    row_i = lax.broadcasted_iota(jnp.int32, (L, L), 0)
    col_i = lax.broadcasted_iota(jnp.int32, (L, L), 1)
    causal = row_i >= col_i
    tri = causal.astype(F32)
    ones_col = jnp.where(lax.broadcasted_iota(jnp.int32, (L, LANES), 1) == 0, 1.0, 0.0).astype(BF16)

    def unit(u, carry):
        r0 = pl.multiple_of(u * L, L)
        bb = u // units_per_b if bb_n > 1 else 0
        gi = g_s[pl.ds(r0, L), 0:LANES]
        gf = g_s[pl.ds(r0, L), LANES:2 * LANES]
        lf = jnp.minimum(gf, 0.0) - jnp.log1p(jnp.exp(-jnp.abs(gf)))
        bc = lax.dot_general(tri, lf, (((1,), (0,)), ((), ())), precision=HIGHEST,
                             preferred_element_type=F32)
        rs = gi - bc
        rs_t = rs.T
        b_last = bc[L - 1:L, :]
        heads = range(M_HEADS)
        hs_of = lambda h: slice(h * LANES, (h + 1) * LANES)
        qs = [q_s[pl.ds(r0, L), hs_of(h)] for h in heads]
        ks = [k_s[pl.ds(r0, L), hs_of(h)] for h in heads]
        vas = [jnp.concatenate([v_s[pl.ds(r0, L), hs_of(h)], ones_col], axis=1) for h in heads]
        m0s = [m_s[bb, h, 0:1, 0:1] for h in heads]
        cs = [c_s[bb, h] for h in heads]
        ss = [_dot_nt(qs[h], ks[h].astype(BF16)) for h in heads]
        qcs = [_dot(qs[h][:, 0:M_DK], cs[h].astype(BF16)) for h in heads]
        b_cols = [bc[:, h:h + 1] for h in heads]
        logds = [jnp.where(causal, b_cols[h] + rs_t[h:h + 1, :], NEG_INF) for h in heads]
        inters = [b_cols[h] + m0s[h] for h in heads]
        mxs = [jnp.maximum(inters[h], jnp.max(logds[h], axis=1, keepdims=True)) for h in heads]
        ps = [(ss[h] * jnp.exp(logds[h] - mxs[h])).astype(BF16) for h in heads]
        nds = [jnp.exp(inters[h] - mxs[h]) * qcs[h] + _dot(ps[h], vas[h]) for h in heads]
        for h in heads:
            num = nds[h][:, 0:M_DV]
            den = nds[h][:, M_DV:M_DV + 1]
            h_s[pl.ds(r0, L), hs_of(h)] = num / jnp.maximum(jnp.abs(den), jnp.exp(-mxs[h]))
        m_news = [mxs[h][L - 1:L, :] for h in heads]
        bls = [b_last[:, h:h + 1] for h in heads]
        wks = [(ks[h][:, 0:M_DK] * jnp.exp(rs[:, h:h + 1] + (bls[h] - m_news[h]))).astype(BF16) for h in heads]
        upds = [_dot_tn(wks[h], vas[h]) for h in heads]
        for h in heads:
            c_s[bb, h] = jnp.exp(bls[h] + m0s[h] - m_news[h]) * cs[h] + upds[h]
            m_s[bb, h] = jnp.broadcast_to(m_news[h], (8, LANES))
        return carry

    lax.fori_loop(0, rows // L, unit, 0)

    for h in range(M_HEADS):
        hs = slice(h * LANES, (h + 1) * LANES)
        hh = h_s[:, hs]
        hn = hh * lax.rsqrt(jnp.mean(hh * hh, axis=-1, keepdims=True) + HEAD_NORM_EPS)
        hb_s[:, hs] = (hn * gn_ref[:, hs] * jax.nn.sigmoid(o_s[:, hs])).astype(BF16)
    mix = _dot(hb_s[...], wo_ref[...])
    y_ref[...] = _layer_norm(alpha * x + mix, lg_ref[...], lb_ref[...])

    @pl.when(j == n_j - 1)
    def _():
        cout_ref[...] = c_s[...]
        mout_ref[...] = m_s[...]


def _mlstm_layer(x2d, c0, n0, m0, w_all, b_gate, g_norm, w_out, ln_g, ln_b, *, alpha, batch, seq, L, bb_n, tt):
    n, d = x2d.shape
    assert n == batch * seq and seq % tt == 0 and tt % L == 0 and batch % bb_n == 0
    assert bb_n == 1 or tt == seq
    n_j = seq // tt
    rows = bb_n * tt
    hd = M_HEADS * LANES
    kern = functools.partial(_mlstm_kernel, alpha=alpha, L=L, bb_n=bb_n, tt=tt, n_j=n_j)
    st4 = lambda bi, j: (bi, 0, 0, 0)
    y, c_out, m_out = pl.pallas_call(
        kern,
        grid=(batch // bb_n, n_j),
        in_specs=[
            pl.BlockSpec((rows, d), lambda bi, j: (bi * n_j + j, 0)),
            pl.BlockSpec((bb_n, M_HEADS, M_DK, M_DV), st4),
            pl.BlockSpec((bb_n, M_HEADS, M_DK, 1), st4),
            pl.BlockSpec((bb_n, M_HEADS, 8, LANES), st4),
            _const_spec(w_all.shape),
            _const_spec(b_gate.shape),
            _const_spec(g_norm.shape),
            _const_spec(w_out.shape),
            _const_spec(ln_g.shape),
            _const_spec(ln_b.shape),
        ],
        out_specs=[
            pl.BlockSpec((rows, d), lambda bi, j: (bi * n_j + j, 0)),
            pl.BlockSpec((bb_n, M_HEADS, M_DK, 2 * LANES), st4),
            pl.BlockSpec((bb_n, M_HEADS, 8, LANES), st4),
        ],
        out_shape=[
            jax.ShapeDtypeStruct((n, d), F32),
            jax.ShapeDtypeStruct((batch, M_HEADS, M_DK, 2 * LANES), F32),
            jax.ShapeDtypeStruct((batch, M_HEADS, 8, LANES), F32),
        ],
        scratch_shapes=[
            pltpu.VMEM((rows, hd), BF16),
            pltpu.VMEM((rows, hd), F32),
            pltpu.VMEM((rows, hd), BF16),
            pltpu.VMEM((rows, hd), F32),
            pltpu.VMEM((rows, 2 * LANES), F32),
            pltpu.VMEM((rows, hd), F32),
            pltpu.VMEM((rows, hd), BF16),
            pltpu.VMEM((bb_n, M_HEADS, M_DK, 2 * LANES), F32),
            pltpu.VMEM((bb_n, M_HEADS, 8, LANES), F32),
        ],
        compiler_params=pltpu.CompilerParams(
            dimension_semantics=("arbitrary", "arbitrary"), vmem_limit_bytes=VMEM_LIMIT_BYTES),
        name="mlstm_layer",
    )(x2d, c0, n0, m0, w_all, b_gate, g_norm, w_out, ln_g, ln_b)
    c_new = c_out[..., 0:M_DV]
    n_new = c_out[..., M_DV]
    m_new = m_out[:, :, 0, 0]
    return y, c_new, n_new, m_new


def _prep_mlstm_weights(w_in, b_gate, g_norm, w_out):
    d = w_in.shape[0]
    hk = M_HEADS * M_DK
    hv = M_HEADS * M_DV
    wq = w_in[:, 0:hk].reshape(d, M_HEADS, M_DK)
    wk = w_in[:, hk:2 * hk].reshape(d, M_HEADS, M_DK) * (M_DK ** -0.5)
    pad = ((0, 0), (0, 0), (0, LANES - M_DK))
    wq = jnp.pad(wq, pad).reshape(d, M_HEADS * LANES)
    wk = jnp.pad(wk, pad).reshape(d, M_HEADS * LANES)
    wv = w_in[:, 2 * hk:2 * hk + hv]
    wo = w_in[:, 2 * hk + hv:2 * hk + 2 * hv]
    wi = jnp.pad(w_in[:, 2 * hk + 2 * hv:2 * hk + 2 * hv + M_HEADS], ((0, 0), (0, LANES - M_HEADS)))
    wf = jnp.pad(w_in[:, 2 * hk + 2 * hv + M_HEADS:], ((0, 0), (0, LANES - M_HEADS)))
    w_all = jnp.concatenate([wq, wk, wv, wo, wi, wf], axis=1).astype(BF16)
    bg = jnp.concatenate([jnp.pad(b_gate[0:M_HEADS], (0, LANES - M_HEADS)),
                          jnp.pad(b_gate[M_HEADS:], (0, LANES - M_HEADS))]).astype(F32)[None, :]
    return w_all, bg, g_norm.astype(F32)[None, :], w_out.astype(BF16)


P_L = LANES
P_AUG = M_DV + 16
P_GATE_COPIES = 3


P_SECTION = 256


def _mlstm_pipe_kernel(x_ref, wqt_ref, wk_ref, wvt_ref, wot_ref, wg_ref, bg_ref, gn_ref, wout_ref,
                       lg_ref, lb_ref, y_ref, cout_ref, mout_ref,
                       xb_s, qt_s, k_s, vt_s, ot_s, g_s, ht_s, hn_s, c_s, m_s, *, alpha, tt, n_j):
    j = pl.program_id(1)
    L = P_L
    H = M_HEADS
    hv = H * M_DV
    d = x_ref.shape[1]
    section = P_SECTION
    n_sec = tt // section
    units = tt // L
    rows_of = lambda sec: slice(sec * section, (sec + 1) * section)

    @pl.when(j == 0)
    def _():
        c_s[...] = jnp.zeros_like(c_s)
        m_s[...] = jnp.zeros_like(m_s)

    row_i = lax.broadcasted_iota(jnp.int32, (L, LANES), 0)
    lane_i = lax.broadcasted_iota(jnp.int32, (L, LANES), 1)
    tri = (row_i >= lane_i).astype(F32)
    key8 = lax.broadcasted_iota(jnp.int32, (L, H * L), 0)
    qry8 = lax.broadcasted_iota(jnp.int32, (L, H * L), 1) & (L - 1)
    causal8 = key8 <= qry8
    ones_rows = jnp.ones((P_AUG - M_DV, L), F32)
    gn = gn_ref[...]

    zeros_dk = jnp.zeros((M_DK, L), BF16)

    def k_pair(h, ts):
        return k_s[ts, (h // 2) * LANES:(h // 2 + 1) * LANES]

    def pad_head(h, a):
        return jnp.concatenate([a, zeros_dk] if h % 2 == 0 else [zeros_dk, a], axis=0)

    def proj_pieces(sec):
        rs = rows_of(sec)

        def q_and_gates():
            xb_s[rs, :] = x_ref[rs, :].astype(BF16)
            qt_s[:, rs] = _dot_nt(wqt_ref[...], xb_s[rs, :]).astype(BF16)
            g_s[rs, :] = _dot(xb_s[rs, :], wg_ref[...]) + bg_ref[...]

        def keys():
            k_s[rs, :] = _dot(xb_s[rs, :], wk_ref[...]).astype(BF16)

        def vt_block(cb):
            vt_s[cb:cb + A_COL_BLOCK, rs] = _dot_nt(wvt_ref[cb:cb + A_COL_BLOCK, :], xb_s[rs, :])

        def ot_block(cb):
            ot_s[cb:cb + A_COL_BLOCK, rs] = _dot_nt(wot_ref[cb:cb + A_COL_BLOCK, :], xb_s[rs, :])

        blocks = range(0, hv, A_COL_BLOCK)
        return ([q_and_gates, keys],
                [functools.partial(f, cb) for f in (vt_block, ot_block) for cb in blocks])

    def post_pieces(sec):
        rs = rows_of(sec)

        def norm_gate(heads):
            for h in heads:
                hs = slice(h * M_DV, (h + 1) * M_DV)
                hh = ht_s[hs, rs]
                scale = lax.rsqrt(jnp.mean(hh * hh, axis=0, keepdims=True) + HEAD_NORM_EPS)
                gcol = jnp.concatenate([gn[hs, :]] * (section // LANES), axis=1)
                hn_s[hs, rs] = (hh * scale * gcol * jax.nn.sigmoid(ot_s[hs, rs])).astype(BF16)

        def out_block(i):
            blk = slice(i * (d // 4), (i + 1) * (d // 4))
            y_ref[rs, blk] = _dot_tn(hn_s[:, rs], wout_ref[:, blk])

        def deep_norm():
            y_ref[rs, :] = _layer_norm(alpha * x_ref[rs, :] + y_ref[rs, :], lg_ref[...], lb_ref[...])

        return ([functools.partial(norm_gate, range(0, H // 2)), functools.partial(norm_gate, range(H // 2, H)),
                 functools.partial(out_block, 0), functools.partial(out_block, 1)],
                [functools.partial(out_block, 2), functools.partial(out_block, 3), deep_norm])

    def pre(u):
        ts = slice(u * L, (u + 1) * L)
        gi = g_s[ts, 0:LANES]
        gf = g_s[ts, LANES:2 * LANES]
        lf = jnp.minimum(gf, 0.0) - jnp.log1p(jnp.exp(-jnp.abs(gf)))
        bc = lax.dot_general(tri, lf, (((1,), (0,)), ((), ())), precision=HIGHEST,
                             preferred_element_type=F32)
        rs = gi - bc
        cm = rs
        k = 1
        while k < L:
            cm = jnp.maximum(cm, jnp.where(row_i >= k, pltpu.roll(cm, k, 0), NEG_INF))
            k *= 2
        packed = jnp.where(lane_i < H, rs, jnp.where(lane_i < 2 * H, cm, bc))
        rows = packed.T[0:3 * H]
        rs_cat = jnp.concatenate([jnp.broadcast_to(rs[:, h:h + 1], (L, L)) for h in range(H)], axis=1)
        e_cat = jnp.where(causal8, rs_cat, NEG_INF)
        s_cat = jnp.concatenate([_dot(k_pair(h, ts), pad_head(h, qt_s[h * M_DK:(h + 1) * M_DK, ts]))
                                 for h in range(H)], axis=1)
        return rows, e_cat, s_cat

    def rec(u, pre_u, fillers):
        ts = slice(u * L, (u + 1) * L)
        rows, e_cat, s_cat = pre_u
        rs_t = rows[0:H]
        cm_t = rows[H:2 * H]
        b_t = rows[2 * H:3 * H]
        m0 = m_s[...]
        a_t = jnp.maximum(m0, cm_t)
        w_inter = jnp.exp(m0 - a_t)
        emx = jnp.exp(-(b_t + a_t))
        m_new = jnp.broadcast_to((b_t + a_t)[:, L - 1:L], (H, L))
        b_last = jnp.broadcast_to(b_t[:, L - 1:L], (H, L))
        w_end = jnp.exp(rs_t + (b_last - m_new))
        decay = jnp.exp(b_last + m0 - m_new)
        m_s[...] = m_new
        a_cat = jnp.concatenate([a_t[h:h + 1, :] for h in range(H)], axis=1)
        p_cat = (s_cat * jnp.exp(e_cat - a_cat)).astype(BF16)
        for h in range(H):
            if fillers:
                fillers.pop(0)()
            va = jnp.concatenate([vt_s[h * M_DV:(h + 1) * M_DV, ts], ones_rows], axis=0)
            ct = c_s[h]
            qw = (qt_s[h * M_DK:(h + 1) * M_DK, ts].astype(F32) * w_inter[h:h + 1, :]).astype(BF16)
            lhs = jnp.concatenate([va.astype(BF16), ct.astype(BF16)], axis=1)
            rhs = jnp.concatenate([p_cat[:, h * L:(h + 1) * L], pad_head(h, qw)], axis=0)
            nd = _dot(lhs, rhs)
            inv = 1.0 / jnp.maximum(jnp.abs(nd[M_DV:M_DV + 1, :]), emx[h:h + 1, :])
            ht_s[h * M_DV:(h + 1) * M_DV, ts] = nd[0:M_DV, :] * inv
            upd = _dot((va * w_end[h:h + 1, :]).astype(BF16), k_pair(h, ts))
            c_s[h] = decay[h:h + 1, :] * ct + upd
        while fillers:
            fillers.pop(0)()

    first, rest = proj_pieces(0)
    for piece in first + rest:
        piece()
    pre_next = pre(0)
    for u in range(units):
        sec, slot = divmod(u, section // L)
        fillers = []
        if sec + 1 < n_sec:
            fillers += proj_pieces(sec + 1)[slot]
        if sec >= 1:
            fillers += post_pieces(sec - 1)[slot]
        pre_cur = pre_next
        if u + 1 < units:
            pre_next = pre(u + 1)
        rec(u, pre_cur, fillers)
    for half in post_pieces(n_sec - 1):
        for piece in half:
            piece()

    @pl.when(j == n_j - 1)
    def _():
        cout_ref[0] = c_s[...]
        mout_ref[0] = m_s[...]


def _mlstm_prompt_layer(x2d, w_in, b_gate, g_norm, w_out, ln_g, ln_b, *, alpha, batch, seq, tt):
    n, d = x2d.shape
    assert n == batch * seq and seq % tt == 0 and tt % P_SECTION == 0 and P_SECTION == 2 * P_L
    assert P_GATE_COPIES * M_HEADS <= LANES and d % 4 == 0
    n_j = seq // tt
    hk = M_HEADS * M_DK
    hv = M_HEADS * M_DV
    wqt = w_in[:, 0:hk].T.astype(BF16)
    wk = (w_in[:, hk:2 * hk] * (M_DK ** -0.5)).astype(BF16)
    wvt = w_in[:, 2 * hk:2 * hk + hv].T.astype(BF16)
    wot = w_in[:, 2 * hk + hv:2 * hk + 2 * hv].T.astype(BF16)
    rep = lambda a: jnp.pad(jnp.tile(a, (1, P_GATE_COPIES)), ((0, 0), (0, LANES - P_GATE_COPIES * M_HEADS)))
    g0 = 2 * hk + 2 * hv
    wg = jnp.concatenate([rep(w_in[:, g0:g0 + M_HEADS]), rep(w_in[:, g0 + M_HEADS:])], axis=1).astype(BF16)
    bg = jnp.concatenate([rep(b_gate[None, 0:M_HEADS]), rep(b_gate[None, M_HEADS:])], axis=1).astype(F32)
    gn = jnp.broadcast_to(g_norm.astype(F32)[:, None], (hv, LANES))
    wout = w_out.astype(BF16)
    consts = (wqt, wk, wvt, wot, wg, bg, gn, wout, ln_g, ln_b)
    y, c_out, m_out = pl.pallas_call(
        functools.partial(_mlstm_pipe_kernel, alpha=alpha, tt=tt, n_j=n_j),
        grid=(batch, n_j),
        in_specs=[pl.BlockSpec((tt, d), lambda b, j: (b * n_j + j, 0))] + [_const_spec(a.shape) for a in consts],
        out_specs=[
            pl.BlockSpec((tt, d), lambda b, j: (b * n_j + j, 0)),
            pl.BlockSpec((1, M_HEADS, P_AUG, 2 * M_DK), lambda b, j: (b, 0, 0, 0)),
            pl.BlockSpec((1, M_HEADS, LANES), lambda b, j: (b, 0, 0)),
        ],
        out_shape=[
            jax.ShapeDtypeStruct((n, d), F32),
            jax.ShapeDtypeStruct((batch, M_HEADS, P_AUG, 2 * M_DK), F32),
            jax.ShapeDtypeStruct((batch, M_HEADS, LANES), F32),
        ],
        scratch_shapes=[
            pltpu.VMEM((tt, d), BF16),
            pltpu.VMEM((hk, tt), BF16),
            pltpu.VMEM((tt, hk), BF16),
            pltpu.VMEM((hv, tt), F32),
            pltpu.VMEM((hv, tt), F32),
            pltpu.VMEM((tt, 2 * LANES), F32),
            pltpu.VMEM((hv, tt), F32),
            pltpu.VMEM((hv, tt), BF16),
            pltpu.VMEM((M_HEADS, P_AUG, 2 * M_DK), F32),
            pltpu.VMEM((M_HEADS, LANES), F32),
        ],
        compiler_params=pltpu.CompilerParams(
            dimension_semantics=("arbitrary", "arbitrary"), vmem_limit_bytes=VMEM_LIMIT_BYTES),
        name="mlstm_prompt",
    )(x2d, *consts)
    ct = jnp.stack([c_out[:, 0::2, :, 0:M_DK], c_out[:, 1::2, :, M_DK:]], axis=2)
    ct = ct.reshape(batch, M_HEADS, P_AUG, M_DK)
    c_new = jnp.swapaxes(ct[:, :, 0:M_DV, :], -1, -2)
    n_new = ct[:, :, M_DV, :]
    m_new = m_out[:, :, 0]
    return y, c_new, n_new, m_new


PAIRS = GROUP // 2


def _attn_units(units):
    halves = [(q, kv[i], kv[2 + i], bias[i], sink[i]) for q, kv, bias, sink in units for i in range(2)]
    ss = [_dot_nt(q, k_op) + bias for q, k_op, _, bias, _ in halves]
    mxs = [jnp.maximum(jnp.max(s, axis=-1, keepdims=True), hf[4]) for s, hf in zip(ss, halves)]
    ps = [jnp.exp(s - mx) for s, mx in zip(ss, mxs)]
    dens = [jnp.sum(p, axis=-1, keepdims=True) + jnp.exp(hf[4] - mx) for p, mx, hf in zip(ps, mxs, halves)]
    outs = [_dot(p.astype(BF16), hf[2]) / den for p, den, hf in zip(ps, dens, halves)]
    return [outs[2 * i] + outs[2 * i + 1] for i in range(len(units))]


def _split_kv(kv):
    lane = lax.broadcasted_iota(jnp.int32, (1, LANES), 1)
    low = lane < HEAD_DIM
    res = []
    kk = kv[:, 0:LANES]
    vv = kv[:, LANES:2 * LANES]
    kk_r = pltpu.roll(kk, HEAD_DIM, 1)
    vv_r = pltpu.roll(vv, HEAD_DIM, 1)
    z = jnp.zeros_like(kk)
    res.append((jnp.where(low, kk, z), jnp.where(low, z, kk_r), jnp.where(low, vv, z), jnp.where(low, z, vv_r)))
    res.append((jnp.where(low, kk_r, z), jnp.where(low, z, kk), jnp.where(low, vv_r, z), jnp.where(low, z, vv)))
    return [tuple(a.astype(BF16) for a in grp) for grp in res]


SWA_UNIT = 2 * CHUNK
SWA_KEYS = WINDOW + SWA_UNIT
SWA_SECTION = 256


def _swa_prompt_kernel(x_ref, kc_ref, kp_ref, vtc_ref, vtp_ref, wqt_ref, wo_ref, bias_ref, sink_ref,
                       lg_ref, lb_ref, y_ref, xb_s, qt_s, o_s, k_s, vt_s, *, alpha, tq, section):
    j = pl.program_id(1)
    k_s[0:WINDOW, :] = kp_ref[...].astype(BF16)
    k_s[WINDOW:WINDOW + tq, :] = kc_ref[...].astype(BF16)
    vt_s[:, 0:WINDOW] = vtp_ref[...]
    vt_s[:, WINDOW:WINDOW + tq] = vtc_ref[...]
    zeros = jnp.zeros((HEAD_DIM, GROUP * SWA_UNIT), BF16)
    key_i = lax.broadcasted_iota(jnp.int32, (SWA_KEYS, GROUP * SWA_UNIT), 0)
    d = x_ref.shape[1]
    n_sec = tq // section
    per_sec = (section // SWA_UNIT) * N_KV_HEADS
    rows_of = lambda sec: slice(sec * section, (sec + 1) * section)

    def project(sec, piece):
        rs = rows_of(sec)
        if piece == 0:
            xb_s[rs, :] = x_ref[rs, :].astype(BF16)
        blk = slice(piece * (d // per_sec), (piece + 1) * (d // per_sec))
        qt_s[blk, rs] = _dot_nt(wqt_ref[blk, :], xb_s[rs, :]).astype(BF16)

    def out_project(sec, piece):
        rs = rows_of(sec)
        blk = slice(piece * (d // per_sec), (piece + 1) * (d // per_sec))
        y_ref[rs, blk] = _dot(o_s[rs, :], wo_ref[:, blk])

    def normalise(sec):
        rs = rows_of(sec)
        y_ref[rs, :] = _layer_norm(alpha * x_ref[rs, :] + y_ref[rs, :], lg_ref[...], lb_ref[...])

    hb = GROUP // 2
    hw = hb * SWA_UNIT
    zeros_h = jnp.zeros((HEAD_DIM, hw), BF16)
    key_h = lax.broadcasted_iota(jnp.int32, (SWA_KEYS, hw), 0)

    def scores(u, g):
        r0 = u * SWA_UNIT
        blocks = []
        for half in range(2):
            qt_g = jnp.concatenate(
                [qt_s[h * HEAD_DIM:(h + 1) * HEAD_DIM, r0:r0 + SWA_UNIT]
                 for h in range(g * GROUP + half * hb, g * GROUP + (half + 1) * hb)], axis=1)
            qz = jnp.concatenate([qt_g, zeros_h] if g == 0 else [zeros_h, qt_g], axis=0)
            s_t = _dot(k_s[r0:r0 + SWA_KEYS, :], qz) + bias_ref[g, :, half * hw:(half + 1) * hw]
            if r0 < WINDOW:
                s_t = jnp.where(j * tq + r0 - WINDOW + key_h >= 0, s_t, NEG_INF)
            blocks.append(s_t)
        return blocks

    def finish(u, g, blocks):
        r0 = u * SWA_UNIT
        vt_g = vt_s[g * HEAD_DIM:(g + 1) * HEAD_DIM, r0:r0 + SWA_KEYS]
        for half, s_t in enumerate(blocks):
            sink = sink_ref[g, :, half * hw:(half + 1) * hw]
            mx = jnp.maximum(jnp.max(s_t, axis=0, keepdims=True), sink)
            p = jnp.exp2(s_t - mx)
            den = jnp.sum(p, axis=0, keepdims=True) + jnp.exp2(sink - mx)
            o_t = (_dot(vt_g, p.astype(BF16)) / den).astype(BF16)
            for i in range(0, hb, 2):
                pair = jnp.concatenate([o_t[:, i * SWA_UNIT:(i + 1) * SWA_UNIT],
                                        o_t[:, (i + 1) * SWA_UNIT:(i + 2) * SWA_UNIT]], axis=0)
                c0 = (g * GROUP + half * hb + i) * HEAD_DIM
                o_s[r0:r0 + SWA_UNIT, c0:c0 + 2 * HEAD_DIM] = pair.T

    todo = [(u, g) for u in range(tq // SWA_UNIT) for g in range(N_KV_HEADS)]
    for piece in range(per_sec):
        project(0, piece)
    s_next = scores(*todo[0])
    for idx, (u, g) in enumerate(todo):
        sec, slot = divmod(idx, per_sec)
        if sec + 1 < n_sec:
            project(sec + 1, slot)
        if sec >= 1:
            out_project(sec - 1, slot)
        s_cur = s_next
        if idx + 1 < len(todo):
            s_next = scores(*todo[idx + 1])
        finish(u, g, s_cur)
        if sec >= 1 and slot == per_sec - 1:
            normalise(sec - 1)
    for piece in range(per_sec):
        out_project(n_sec - 1, piece)
    normalise(n_sec - 1)


def _swa_prompt_layer(x2d, kv2d, vt, wqt, wo, bias, sinks, ln_g, ln_b, *, alpha, batch, seq, tq):
    n, d = x2d.shape
    section = SWA_SECTION
    assert n == batch * seq and seq % tq == 0 and tq % section == 0 and section % SWA_UNIT == 0 and WINDOW == LANES
    assert d % ((section // SWA_UNIT) * N_KV_HEADS) == 0
    n_j = seq // tq
    per = tq // WINDOW
    prev = lambda b, j: jnp.maximum((b * n_j + j) * per - 1, 0)
    return pl.pallas_call(
        functools.partial(_swa_prompt_kernel, alpha=alpha, tq=tq, section=section),
        grid=(batch, n_j),
        in_specs=[
            pl.BlockSpec((tq, d), lambda b, j: (b * n_j + j, 0)),
            pl.BlockSpec((tq, LANES), lambda b, j: (b * n_j + j, 0)),
            pl.BlockSpec((WINDOW, LANES), lambda b, j: (prev(b, j), 0)),
            pl.BlockSpec((LANES, tq), lambda b, j: (0, b * n_j + j)),
            pl.BlockSpec((LANES, WINDOW), lambda b, j: (0, prev(b, j))),
            _const_spec(wqt.shape),
            _const_spec(wo.shape),
            _const_spec(bias.shape),
            _const_spec(sinks.shape),
            _const_spec(ln_g.shape),
            _const_spec(ln_b.shape),
        ],
        out_specs=pl.BlockSpec((tq, d), lambda b, j: (b * n_j + j, 0)),
        out_shape=jax.ShapeDtypeStruct((n, d), F32),
        scratch_shapes=[
            pltpu.VMEM((tq, d), BF16),
            pltpu.VMEM((d, tq), BF16),
            pltpu.VMEM((tq, d), BF16),
            pltpu.VMEM((WINDOW + tq, LANES), BF16),
            pltpu.VMEM((LANES, WINDOW + tq), BF16),
        ],
        compiler_params=pltpu.CompilerParams(
            dimension_semantics=("arbitrary", "arbitrary"), vmem_limit_bytes=VMEM_LIMIT_BYTES),
        name="swa_prompt",
    )(x2d, kv2d, kv2d, vt, vt, wqt, wo, bias, sinks, ln_g, ln_b)


def _swa_prompt_tables(sinks):
    slopes = jnp.exp2(-8.0 * jnp.arange(1, N_Q_HEADS + 1, dtype=F32) / N_Q_HEADS)
    q = jnp.arange(SWA_UNIT)[None, :]
    kx = jnp.arange(SWA_KEYS)[:, None]
    dist = jnp.abs(q + WINDOW - kx).astype(F32)
    first = q < CHUNK
    visible = (first & (kx < WINDOW + CHUNK)) | (~first & (kx >= CHUNK))
    bias = jnp.where(visible[None], -(LOG2_E * slopes)[:, None, None] * dist[None], NEG_INF)
    sinks = sinks.astype(F32) * LOG2_E
    bias = bias.reshape(N_KV_HEADS, GROUP, SWA_KEYS, SWA_UNIT).transpose(0, 2, 1, 3)
    sink = jnp.broadcast_to(sinks.astype(F32).reshape(N_KV_HEADS, 1, GROUP, 1), (N_KV_HEADS, 1, GROUP, SWA_UNIT))
    return (bias.reshape(N_KV_HEADS, SWA_KEYS, GROUP * SWA_UNIT), sink.reshape(N_KV_HEADS, 1, GROUP * SWA_UNIT))


def _swa_sample_kernel(x_ref, kvn_ref, kc_ref, vc_ref, wq_ref, wo_ref, bias_ref, sink_ref, lg_ref, lb_ref,
                       y_ref, q_s, o_s, *, alpha, batch, seq):
    x = x_ref[...]
    xb = x.astype(BF16)
    q_s[...] = _dot(xb, wq_ref[...]).astype(BF16)
    per_pass = 2
    for b0 in range(0, batch, per_pass):
        units = []
        for b in range(b0, min(b0 + per_pass, batch)):
            r0 = b * seq
            kv_old = jnp.concatenate([kc_ref[b], vc_ref[b]], axis=1)
            groups = _split_kv(jnp.concatenate([kv_old, kvn_ref[r0:r0 + seq, :]], axis=0))
            for g in range(N_KV_HEADS):
                q_rows = jnp.concatenate(
                    [q_s[r0:r0 + seq, (g * PAIRS + pp) * LANES:(g * PAIRS + pp + 1) * LANES] for pp in range(PAIRS)],
                    axis=0)
                units.append((q_rows, groups[g], (bias_ref[g, 0], bias_ref[g, 1]), (sink_ref[g, 0], sink_ref[g, 1])))
        outs = _attn_units(units)
        for i, o in enumerate(outs):
            r0 = (b0 + i // N_KV_HEADS) * seq
            g = i % N_KV_HEADS
            for pp in range(PAIRS):
                o_s[r0:r0 + seq, (g * PAIRS + pp) * LANES:(g * PAIRS + pp + 1) * LANES] = (
                    o[pp * seq:(pp + 1) * seq, :].astype(BF16))
    mix = _dot(o_s[...], wo_ref[...])
    y_ref[...] = _layer_norm(alpha * x + mix, lg_ref[...], lb_ref[...])


def _swa_sample_layer(x2d, kv_new, k_cache, v_cache, wq, wo, bias, sink, ln_g, ln_b, *, alpha, batch, seq):
    n, d = x2d.shape
    assert n == batch * seq
    args = (x2d, kv_new, k_cache, v_cache, wq, wo, bias, sink, ln_g, ln_b)
    return pl.pallas_call(
        functools.partial(_swa_sample_kernel, alpha=alpha, batch=batch, seq=seq),
        grid=(1,),
        in_specs=[_const_spec(a.shape) for a in args],
        out_specs=pl.BlockSpec((n, d), lambda i: (0, 0)),
        out_shape=jax.ShapeDtypeStruct((n, d), F32),
        scratch_shapes=[pltpu.VMEM((n, d), BF16), pltpu.VMEM((n, d), BF16)],
        compiler_params=pltpu.CompilerParams(
            dimension_semantics=("arbitrary",), vmem_limit_bytes=VMEM_LIMIT_BYTES),
        name="swa_sample",
    )(*args)


def _attn_tables(sinks, ql, nk):
    slopes = jnp.exp2(-8.0 * jnp.arange(1, N_Q_HEADS + 1, dtype=F32) / N_Q_HEADS)
    dist = jnp.abs(jnp.arange(ql)[:, None] - jnp.arange(nk)[None, :] + (nk - ql)).astype(F32)
    head = (jnp.arange(N_KV_HEADS)[:, None, None] * GROUP + 2 * jnp.arange(PAIRS)[None, None, :]
            + jnp.arange(2)[None, :, None])
    bias = -slopes[head][..., None, None] * dist
    sink = jnp.broadcast_to(sinks.astype(F32)[head][..., None, None], head.shape + (ql, 1))
    return (bias.reshape(N_KV_HEADS, 2, PAIRS * ql, nk), sink.reshape(N_KV_HEADS, 2, PAIRS * ql, 1))


def _trunk(x, c0s, n0s, m0s, k_cache, v_cache, params, *, is_prompt, kv_rows):
    (w_in_a, b_gate_a, g_norm_a, w_out_a, w_kv, w_q_b, sinks_b, w_out_b, w_gu, w_down, ln_g, ln_b) = params
    batch, seq, d = x.shape
    depth = w_gu.shape[0]
    n_a = w_in_a.shape[0]
    alpha = (2 * depth) ** 0.25
    d_ff = w_down.shape[1]
    n = batch * seq
    x2d = x.reshape(n, d)
    tm = min(512, n)
    if is_prompt:
        L, bb_n, tt = 128, 1, min(1024, seq)
    else:
        L, bb_n, tt = seq, 8, seq
    cs, ns, ms = [], [], []
    kv2d = None
    for layer in range(depth):
        row = lambda a: a.astype(F32)[None, :]
        if layer < n_a and is_prompt:
            x2d, c, nn, m = _mlstm_prompt_layer(
                x2d, w_in_a[layer], b_gate_a[layer], g_norm_a[layer], w_out_a[layer],
                row(ln_g[layer, 0]), row(ln_b[layer, 0]), alpha=alpha, batch=batch, seq=seq, tt=tt)
            cs.append(c)
            ns.append(nn)
            ms.append(m)
        elif layer < n_a:
            w_all, bg, gn, wo = _prep_mlstm_weights(w_in_a[layer], b_gate_a[layer], g_norm_a[layer], w_out_a[layer])
            m0 = jnp.broadcast_to(m0s[layer].astype(F32)[:, :, None, None], (batch, M_HEADS, 8, LANES))
            x2d, c, nn, m = _mlstm_layer(
                x2d, c0s[layer].astype(F32), n0s[layer].astype(F32)[..., None], m0, w_all, bg, gn, wo,
                row(ln_g[layer, 0]), row(ln_b[layer, 0]), alpha=alpha, batch=batch, seq=seq, L=L, bb_n=bb_n, tt=tt)
            cs.append(c)
            ns.append(nn)
            ms.append(m)
        else:
            jb = layer - n_a
            if kv2d is None:
                kv2d, vt = _kv_proj(x2d, w_kv, tm=tm, with_vt=is_prompt)
            wq = (w_q_b[jb] * (HEAD_DIM ** -0.5)).astype(BF16)
            wo = w_out_b[jb].astype(BF16)
            if is_prompt:
                bias, sink = _swa_prompt_tables(sinks_b[jb])
                wqt = (w_q_b[jb] * (HEAD_DIM ** -0.5 * LOG2_E)).T.astype(BF16)
                x2d = _swa_prompt_layer(x2d, kv2d, vt, wqt, wo, bias, sink,
                                        row(ln_g[layer, 0]), row(ln_b[layer, 0]),
                                        alpha=alpha, batch=batch, seq=seq, tq=min(1024, seq))
            else:
                w_rows = k_cache.shape[1]
                bias, sink = _attn_tables(sinks_b[jb], seq, w_rows + seq)
                x2d = _swa_sample_layer(
                    x2d, kv2d, k_cache.astype(F32).reshape(batch, w_rows, N_KV_HEADS * HEAD_DIM),
                    v_cache.astype(F32).reshape(batch, w_rows, N_KV_HEADS * HEAD_DIM), wq, wo, bias, sink,
                    row(ln_g[layer, 0]), row(ln_b[layer, 0]), alpha=alpha, batch=batch, seq=seq)
        feeds_kv = layer + 1 == n_a and n_a < depth
        x2d, kv_new, vt_new = _ffn(
            x2d, w_gu, w_down, row(ln_g[layer, 1]), row(ln_b[layer, 1]), layer=layer, alpha=alpha,
            tm=min(2 * tm, n), sub=tm, w_kv=w_kv if feeds_kv else None, with_vt=feeds_kv and is_prompt)
        if feeds_kv:
            kv2d, vt = kv_new, vt_new
    kv = kv2d.reshape(batch, seq, kv2d.shape[1])[:, seq - kv_rows:].reshape(batch, kv_rows, 2, N_KV_HEADS, HEAD_DIM)
    return (x2d.reshape(batch, seq, d), jnp.stack(cs), jnp.stack(ns), jnp.stack(ms), kv[:, :, 0], kv[:, :, 1])


def kernel(x_prompt, x_sample, state_C, state_n, state_m, cache_k, cache_v, w_in_a, b_gate_a, g_norm_a,
           w_out_a, w_kv, w_q_b, sinks_b, w_out_b, w_gu, w_down, ln_g, ln_b):
    params = (w_in_a, b_gate_a, g_norm_a, w_out_a, w_kv, w_q_b, sinks_b, w_out_b,
              w_gu.astype(BF16), w_down.astype(BF16), ln_g, ln_b)
    y_p, p_c, p_n, p_m, p_k, p_v = _trunk(x_prompt, None, None, None, None, None, params, is_prompt=True,
                                          kv_rows=min(WINDOW, x_prompt.shape[1]))
    y_s, s_c, s_n, s_m, s_k, s_v = _trunk(x_sample, state_C, state_n, state_m, cache_k, cache_v, params,
                                          is_prompt=False, kv_rows=x_sample.shape[1])
    return (y_p, y_s, p_c, p_n, p_m, p_k, p_v, s_c, s_n, s_m, s_k, s_v)
```
